```python
import math
import jax, jax.numpy as jnp
from jax import lax
import numpy as np

D_MODEL = 4096
BATCH = 2
SEQ = 8192
DEPTH = 2

CTX_LEN = 256
GRID_W = 64
HEAD_DIM = 128
RMS_EPS = 1e-6
ROPE_THETA = 10000.0
N_MOD = 6

A_HEADS = 16
A_DK = 128
A_DV = 128
A_KEY = A_HEADS * A_DK
A_VAL = A_HEADS * A_DV
A_CHUNK = 32
A_IN = 5 * A_KEY

B_HEADS = 16
B_Q_RANK = 1024
B_KV_RANK = 512
B_NOPE = 128
B_ROPE = 64
B_VDIM = 128
B_IN = B_Q_RANK + B_KV_RANK + B_ROPE
ATTN_BLOCK = 128

AB_IN = A_IN + B_IN
AB_OUT = A_VAL + B_HEADS * B_VDIM

C_HEADS = 16
C_KV_HEADS = 4
C_GROUP = C_HEADS // C_KV_HEADS
C_WINDOW = 128
C_BLOCK = 128
C_Q = C_HEADS * HEAD_DIM
C_KV = C_KV_HEADS * HEAD_DIM
C_IN = C_Q + 2 * C_KV

D_CH = 2048
D_CONV = 3
D_IN = 3 * D_CH

CD_IN = C_IN + D_IN
CD_OUT = C_Q + D_CH

N_EXPERTS = 64
TOP_K = 8
N_GROUPS = 8
TOPK_GROUPS = 4
D_EXPERT = 256
ROUTE_SCALE = 2.5

kernel_name = "hybrid_hgrn2_mla_swa_conv_moe_dit"

F32 = jnp.float32


def _rmsnorm(x, w):
    xf = x.astype(F32)
    y = xf * lax.rsqrt(jnp.mean(xf * xf, axis=-1, keepdims=True) + RMS_EPS)
    return (y * w.astype(F32)).astype(x.dtype)


def _modulate(h, shift, scale):
    return h * (1.0 + scale) + shift


def _axial_rope(length, rot_dim):
    rows = length // GRID_W
    row = jnp.repeat(jnp.arange(rows), GRID_W).astype(F32)
    col = jnp.tile(jnp.arange(GRID_W), rows).astype(F32)
    n_freq = rot_dim // 4
    inv = ROPE_THETA ** (-jnp.arange(n_freq, dtype=F32) / n_freq)
    ang = jnp.concatenate([row[:, None] * inv, col[:, None] * inv], axis=-1)
    return jnp.cos(ang)[:, None, :], jnp.sin(ang)[:, None, :]


def _apply_rope(x, cos, sin):
    xf = x.astype(F32).reshape(*x.shape[:-1], -1, 2)
    x0, x1 = xf[..., 0], xf[..., 1]
    out = jnp.stack([x0 * cos - x1 * sin, x0 * sin + x1 * cos], axis=-1)
    return out.reshape(x.shape).astype(x.dtype)


def _to_bhld(t, n_heads):
    b, l, _ = t.shape
    return t.reshape(b, l, n_heads, -1).transpose(0, 2, 1, 3)


def _gla_chunks(q, k, v, logf, s0):
    b, h, length, dk = q.shape
    dv = v.shape[-1]
    n = length // A_CHUNK
    q, k, logf = (t.reshape(b, h, n, A_CHUNK, dk) for t in (q, k, logf))
    v = v.reshape(b, h, n, A_CHUNK, dv)
    cum = jnp.cumsum(logf, axis=3)
    ref = cum[:, :, :, A_CHUNK // 2][:, :, :, None]
    last = cum[:, :, :, -1]
    att = jnp.einsum('bhncd,bhnsd->bhncs', q * jnp.exp(cum - ref), k * jnp.exp(ref - cum))
    lower = jnp.tril(jnp.ones((A_CHUNK, A_CHUNK), dtype=bool))
    o_intra = jnp.einsum('bhncs,bhnsv->bhncv', jnp.where(lower, att, 0.0), v)
    q_dec = q * jnp.exp(cum)
    k_dec = k * jnp.exp(last[:, :, :, None] - cum)

    def step(state, xs):
        qn, kn, vn, an = xs
        o = jnp.einsum('bhcd,bhdv->bhcv', qn, state)
        state = an[..., None] * state + jnp.einsum('bhcd,bhcv->bhdv', kn, vn)
        return state, o

    xs = tuple(jnp.moveaxis(t, 2, 0) for t in (q_dec, k_dec, v, jnp.exp(last)))
    s_fin, o_inter = lax.scan(step, s0, xs)
    o = o_intra + jnp.moveaxis(o_inter, 0, 2)
    return o.reshape(b, h, length, dv), s_fin


def _hgrn2(ua_c, ua_x, lb, gnorm_w, need_ctx):
    out_dtype = ua_x.dtype

    def streams(u):
        q, f_fwd, f_bwd, v, g = jnp.split(u, 5, axis=-1)
        q = _to_bhld(jax.nn.silu(q.astype(F32)) * A_DK ** -0.5, A_HEADS)
        v = _to_bhld(v.astype(F32), A_HEADS)
        dirs = []
        for f_pre, lb_d in ((f_fwd, lb[0]), (f_bwd, lb[1])):
            f = lb_d + (1.0 - lb_d) * jax.nn.sigmoid(f_pre.astype(F32))
            dirs.append((_to_bhld(1.0 - f, A_HEADS), _to_bhld(jnp.log(f), A_HEADS)))
        return q, v, g, dirs

    def readout(o, g):
        o = o * lax.rsqrt(jnp.mean(o * o, axis=-1, keepdims=True) + RMS_EPS) * gnorm_w.astype(F32)
        o = o * jax.nn.silu(_to_bhld(g.astype(F32), A_HEADS))
        b, h, l, d = o.shape
        return o.transpose(0, 2, 1, 3).reshape(b, l, h * d).astype(out_dtype)

    qc, vc, gc, dc = streams(ua_c)
    qx, vx, gx, dx = streams(ua_x)
    s0 = jnp.zeros((qx.shape[0], A_HEADS, A_DK, A_DV), F32)
    flip = lambda t: jnp.flip(t, axis=2)
    (kc, lfc), (kx, lfx) = dc[0], dx[0]
    oc_f, sc_f = _gla_chunks(qc, kc, vc, lfc, s0)
    ox_f, _ = _gla_chunks(qx, kx, vx, lfx, sc_f)
    (kc, lfc), (kx, lfx) = dc[1], dx[1]
    oc_b, sc_b = _gla_chunks(flip(qc), flip(kc), flip(vc), flip(lfc), s0)
    ox_b, _ = _gla_chunks(flip(qx), flip(kx), flip(vx), flip(lfx), sc_b)
    y_x = readout(ox_f + flip(ox_b), gx)
    y_c = readout(oc_f + flip(oc_b), gc) if need_ctx else None
    return y_c, y_x


def _blocked_attention(q, k, v, scale):
    b, lq, h, _ = q.shape
    dv = v.shape[-1]

    def one(j):
        qj = lax.dynamic_slice_in_dim(q, j * ATTN_BLOCK, ATTN_BLOCK, axis=1)
        s = jnp.einsum('bqhd,bkhd->bhqk', qj, k, preferred_element_type=F32) * scale
        p = jax.nn.softmax(s, axis=-1).astype(v.dtype)
        return jnp.einsum('bhqk,bkhd->bqhd', p, v)

    o = lax.map(one, jnp.arange(lq // ATTN_BLOCK))
    return jnp.moveaxis(o, 0, 1).reshape(b, lq, h * dv)


def _mla(ub_c, ub_x, q_norm_w, w_uq, kv_norm_w, w_ukv, rope, need_ctx):
    def qkv(u, rope_tab):
        b, l, _ = u.shape
        cq, ckv, kr = jnp.split(u, [B_Q_RANK, B_Q_RANK + B_KV_RANK], axis=-1)
        q = (_rmsnorm(cq, q_norm_w) @ w_uq).reshape(b, l, B_HEADS, B_NOPE + B_ROPE)
        kv = (_rmsnorm(ckv, kv_norm_w) @ w_ukv).reshape(b, l, B_HEADS, B_NOPE + B_VDIM)
        q_nope, q_rope = jnp.split(q, [B_NOPE], axis=-1)
        k_nope, v = jnp.split(kv, [B_NOPE], axis=-1)
        k_rope = kr[:, :, None, :]
        if rope_tab is not None:
            q_rope = _apply_rope(q_rope, *rope_tab)
            k_rope = _apply_rope(k_rope, *rope_tab)
        q = jnp.concatenate([q_nope, q_rope], axis=-1)
        k = jnp.concatenate([k_nope, jnp.broadcast_to(k_rope, (b, l, B_HEADS, B_ROPE))], axis=-1)
        return q, k, v

    qc, kc, vc = qkv(ub_c, None)
    qx, kx, vx = qkv(ub_x, rope)
    scale = (B_NOPE + B_ROPE) ** -0.5
    y_x = _blocked_attention(qx, jnp.concatenate([kc, kx], axis=1), jnp.concatenate([vc, vx], axis=1), scale)
    y_c = _blocked_attention(qc, kc, vc, scale) if need_ctx else None
    return y_c, y_x


def _window_gqa(q, k, v, k_ctx, v_ctx, sink):
    b, length, _, _ = q.shape
    lc = k_ctx.shape[1]
    span = C_BLOCK + 2 * C_WINDOW
    qg = q.reshape(b, length, C_KV_HEADS, C_GROUP, HEAD_DIM)
    pad = ((0, 0), (C_WINDOW, C_WINDOW), (0, 0), (0, 0))
    kp, vp = jnp.pad(k, pad), jnp.pad(v, pad)
    sink_logit = jnp.broadcast_to(sink.astype(F32).reshape(1, C_KV_HEADS, C_GROUP, 1, 1),
                                  (b, C_KV_HEADS, C_GROUP, C_BLOCK, 1))
    scale = HEAD_DIM ** -0.5

    def block(j):
        start = j * C_BLOCK
        qj = lax.dynamic_slice_in_dim(qg, start, C_BLOCK, axis=1)
        kj = lax.dynamic_slice_in_dim(kp, start, span, axis=1)
        vj = lax.dynamic_slice_in_dim(vp, start, span, axis=1)
        s_win = jnp.einsum('bqgrd,bkgd->bgrqk', qj, kj, preferred_element_type=F32) * scale
        qpos = start + jnp.arange(C_BLOCK)
        kpos = start - C_WINDOW + jnp.arange(span)
        valid = ((jnp.abs(qpos[:, None] - kpos[None, :]) <= C_WINDOW)
                 & (kpos >= 0)[None, :] & (kpos < length)[None, :])
        s_win = jnp.where(valid, s_win, -jnp.inf)
        s_ctx = jnp.einsum('bqgrd,bcgd->bgrqc', qj, k_ctx, preferred_element_type=F32) * scale
        p = jax.nn.softmax(jnp.concatenate([sink_logit, s_ctx, s_win], axis=-1), axis=-1).astype(v.dtype)
        return (jnp.einsum('bgrqc,bcgd->bqgrd', p[..., 1:1 + lc], v_ctx)
                + jnp.einsum('bgrqk,bkgd->bqgrd', p[..., 1 + lc:], vj))

    o = lax.map(block, jnp.arange(length // C_BLOCK))
    return jnp.moveaxis(o, 0, 1).reshape(b, length, C_Q)


def _ctx_sink_attention(q, k, v, sink):
    b, lc, _, _ = q.shape
    qg = q.reshape(b, lc, C_KV_HEADS, C_GROUP, HEAD_DIM)
    s = jnp.einsum('bqgrd,bkgd->bgrqk', qg, k, preferred_element_type=F32) * HEAD_DIM ** -0.5
    sl = jnp.broadcast_to(sink.astype(F32).reshape(1, C_KV_HEADS, C_GROUP, 1, 1), (b, C_KV_HEADS, C_GROUP, lc, 1))
    p = jax.nn.softmax(jnp.concatenate([sl, s], axis=-1), axis=-1)[..., 1:].astype(v.dtype)
    return jnp.einsum('bgrqk,bkgd->bqgrd', p, v).reshape(b, lc, C_Q)


def _short_conv(u, conv_w):
    bg, cg, hh = jnp.split(u, 3, axis=-1)
    z = cg * hh
    half = D_CONV // 2
    length = z.shape[1]
    zp = jnp.pad(z, ((0, 0), (half, half), (0, 0)))
    y = sum(conv_w[i] * zp[:, i:i + length] for i in range(D_CONV))
    return bg * y


def _moe(h, w_router, b_router, w_gate, w_up, w_down, s_gate, s_up, s_down):
    shp = h.shape
    t = h.reshape(-1, shp[-1])
    scores = jax.nn.sigmoid(jnp.dot(t, w_router, preferred_element_type=F32))
    biased = scores + b_router.astype(F32)
    per_group = biased.reshape(-1, N_GROUPS, N_EXPERTS // N_GROUPS)
    group_score = lax.top_k(per_group, 2)[0].sum(-1)
    _, top_groups = lax.top_k(group_score, TOPK_GROUPS)
    group_keep = jax.nn.one_hot(top_groups, N_GROUPS, dtype=F32).sum(1) > 0
    expert_keep = jnp.repeat(group_keep, N_EXPERTS // N_GROUPS, axis=-1)
    _, top_idx = lax.top_k(jnp.where(expert_keep, biased, -jnp.inf), TOP_K)
    w = jnp.take_along_axis(scores, top_idx, axis=-1)
    w = w / jnp.sum(w, axis=-1, keepdims=True) * ROUTE_SCALE
    gates = jnp.einsum('nk,nke->ne', w, jax.nn.one_hot(top_idx, N_EXPERTS, dtype=F32))
    shared = (jax.nn.silu(t @ s_gate) * (t @ s_up)) @ s_down

    def step(acc, xs):
        wg, wu, wd, g = xs
        hid = jax.nn.silu(t @ wg) * (t @ wu)
        return acc + (g[:, None].astype(hid.dtype) * hid) @ wd, None

    out, _ = lax.scan(step, shared, (w_gate, w_up, w_down, gates.T))
    return out.reshape(shp)


def setup_inputs(seed: int = 0) -> dict:
    key = jax.random.key(seed)
    ks = iter(jax.random.split(key, 40))
    D = D_MODEL
    NE = (DEPTH + 1) // 2
    NO = DEPTH // 2

    def nrm(shape, scale):
        return jax.random.normal(next(ks), shape, F32) * scale

    def gain(shape):
        return 1.0 + nrm(shape, 0.02)

    return {
        "x": nrm((BATCH, SEQ, D), 1.0),
        "c": nrm((BATCH, D), 1.0),
        "ctx": nrm((BATCH, CTX_LEN, D), 1.0),
        "c_ctx": nrm((D,), 1.0),
        "ada_w": nrm((DEPTH, D, N_MOD * D), 0.5 * D ** -0.5),
        "ada_b": nrm((DEPTH, N_MOD * D), 0.02),
        "norm_mix_w": gain((DEPTH, D)),
        "norm_ffn_w": gain((DEPTH, D)),
        "ab_w_in": nrm((NE, D, AB_IN), D ** -0.5),
        "ab_w_out": nrm((NE, AB_OUT, D), AB_OUT ** -0.5),
        "hgrn_lb_logits": nrm((DEPTH + 1, 2, A_KEY), 0.5),
        "hgrn_gnorm_w": gain((NE, A_DV)),
        "mla_q_norm_w": gain((NE, B_Q_RANK)),
        "mla_w_uq": nrm((NE, B_Q_RANK, B_HEADS * (B_NOPE + B_ROPE)), B_Q_RANK ** -0.5),
        "mla_kv_norm_w": gain((NE, B_KV_RANK)),
        "mla_w_ukv": nrm((NE, B_KV_RANK, B_HEADS * (B_NOPE + B_VDIM)), B_KV_RANK ** -0.5),
        "cd_w_in": nrm((NO, D, CD_IN), D ** -0.5),
        "cd_w_out": nrm((NO, CD_OUT, D), CD_OUT ** -0.5),
        "gqa_sink": nrm((NO, C_HEADS), 1.0),
        "conv_w": nrm((NO, D_CONV, D_CH), D_CONV ** -0.5),
        "router_w": nrm((DEPTH, D, N_EXPERTS), D ** -0.5),
        "router_bias": nrm((DEPTH, N_EXPERTS), 0.01),
        "exp_w_gate": nrm((DEPTH, N_EXPERTS, D, D_EXPERT), D ** -0.5),
        "exp_w_up": nrm((DEPTH, N_EXPERTS, D, D_EXPERT), D ** -0.5),
        "exp_w_down": nrm((DEPTH, N_EXPERTS, D_EXPERT, D), D_EXPERT ** -0.5),
        "sh_w_gate": nrm((DEPTH, D, D_EXPERT), D ** -0.5),
        "sh_w_up": nrm((DEPTH, D, D_EXPERT), D ** -0.5),
        "sh_w_down": nrm((DEPTH, D_EXPERT, D), D_EXPERT ** -0.5),
        "final_norm_w": gain((D,)),
    }


def reference(x, c, ctx, c_ctx, ada_w, ada_b, norm_mix_w, norm_ffn_w, ab_w_in, ab_w_out,
              hgrn_lb_logits, hgrn_gnorm_w, mla_q_norm_w, mla_w_uq, mla_kv_norm_w, mla_w_ukv,
              cd_w_in, cd_w_out, gqa_sink, conv_w, router_w, router_bias, exp_w_gate, exp_w_up,
              exp_w_down, sh_w_gate, sh_w_up, sh_w_down, final_norm_w):
    length = x.shape[1]
    lc = ctx.shape[1]
    rope_b = _axial_rope(length, B_ROPE)
    rope_c = _axial_rope(length, HEAD_DIM)
    lb_all = jnp.cumsum(jax.nn.softmax(hgrn_lb_logits.astype(F32), axis=0), axis=0)
    silu_c = jax.nn.silu(c)
    silu_cc = jax.nn.silu(c_ctx)
    for i in range(DEPTH):
        last = i == DEPTH - 1
        need_ctx = not last
        j = i // 2
        mod_x = jnp.split((silu_c @ ada_w[i] + ada_b[i])[:, None, :], N_MOD, axis=-1)
        mod_c = jnp.split((silu_cc @ ada_w[i] + ada_b[i])[None, None, :], N_MOD, axis=-1)
        hx = _modulate(_rmsnorm(x, norm_mix_w[i]), mod_x[0], mod_x[1])
        hc = _modulate(_rmsnorm(ctx, norm_mix_w[i]), mod_c[0], mod_c[1])
        if i % 2 == 0:
            u = jnp.concatenate([hc, hx], axis=1) @ ab_w_in[j]
            uc, ux = u[:, :lc], u[:, lc:]
            a_c, a_x = _hgrn2(uc[..., :A_IN], ux[..., :A_IN], lb_all[i], hgrn_gnorm_w[j], need_ctx)
            b_c, b_x = _mla(uc[..., A_IN:], ux[..., A_IN:], mla_q_norm_w[j], mla_w_uq[j],
                            mla_kv_norm_w[j], mla_w_ukv[j], rope_b, need_ctx)
            y_x = jnp.concatenate([a_x, b_x], axis=-1) @ ab_w_out[j]
            y_c = jnp.concatenate([a_c, b_c], axis=-1) @ ab_w_out[j] if need_ctx else None
        else:
            u = jnp.concatenate([hc, hx], axis=1) @ cd_w_in[j]
            uc, ux = u[:, :lc], u[:, lc:]
            b_ = x.shape[0]
            qx, kx, vx = jnp.split(ux[..., :C_IN], [C_Q, C_Q + C_KV], axis=-1)
            qc, kc, vc = jnp.split(uc[..., :C_IN], [C_Q, C_Q + C_KV], axis=-1)
            qx = _apply_rope(qx.reshape(b_, length, C_HEADS, HEAD_DIM), *rope_c)
            kx = _apply_rope(kx.reshape(b_, length, C_KV_HEADS, HEAD_DIM), *rope_c)
            vx = vx.reshape(b_, length, C_KV_HEADS, HEAD_DIM)
            qc = qc.reshape(b_, lc, C_HEADS, HEAD_DIM)
            kc = kc.reshape(b_, lc, C_KV_HEADS, HEAD_DIM)
            vc = vc.reshape(b_, lc, C_KV_HEADS, HEAD_DIM)
            att_x = _window_gqa(qx, kx, vx, kc, vc, gqa_sink[j])
            conv_x = _short_conv(ux[..., C_IN:], conv_w[j])
            y_x = jnp.concatenate([att_x, conv_x], axis=-1) @ cd_w_out[j]
            if need_ctx:
                att_c = _ctx_sink_attention(qc, kc, vc, gqa_sink[j])
                conv_c = _short_conv(uc[..., C_IN:], conv_w[j])
                y_c = jnp.concatenate([att_c, conv_c], axis=-1) @ cd_w_out[j]
            else:
                y_c = None
        x = x + mod_x[2] * y_x
        if need_ctx:
            ctx = ctx + mod_c[2] * y_c
        moe_args = (router_w[i], router_bias[i], exp_w_gate[i], exp_w_up[i], exp_w_down[i],
                    sh_w_gate[i], sh_w_up[i], sh_w_down[i])
        hx = _modulate(_rmsnorm(x, norm_ffn_w[i]), mod_x[3], mod_x[4])
        if need_ctx:
            hc = _modulate(_rmsnorm(ctx, norm_ffn_w[i]), mod_c[3], mod_c[4])
            y = _moe(jnp.concatenate([hc, hx], axis=1), *moe_args)
            ctx = ctx + mod_c[5] * y[:, :lc]
            x = x + mod_x[5] * y[:, lc:]
        else:
            x = x + mod_x[5] * _moe(hx, *moe_args)
    return _rmsnorm(x, final_norm_w)
```

```python
import functools

import jax
import jax.numpy as jnp
import numpy as np
from jax import lax
from jax.experimental import pallas as pl
from jax.experimental.pallas import tpu as pltpu

F32 = jnp.float32
BF16 = jnp.bfloat16

D = 4096
B = 2
L = 8192
LC = 256
GRID_W = 64
RMS_EPS = 1e-6
ROPE_THETA = 10000.0
N_MOD = 6

A_HEADS = 16
A_DK = 128
A_KEY = A_HEADS * A_DK
A_CHUNK = 32
A_IN = 5 * A_KEY

B_HEADS = 16
B_Q_RANK = 1024
B_KV_RANK = 512
B_NOPE = 128
B_ROPE = 64
B_VDIM = 128
B_QPAD = 256

C_HEADS = 16
C_KV_HEADS = 4
C_GROUP = C_HEADS // C_KV_HEADS
C_WINDOW = 128
HEAD_DIM = 128
C_Q = C_HEADS * HEAD_DIM
C_KV = C_KV_HEADS * HEAD_DIM
C_IN = C_Q + 2 * C_KV
D_CH = 2048
D_CONV = 3

N_EXPERTS = 64
TOP_K = 8
N_GROUPS = 8
TOPK_GROUPS = 4
GROUP_SIZE = N_EXPERTS // N_GROUPS
D_EXPERT = 256
ROUTE_SCALE = 2.5

T = 256
TM = 512
LANE = 128
VMEM_LIMIT_BYTES = 48 * 1024 * 1024
NEG_BIG = -1e30


def _cp(*sem):
    return pltpu.CompilerParams(dimension_semantics=sem, vmem_limit_bytes=VMEM_LIMIT_BYTES)


def _sigmoid(x):
    return 1.0 / (1.0 + jnp.exp(-x))


def _silu(x):
    return x * _sigmoid(x)


def _dot(a, b):
    return jnp.dot(a, b, preferred_element_type=F32)


def _dot_nt(a, b):
    return lax.dot_general(a, b, (((1,), (1,)), ((), ())), preferred_element_type=F32)


def _mod_row(i, tile):
    return jnp.where(i < (B * L) // tile, i // (L // tile), B)


def _ada_kernel(a_ref, w_ref, b_ref, o_ref):
    a = _silu(a_ref[...]).astype(BF16)
    o_ref[...] = _dot(a, w_ref[...].astype(BF16)) + b_ref[...]


def _ada(a, w, bias):
    n = w.shape[1]
    tn = 512
    return pl.pallas_call(
        _ada_kernel,
        grid=(n // tn,),
        in_specs=[pl.BlockSpec((8, D), lambda j: (0, 0)),
                  pl.BlockSpec((D, tn), lambda j: (0, j)),
                  pl.BlockSpec((1, tn), lambda j: (0, j))],
        out_specs=pl.BlockSpec((8, tn), lambda j: (0, j)),
        out_shape=jax.ShapeDtypeStruct((8, n), F32),
        compiler_params=_cp("parallel"),
    )(a, w, bias)


def _norm_mod_kernel(s_ref, w_ref, sc_ref, sh_ref, o_ref):
    x = s_ref[...]
    y = x * lax.rsqrt(jnp.mean(x * x, axis=-1, keepdims=True) + RMS_EPS) * w_ref[...]
    o_ref[...] = (y * (1.0 + sc_ref[0]) + sh_ref[0]).astype(o_ref.dtype)


def _norm_mod(s, w, scale, shift, nrows):
    mod_spec = pl.BlockSpec((1, 1, D), lambda i: (_mod_row(i, T), 0, 0))
    return pl.pallas_call(
        _norm_mod_kernel,
        grid=(nrows // T,),
        in_specs=[pl.BlockSpec((T, D), lambda i: (i, 0)),
                  pl.BlockSpec((1, D), lambda i: (0, 0)),
                  mod_spec, mod_spec],
        out_specs=pl.BlockSpec((T, D), lambda i: (i, 0)),
        out_shape=jax.ShapeDtypeStruct((nrows, D), BF16),
        compiler_params=_cp("parallel"),
    )(s, w.reshape(1, D), scale, shift)


def _rms_kernel(s_ref, w_ref, o_ref):
    x = s_ref[...]
    o_ref[...] = x * lax.rsqrt(jnp.mean(x * x, axis=-1, keepdims=True) + RMS_EPS) * w_ref[...]


def _rms(s, w):
    nrows = s.shape[0]
    return pl.pallas_call(
        _rms_kernel,
        grid=(nrows // T,),
        in_specs=[pl.BlockSpec((T, D), lambda i: (i, 0)),
                  pl.BlockSpec((1, D), lambda i: (0, 0))],
        out_specs=pl.BlockSpec((T, D), lambda i: (i, 0)),
        out_shape=jax.ShapeDtypeStruct((nrows, D), F32),
        compiler_params=_cp("parallel"),
    )(s, w.reshape(1, D))


def _mm_kernel(a_ref, w_ref, o_ref):
    o_ref[...] = _dot(a_ref[...], w_ref[...]).astype(o_ref.dtype)


def _matmul(a, w, tn):
    m, k = a.shape
    n = w.shape[1]
    return pl.pallas_call(
        _mm_kernel,
        grid=(m // TM, n // tn),
        in_specs=[pl.BlockSpec((TM, k), lambda i, j: (i, 0)),
                  pl.BlockSpec((k, tn), lambda i, j: (0, j))],
        out_specs=pl.BlockSpec((TM, tn), lambda i, j: (i, j)),
        out_shape=jax.ShapeDtypeStruct((m, n), BF16),
        compiler_params=_cp("parallel", "arbitrary"),
    )(a, w)


def _norm_mm_kernel(a_ref, nw_ref, w_ref, o_ref):
    x = a_ref[...].astype(F32)
    y = x * lax.rsqrt(jnp.mean(x * x, axis=-1, keepdims=True) + RMS_EPS) * nw_ref[...]
    o_ref[...] = _dot(y.astype(BF16), w_ref[...]).astype(o_ref.dtype)


def _norm_mm(u, acol, kdim, nw, w, tn):
    m = u.shape[0]
    n = w.shape[1]
    return pl.pallas_call(
        _norm_mm_kernel,
        grid=(m // TM, n // tn),
        in_specs=[pl.BlockSpec((TM, kdim), lambda i, j: (i, acol)),
                  pl.BlockSpec((1, kdim), lambda i, j: (0, 0)),
                  pl.BlockSpec((kdim, tn), lambda i, j: (0, j))],
        out_specs=pl.BlockSpec((TM, tn), lambda i, j: (i, j)),
        out_shape=jax.ShapeDtypeStruct((m, n), BF16),
        compiler_params=_cp("parallel", "arbitrary"),
    )(u, nw.reshape(1, kdim), w)


def _mm2_res_kernel(a1_ref, a2_ref, w1_ref, w2_ref, r_ref, g_ref, o_ref):
    acc = _dot(a1_ref[...], w1_ref[...]) + _dot(a2_ref[...], w2_ref[...])
    o_ref[...] = r_ref[...] + g_ref[0] * acc


def _mm2_res(a1, a2, w, resid, gate, nrows):
    k1 = a1.shape[1]
    k2 = a2.shape[1]
    assert k1 == k2 and w.shape == (k1 + k2, D)
    tn = 512
    return pl.pallas_call(
        _mm2_res_kernel,
        grid=(nrows // TM, D // tn),
        in_specs=[pl.BlockSpec((TM, k1), lambda i, j: (i, 0)),
                  pl.BlockSpec((TM, k2), lambda i, j: (i, 0)),
                  pl.BlockSpec((k1, tn), lambda i, j: (0, j)),
                  pl.BlockSpec((k2, tn), lambda i, j: (1, j)),
                  pl.BlockSpec((TM, tn), lambda i, j: (i, j)),
                  pl.BlockSpec((1, 1, tn), lambda i, j: (_mod_row(i, TM), 0, j))],
        out_specs=pl.BlockSpec((TM, tn), lambda i, j: (i, j)),
        out_shape=jax.ShapeDtypeStruct((nrows, D), F32),
        compiler_params=_cp("parallel", "arbitrary"),
    )(a1, a2, w, w, resid, gate)


def _split3(x):
    hi = x.astype(BF16)
    r = x - hi.astype(F32)
    mid = r.astype(BF16)
    lo = (r - mid.astype(F32)).astype(BF16)
    return hi, mid, lo


def _hgrn_dir(q_raw, f_raw, v, lb, st_ref, o_ref, rev):
    c = A_CHUNK
    q = _silu(q_raw.astype(F32)) * (A_DK ** -0.5)
    f = lb + (1.0 - lb) * _sigmoid(f_raw.astype(F32))
    k = 1.0 - f
    logf = jnp.log(f)
    row = lax.broadcasted_iota(jnp.int32, (T, T), 0)
    col = lax.broadcasted_iota(jnp.int32, (T, T), 1)
    same = (row ^ col) < c
    tri = jnp.where(same & ((col >= row) if rev else (col <= row)), 1.0, 0.0).astype(BF16)
    hi, mid, lo = _split3(logf)
    cum = _dot(tri, hi) + _dot(tri, mid) + _dot(tri, lo)
    crow = lax.broadcasted_iota(jnp.int32, (c, c), 0)
    ccol = lax.broadcasted_iota(jnp.int32, (c, c), 1)
    tri_c = (ccol >= crow) if rev else (ccol <= crow)
    st = st_ref[...]
    nchunk = T // c
    for n in (range(nchunk - 1, -1, -1) if rev else range(nchunk)):
        sl = slice(n * c, (n + 1) * c)
        cum_c = cum[sl]
        ref = cum_c[c // 2 - 1:c // 2] if rev else cum_c[c // 2:c // 2 + 1]
        last = cum_c[0:1] if rev else cum_c[c - 1:c]
        qc, kc, vc = q[sl], k[sl], v[sl]
        qs = (qc * jnp.exp(cum_c - ref)).astype(BF16)
        ks = (kc * jnp.exp(ref - cum_c)).astype(BF16)
        att = jnp.where(tri_c, _dot_nt(qs, ks), 0.0)
        o = _dot(att.astype(BF16), vc)
        qd = (qc * jnp.exp(cum_c)).astype(BF16)
        kd = (kc * jnp.exp(last - cum_c)).astype(BF16)
        o = o + _dot_nt(qd, st.astype(BF16))
        st = st * jnp.exp(last) + lax.dot_general(vc, kd, (((0,), (0,)), ((), ())),
                                                  preferred_element_type=F32)
        o_ref[sl, :] = o.astype(o_ref.dtype)
    st_ref[...] = st


def _hgrn_kernel(qf, ff, vf, qb, fb, vb, lbf, lbb, of, ob, stf, stb):
    @pl.when(pl.program_id(2) == 0)
    def _():
        stf[...] = jnp.zeros_like(stf)
        stb[...] = jnp.zeros_like(stb)

    _hgrn_dir(qf[...], ff[...], vf[...], lbf[0], stf, of, False)
    _hgrn_dir(qb[...], fb[...], vb[...], lbb[0], stb, ob, True)


def _hgrn(u, lbf, lbb):
    assert LC == T
    xt = L // T
    n_rows = u.shape[0]
    hh = A_HEADS

    def blk_f(b, n):
        return jnp.where(n == 0, B * xt + b, b * xt + n - 1)

    def blk_b(b, n):
        return jnp.where(n == 0, B * xt + b, b * xt + xt - n)

    def spec(blk, cb):
        return pl.BlockSpec((T, LANE), lambda b, h, n: (blk(b, n), cb * hh + h))

    lb_spec = pl.BlockSpec((1, 1, LANE), lambda b, h, n: (h, 0, 0))
    return pl.pallas_call(
        _hgrn_kernel,
        grid=(B, hh, xt + 1),
        in_specs=[spec(blk_f, 0), spec(blk_f, 1), spec(blk_f, 3),
                  spec(blk_b, 0), spec(blk_b, 2), spec(blk_b, 3),
                  lb_spec, lb_spec],
        out_specs=[spec(blk_f, 0), spec(blk_b, 0)],
        out_shape=[jax.ShapeDtypeStruct((n_rows, A_KEY), BF16)] * 2,
        scratch_shapes=[pltpu.VMEM((LANE, LANE), F32)] * 2,
        compiler_params=_cp("parallel", "parallel", "arbitrary"),
    )(u, u, u, u, u, u, lbf, lbb)


def _hgrn_out_kernel(of_ref, ob_ref, g_ref, w_ref, y_ref):
    for h in range(A_HEADS):
        sl = slice(h * LANE, (h + 1) * LANE)
        o = of_ref[:, sl].astype(F32) + ob_ref[:, sl].astype(F32)
        o = o * lax.rsqrt(jnp.mean(o * o, axis=-1, keepdims=True) + RMS_EPS) * w_ref[...]
        y_ref[:, sl] = (o * _silu(g_ref[:, sl].astype(F32))).astype(y_ref.dtype)


def _hgrn_out(o_f, o_b, u, gnorm_w):
    n_rows = o_f.shape[0]
    blk = pl.BlockSpec((T, A_KEY), lambda i: (i, 0))
    return pl.pallas_call(
        _hgrn_out_kernel,
        grid=(n_rows // T,),
        in_specs=[blk, blk,
                  pl.BlockSpec((T, A_KEY), lambda i: (i, 4)),
                  pl.BlockSpec((1, LANE), lambda i: (0, 0))],
        out_specs=blk,
        out_shape=jax.ShapeDtypeStruct((n_rows, A_KEY), BF16),
        compiler_params=_cp("parallel"),
    )(o_f, o_b, u, gnorm_w.reshape(1, LANE))


def _rope_perm(n):
    return np.concatenate([np.arange(0, n, 2), np.arange(1, n, 2)])


def _rope_tables(rot_dim, with_ctx):
    half = rot_dim // 2
    rows = L // GRID_W
    row = jnp.repeat(jnp.arange(rows), GRID_W).astype(F32)
    col = jnp.tile(jnp.arange(GRID_W), rows).astype(F32)
    n_freq = rot_dim // 4
    inv = ROPE_THETA ** (-jnp.arange(n_freq, dtype=F32) / n_freq)
    ang = jnp.concatenate([row[:, None] * inv, col[:, None] * inv], axis=-1)
    cos, sin = jnp.cos(ang), jnp.sin(ang)
    zpad = jnp.zeros((L, LANE - rot_dim), F32)
    zh = jnp.zeros((L, half), F32)
    c = jnp.concatenate([cos, cos, zpad], axis=-1)
    s1 = jnp.concatenate([-sin, zh, zpad], axis=-1)
    s2 = jnp.concatenate([zh, sin, zpad], axis=-1)
    if with_ctx:
        ident = jnp.ones((T, LANE), F32)
        zero = jnp.zeros((T, LANE), F32)
        c = jnp.concatenate([ident, c], axis=0)
        s1 = jnp.concatenate([zero, s1], axis=0)
        s2 = jnp.concatenate([zero, s2], axis=0)
    return c, s1, s2


def _rope_apply(x, c, s1, s2, half):
    return x * c + pltpu.roll(x, LANE - half, 1) * s1 + pltpu.roll(x, half, 1) * s2


def _tab_blk(i):
    xt = L // T
    return jnp.where(i < B * xt, 1 + i % xt, 0)


def _krope_kernel(x_ref, c_ref, s1_ref, s2_ref, o_ref):
    x = x_ref[...].astype(F32)
    o_ref[...] = _rope_apply(x, c_ref[...], s1_ref[...], s2_ref[...], B_ROPE // 2).astype(o_ref.dtype)


def _krope(ub, tabs):
    n_rows = ub.shape[0]
    cb = ub.shape[1] // LANE - 1
    tspec = pl.BlockSpec((T, LANE), lambda i: (_tab_blk(i), 0))
    return pl.pallas_call(
        _krope_kernel,
        grid=(n_rows // T,),
        in_specs=[pl.BlockSpec((T, LANE), lambda i: (i, cb)), tspec, tspec, tspec],
        out_specs=pl.BlockSpec((T, LANE), lambda i: (i, 0)),
        out_shape=jax.ShapeDtypeStruct((n_rows, LANE), BF16),
        compiler_params=_cp("parallel"),
    )(ub, *tabs)


MLA_KV_CHUNK = 1024


def _softmax_step(q, k, v, m, l, acc):
    s = _dot_nt(q, k)
    m_new = jnp.maximum(m, jnp.max(s, axis=-1, keepdims=True))
    alpha = jnp.exp(m - m_new)
    p = jnp.exp(s - m_new)
    l = alpha * l + jnp.sum(p, axis=-1, keepdims=True)
    acc = alpha * acc + _dot(p.astype(BF16), v)
    return m_new, l, acc


def _mla_attn_kernel(q_ref, c_ref, s1_ref, s2_ref, kxn_ref, vx_ref, krx_ref, kcn_ref, vc_ref, krc_ref,
                     o_ref, kfx, kfc):
    qi = pl.program_id(2)
    xt = L // T

    @pl.when(qi == 0)
    def _():
        kfx[:, :B_NOPE] = kxn_ref[...]
        kfx[:, B_NOPE:] = krx_ref[...]
        kfc[:, :B_NOPE] = kcn_ref[...]
        kfc[:, B_NOPE:] = krc_ref[...]

    qr = _rope_apply(q_ref[:, B_NOPE:].astype(F32), c_ref[...], s1_ref[...], s2_ref[...], B_ROPE // 2)
    q = jnp.concatenate([q_ref[:, :B_NOPE], qr.astype(BF16)], axis=1)
    m0 = jnp.full((T, 1), NEG_BIG, F32)
    l0 = jnp.zeros((T, 1), F32)
    a0 = jnp.zeros((T, B_VDIM), F32)

    @pl.when(qi < xt)
    def _():
        m, l, acc = _softmax_step(q, kfc[...], vc_ref[...], m0, l0, a0)
        for n in range(L // MLA_KV_CHUNK):
            sl = slice(n * MLA_KV_CHUNK, (n + 1) * MLA_KV_CHUNK)
            m, l, acc = _softmax_step(q, kfx[sl, :], vx_ref[sl, :], m, l, acc)
        o_ref[...] = (acc / l).astype(o_ref.dtype)

    @pl.when(qi == xt)
    def _():
        m, l, acc = _softmax_step(q, kfc[...], vc_ref[...], m0, l0, a0)
        o_ref[...] = (acc / l).astype(o_ref.dtype)


def _mla_attn(q, kv, kr, tabs):
    xt = L // T
    n_rows = q.shape[0]
    ctx0 = B * L // LC

    def rowblk(b, qi):
        return jnp.where(qi < xt, b * xt + qi, B * xt + b)

    tspec = pl.BlockSpec((T, LANE), lambda b, h, qi: (jnp.where(qi < xt, 1 + qi, 0), 0))
    return pl.pallas_call(
        _mla_attn_kernel,
        grid=(B, B_HEADS, xt + 1),
        in_specs=[pl.BlockSpec((T, B_QPAD), lambda b, h, qi: (rowblk(b, qi), h)),
                  tspec, tspec, tspec,
                  pl.BlockSpec((L, LANE), lambda b, h, qi: (b, 2 * h)),
                  pl.BlockSpec((L, LANE), lambda b, h, qi: (b, 2 * h + 1)),
                  pl.BlockSpec((L, LANE), lambda b, h, qi: (b, 0)),
                  pl.BlockSpec((LC, LANE), lambda b, h, qi: (ctx0 + b, 2 * h)),
                  pl.BlockSpec((LC, LANE), lambda b, h, qi: (ctx0 + b, 2 * h + 1)),
                  pl.BlockSpec((LC, LANE), lambda b, h, qi: (ctx0 + b, 0))],
        out_specs=pl.BlockSpec((T, B_VDIM), lambda b, h, qi: (rowblk(b, qi), h)),
        out_shape=jax.ShapeDtypeStruct((n_rows, B_HEADS * B_VDIM), BF16),
        scratch_shapes=[pltpu.VMEM((L, B_QPAD), BF16), pltpu.VMEM((LC, B_QPAD), BF16)],
        compiler_params=_cp("parallel", "parallel", "arbitrary"),
    )(q, *tabs, kv, kv, kr, kv, kv, kr)


def _rope_qk_kernel(x_ref, c_ref, s1_ref, s2_ref, o_ref):
    c, s1, s2 = c_ref[...], s1_ref[...], s2_ref[...]
    for h in range(x_ref.shape[1] // LANE):
        sl = slice(h * LANE, (h + 1) * LANE)
        o_ref[:, sl] = _rope_apply(x_ref[:, sl].astype(F32), c, s1, s2, HEAD_DIM // 2).astype(o_ref.dtype)


def _rope_qk(ux, tabs):
    width = C_Q + C_KV
    tn = C_KV
    xt = L // T
    tspec = pl.BlockSpec((T, LANE), lambda i, j: (i % xt, 0))
    return pl.pallas_call(
        _rope_qk_kernel,
        grid=(B * xt, width // tn),
        in_specs=[pl.BlockSpec((T, tn), lambda i, j: (i, j)), tspec, tspec, tspec],
        out_specs=pl.BlockSpec((T, tn), lambda i, j: (i, j)),
        out_shape=jax.ShapeDtypeStruct((B * L, width), BF16),
        compiler_params=_cp("parallel", "parallel"),
    )(ux, *tabs)


def _win_kernel(sink_ref, q_ref, kx_ref, vx_ref, kc_ref, vc_ref, o_ref):
    g = pl.program_id(1)
    t = pl.program_id(2)
    span = 2 * T
    start = pl.multiple_of(jnp.clip(t * T - C_WINDOW, 0, L - span), C_WINDOW)
    kw = kx_ref[pl.ds(start, span), :]
    vw = vx_ref[pl.ds(start, span), :]
    qpos = t * T + lax.broadcasted_iota(jnp.int32, (T, span), 0)
    kpos = start + lax.broadcasted_iota(jnp.int32, (T, span), 1)
    valid = jnp.abs(qpos - kpos) <= C_WINDOW
    kc = kc_ref[...]
    vc = vc_ref[...]
    for r in range(C_GROUP):
        sl = slice(r * HEAD_DIM, (r + 1) * HEAD_DIM)
        q = q_ref[:, sl]
        sw = jnp.where(valid, _dot_nt(q, kw), NEG_BIG)
        sc = _dot_nt(q, kc)
        sk = sink_ref[g * C_GROUP + r]
        m = jnp.maximum(jnp.maximum(jnp.max(sw, axis=-1, keepdims=True),
                                    jnp.max(sc, axis=-1, keepdims=True)), sk)
        pw = jnp.exp(sw - m)
        pc = jnp.exp(sc - m)
        l = jnp.sum(pw, axis=-1, keepdims=True) + jnp.sum(pc, axis=-1, keepdims=True) + jnp.exp(sk - m)
        o = _dot(pw.astype(BF16), vw) + _dot(pc.astype(BF16), vc)
        o_ref[:, sl] = (o / l).astype(o_ref.dtype)


def _win_attn(qk, ux, sink):
    assert T == 2 * C_WINDOW
    xt = L // T
    ctx0 = B * L // LC
    kcol = C_Q // LANE
    vcol = (C_Q + C_KV) // LANE
    gw = C_GROUP * HEAD_DIM
    return pl.pallas_call(
        _win_kernel,
        grid=(B, C_KV_HEADS, xt),
        in_specs=[pl.BlockSpec(memory_space=pltpu.SMEM),
                  pl.BlockSpec((T, gw), lambda b, g, t: (b * xt + t, g)),
                  pl.BlockSpec((L, LANE), lambda b, g, t: (b, kcol + g)),
                  pl.BlockSpec((L, LANE), lambda b, g, t: (b, vcol + g)),
                  pl.BlockSpec((LC, LANE), lambda b, g, t: (ctx0 + b, kcol + g)),
                  pl.BlockSpec((LC, LANE), lambda b, g, t: (ctx0 + b, vcol + g))],
        out_specs=pl.BlockSpec((T, gw), lambda b, g, t: (b * xt + t, g)),
        out_shape=jax.ShapeDtypeStruct((B * L, C_Q), BF16),
        compiler_params=_cp("parallel", "parallel", "arbitrary"),
    )(sink, qk, qk, ux, ux, ux)


CONV_TN = 512
SUBLANE = 8


def _conv_kernel(bg_ref, cg_ref, hh_ref, cgp_ref, hhp_ref, cgn_ref, hhn_ref, w_ref, o_ref):
    xt = L // T
    t = pl.program_id(0) % xt
    z = cg_ref[...].astype(F32) * hh_ref[...].astype(F32)
    zp = cgp_ref[SUBLANE - 1:SUBLANE, :].astype(F32) * hhp_ref[SUBLANE - 1:SUBLANE, :].astype(F32)
    zn = cgn_ref[0:1, :].astype(F32) * hhn_ref[0:1, :].astype(F32)
    zp = jnp.where(t > 0, zp, 0.0)
    zn = jnp.where(t < xt - 1, zn, 0.0)
    rows = lax.broadcasted_iota(jnp.int32, (T, CONV_TN), 0)
    z_prev = jnp.where(rows == 0, zp, pltpu.roll(z, 1, 0))
    z_next = jnp.where(rows == T - 1, zn, pltpu.roll(z, T - 1, 0))
    y = w_ref[0:1, :] * z_prev + w_ref[1:2, :] * z + w_ref[2:3, :] * z_next
    o_ref[...] = (bg_ref[...].astype(F32) * y).astype(o_ref.dtype)


def _conv(ux, conv_w):
    xt = L // T
    nx = B * xt
    c0 = C_IN // CONV_TN
    nc = D_CH // CONV_TN
    rb = T // SUBLANE

    def main(cb):
        return pl.BlockSpec((T, CONV_TN), lambda i, j: (i, c0 + cb * nc + j))

    def prev(cb):
        return pl.BlockSpec((SUBLANE, CONV_TN), lambda i, j: (jnp.maximum(i * rb - 1, 0), c0 + cb * nc + j))

    def nxt(cb):
        return pl.BlockSpec((SUBLANE, CONV_TN),
                            lambda i, j: (jnp.minimum((i + 1) * rb, nx * rb - 1), c0 + cb * nc + j))

    return pl.pallas_call(
        _conv_kernel,
        grid=(nx, nc),
        in_specs=[main(0), main(1), main(2), prev(1), prev(2), nxt(1), nxt(2),
                  pl.BlockSpec((D_CONV, CONV_TN), lambda i, j: (0, j))],
        out_specs=pl.BlockSpec((T, CONV_TN), lambda i, j: (i, j)),
        out_shape=jax.ShapeDtypeStruct((B * L, D_CH), BF16),
        compiler_params=_cp("parallel", "parallel"),
    )(ux, ux, ux, ux, ux, ux, ux, conv_w)


def _router_kernel(h_ref, wr_ref, b_ref, g_ref):
    tm = h_ref.shape[0]
    scores = _sigmoid(_dot_nt(wr_ref[...], h_ref[...]))
    biased = scores + b_ref[...]
    ng = N_GROUPS
    sl = [biased[j * ng:(j + 1) * ng] for j in range(GROUP_SIZE)]
    sc = [scores[j * ng:(j + 1) * ng] for j in range(GROUP_SIZE)]
    m1 = sl[0]
    m2 = jnp.full_like(m1, -jnp.inf)
    for v in sl[1:]:
        m2 = jnp.maximum(m2, jnp.minimum(m1, v))
        m1 = jnp.maximum(m1, v)
    gs = m1 + m2
    gidx = lax.broadcasted_iota(jnp.int32, (ng, tm), 0)
    rank = jnp.zeros((ng, tm), jnp.int32)
    for k in range(1, ng):
        r = pltpu.roll(gs, k, 0)
        rank += jnp.where((r > gs) | ((r == gs) & (gidx >= k)), 1, 0)
    keep = rank < TOPK_GROUPS
    ms = [jnp.where(keep, v, -jnp.inf) for v in sl]
    rolled = [[ms[j] if k == 0 else pltpu.roll(ms[j], k, 0) for k in range(ng)] for j in range(GROUP_SIZE)]
    tot = jnp.zeros((ng, tm), F32)
    picked = []
    for j in range(GROUP_SIZE):
        cnt = jnp.zeros((ng, tm), jnp.int32)
        for jp in range(GROUP_SIZE):
            for k in range(ng):
                r = rolled[jp][k]
                if k == 0:
                    if jp == j:
                        continue
                    beats = (r >= ms[j]) if jp < j else (r > ms[j])
                else:
                    beats = (r > ms[j]) | ((r == ms[j]) & (gidx >= k))
                cnt += jnp.where(beats, 1, 0)
        w = jnp.where(cnt < TOP_K, sc[j], 0.0)
        picked.append(w)
        tot = tot + w
    tot = jnp.sum(tot, axis=0, keepdims=True)
    for j in range(GROUP_SIZE):
        g_ref[j * ng:(j + 1) * ng, :] = picked[j] / tot * ROUTE_SCALE


def _router(h, wr_t, bias, nrows):
    return pl.pallas_call(
        _router_kernel,
        grid=(nrows // TM,),
        in_specs=[pl.BlockSpec((TM, D), lambda i: (i, 0)),
                  pl.BlockSpec((N_EXPERTS, D), lambda i: (0, 0)),
                  pl.BlockSpec((N_EXPERTS, 1), lambda i: (0, 0))],
        out_specs=pl.BlockSpec((N_EXPERTS, TM), lambda i: (0, i)),
        out_shape=jax.ShapeDtypeStruct((N_EXPERTS, nrows), F32),
        compiler_params=_cp("parallel"),
    )(h, wr_t, bias)


def _router_order():
    r = np.arange(N_EXPERTS)
    return (r % N_GROUPS) * GROUP_SIZE + r // N_GROUPS


def _glu_kernel(x_ref, wg_ref, wu_ref, wd_ref, o_ref):
    x = x_ref[...]
    hid = _silu(_dot(x, wg_ref[...])) * _dot(x, wu_ref[...])
    o_ref[...] = _dot(hid.astype(BF16), wd_ref[...]).astype(o_ref.dtype)


def _glu(h, wg, wu, wd, nrows):
    return pl.pallas_call(
        _glu_kernel,
        grid=(nrows // TM,),
        in_specs=[pl.BlockSpec((TM, D), lambda i: (i, 0)),
                  pl.BlockSpec((D, D_EXPERT), lambda i: (0, 0)),
                  pl.BlockSpec((D, D_EXPERT), lambda i: (0, 0)),
                  pl.BlockSpec((D_EXPERT, D), lambda i: (0, 0))],
        out_specs=pl.BlockSpec((TM, D), lambda i: (i, 0)),
        out_shape=jax.ShapeDtypeStruct((nrows, D), BF16),
        compiler_params=_cp("parallel"),
    )(h, wg, wu, wd)


def _moe_dense_kernel(x_ref, gt_ref, wg_ref, wu_ref, wd_ref, o_ref, acc_ref):
    e = pl.program_id(1)

    @pl.when(e == 0)
    def _():
        acc_ref[...] = jnp.zeros_like(acc_ref)

    x = x_ref[...]
    lane = lax.broadcasted_iota(jnp.int32, gt_ref.shape, 1)
    g = jnp.sum(jnp.where(lane == e, gt_ref[...], 0.0), axis=1, keepdims=True)
    hid = _silu(_dot(x, wg_ref[0])) * _dot(x, wu_ref[0])
    acc_ref[...] += _dot((g * hid).astype(BF16), wd_ref[0])

    @pl.when(e == N_EXPERTS - 1)
    def _():
        o_ref[...] = acc_ref[...].astype(o_ref.dtype)


def _moe_dense(h, gates, wg, wu, wd, nrows):
    return pl.pallas_call(
        _moe_dense_kernel,
        grid=(nrows // TM, N_EXPERTS),
        in_specs=[pl.BlockSpec((TM, D), lambda i, e: (i, 0)),
                  pl.BlockSpec((TM, N_EXPERTS), lambda i, e: (i, 0)),
                  pl.BlockSpec((1, D, D_EXPERT), lambda i, e: (e, 0, 0)),
                  pl.BlockSpec((1, D, D_EXPERT), lambda i, e: (e, 0, 0)),
                  pl.BlockSpec((1, D_EXPERT, D), lambda i, e: (e, 0, 0))],
        out_specs=pl.BlockSpec((TM, D), lambda i, e: (i, 0)),
        out_shape=jax.ShapeDtypeStruct((nrows, D), BF16),
        scratch_shapes=[pltpu.VMEM((TM, D), F32)],
        compiler_params=_cp("parallel", "arbitrary"),
    )(h, gates, wg, wu, wd)


def _combine_kernel(r_ref, a_ref, s_ref, g_ref, o_ref):
    o_ref[...] = r_ref[...] + g_ref[0] * (a_ref[...].astype(F32) + s_ref[...].astype(F32))


def _combine(resid, routed, shared, gate, nrows):
    blk = pl.BlockSpec((T, D), lambda i: (i, 0))
    return pl.pallas_call(
        _combine_kernel,
        grid=(nrows // T,),
        in_specs=[blk, blk, blk, pl.BlockSpec((1, 1, D), lambda i: (_mod_row(i, T), 0, 0))],
        out_specs=blk,
        out_shape=jax.ShapeDtypeStruct((nrows, D), F32),
        compiler_params=_cp("parallel"),
    )(resid, routed, shared, gate)


def _moe(s, h, gate, nrows, router_w, router_bias, wg, wu, wd, sg, su, sd):
    order = _router_order()
    wr_t = router_w.T[order].astype(BF16)
    bias = router_bias[order].reshape(N_EXPERTS, 1).astype(F32)
    g_t = _router(h, wr_t, bias, nrows)
    gates = g_t[np.argsort(order)].T
    shared = _glu(h, sg.astype(BF16), su.astype(BF16), sd.astype(BF16), nrows)
    routed = _moe_dense(h, gates, wg.astype(BF16), wu.astype(BF16), wd.astype(BF16), nrows)
    return _combine(s, routed, shared, gate, nrows)


def _mla_in_weights(w_in):
    wb = w_in[:, A_IN:]
    kr = wb[:, B_Q_RANK + B_KV_RANK:][:, _rope_perm(B_ROPE)]
    pad = jnp.zeros((D, LANE - B_ROPE), w_in.dtype)
    return jnp.concatenate([wb[:, :B_Q_RANK + B_KV_RANK], kr, pad], axis=1).astype(BF16)


def _mla_uq_weights(w_uq):
    scale = (B_NOPE + B_ROPE) ** -0.5
    w = (w_uq * scale).reshape(B_Q_RANK, B_HEADS, B_NOPE + B_ROPE)
    rope = w[:, :, B_NOPE:][:, :, _rope_perm(B_ROPE)]
    pad = jnp.zeros((B_Q_RANK, B_HEADS, B_QPAD - B_NOPE - B_ROPE), w.dtype)
    return jnp.concatenate([w[:, :, :B_NOPE], rope, pad], axis=-1).reshape(B_Q_RANK, B_HEADS * B_QPAD).astype(BF16)


def _cd_in_weights(w_in):
    perm = _rope_perm(HEAD_DIM)
    q = (w_in[:, :C_Q] * HEAD_DIM ** -0.5).reshape(D, C_HEADS, HEAD_DIM)[:, :, perm].reshape(D, C_Q)
    k = w_in[:, C_Q:C_Q + C_KV].reshape(D, C_KV_HEADS, HEAD_DIM)[:, :, perm].reshape(D, C_KV)
    return jnp.concatenate([q, k, w_in[:, C_Q + C_KV:]], axis=1).astype(BF16)


def kernel(x, c, ctx, c_ctx, ada_w, ada_b, norm_mix_w, norm_ffn_w, ab_w_in, ab_w_out, hgrn_lb_logits, hgrn_gnorm_w, mla_q_norm_w, mla_w_uq, mla_kv_norm_w, mla_w_ukv, cd_w_in, cd_w_out, gqa_sink, conv_w, router_w, router_bias, exp_w_gate, exp_w_up, exp_w_down, sh_w_gate, sh_w_up, sh_w_down, final_norm_w):
    nx = B * L
    n = nx + B * LC
    s = jnp.concatenate([x.reshape(nx, D), ctx.reshape(B * LC, D)], axis=0)
    lb_all = jnp.cumsum(jax.nn.softmax(hgrn_lb_logits.astype(F32), axis=0), axis=0)
    cond = jnp.concatenate([c, c_ctx[None, :], jnp.zeros((8 - B - 1, D), F32)], axis=0)

    def modulation(i):
        m = _ada(cond, ada_w[i], ada_b[i].reshape(1, N_MOD * D)).reshape(8, N_MOD, D)
        return [m[:B + 1, k, :].reshape(B + 1, 1, D) for k in range(N_MOD)]

    def moe_args(i):
        return (router_w[i], router_bias[i], exp_w_gate[i], exp_w_up[i], exp_w_down[i],
                sh_w_gate[i], sh_w_up[i], sh_w_down[i])

    mod = modulation(0)
    h = _norm_mod(s, norm_mix_w[0], mod[1], mod[0], n)
    ua = _matmul(h, ab_w_in[0][:, :A_IN].astype(BF16), 1024)
    ub = _matmul(h, _mla_in_weights(ab_w_in[0]), B_Q_RANK + B_KV_RANK + LANE)
    lb = lb_all[0].reshape(2, A_HEADS, 1, A_DK)
    o_f, o_b = _hgrn(ua, lb[0], lb[1])
    ya = _hgrn_out(o_f, o_b, ua, hgrn_gnorm_w[0])
    tabs = _rope_tables(B_ROPE, True)
    q = _norm_mm(ub, 0, B_Q_RANK, mla_q_norm_w[0], _mla_uq_weights(mla_w_uq[0]), 1024)
    kv = _norm_mm(ub, B_Q_RANK // B_KV_RANK, B_KV_RANK, mla_kv_norm_w[0], mla_w_ukv[0].astype(BF16), 1024)
    kr = _krope(ub, tabs)
    yb = _mla_attn(q, kv, kr, tabs)
    s = _mm2_res(ya, yb, ab_w_out[0].astype(BF16), s, mod[2], n)
    h = _norm_mod(s, norm_ffn_w[0], mod[4], mod[3], n)
    s = _moe(s, h, mod[5], n, *moe_args(0))

    mod = modulation(1)
    h = _norm_mod(s, norm_mix_w[1], mod[1], mod[0], n)
    ux = _matmul(h, _cd_in_weights(cd_w_in[0]), 1024)
    qk = _rope_qk(ux, _rope_tables(HEAD_DIM, False))
    att = _win_attn(qk, ux, gqa_sink[0].astype(F32))
    cv = _conv(ux, conv_w[0])
    s = _mm2_res(att, cv, cd_w_out[0].astype(BF16), s, mod[2], nx)
    h = _norm_mod(s, norm_ffn_w[1], mod[4], mod[3], nx)
    s = _moe(s, h, mod[5], nx, *moe_args(1))
    return _rms(s, final_norm_w).reshape(B, L, D)
```

```python
import functools

import jax
import jax.numpy as jnp
import numpy as np
from jax import lax
from jax.experimental import pallas as pl
from jax.experimental.pallas import tpu as pltpu

F32 = jnp.float32
BF16 = jnp.bfloat16

D = 4096
B = 2
L = 8192
LC = 256
GRID_W = 64
RMS_EPS = 1e-6
ROPE_THETA = 10000.0
N_MOD = 6

A_HEADS = 16
A_DK = 128
A_KEY = A_HEADS * A_DK
A_CHUNK = 32
A_IN = 5 * A_KEY

B_HEADS = 16
B_Q_RANK = 1024
B_KV_RANK = 512
B_NOPE = 128
B_ROPE = 64
B_VDIM = 128
B_QPAD = 256

C_HEADS = 16
C_KV_HEADS = 4
C_GROUP = C_HEADS // C_KV_HEADS
C_WINDOW = 128
HEAD_DIM = 128
C_Q = C_HEADS * HEAD_DIM
C_KV = C_KV_HEADS * HEAD_DIM
C_IN = C_Q + 2 * C_KV
D_CH = 2048
D_CONV = 3

N_EXPERTS = 64
TOP_K = 8
N_GROUPS = 8
TOPK_GROUPS = 4
GROUP_SIZE = N_EXPERTS // N_GROUPS
D_EXPERT = 256
ROUTE_SCALE = 2.5

T = 256
TM = 512
LANE = 128
VMEM_LIMIT_BYTES = 48 * 1024 * 1024
EXPERT_VMEM_LIMIT_BYTES = 56 * 1024 * 1024
NEG_BIG = -1e30
PACK = 2 * LANE
NP = D // PACK


def _cp(*sem):
    return pltpu.CompilerParams(dimension_semantics=sem, vmem_limit_bytes=VMEM_LIMIT_BYTES)


def _sigmoid(x):
    return 1.0 / (1.0 + jnp.exp(-x))


def _silu(x):
    return x * _sigmoid(x)


def _dot(a, b):
    return jnp.dot(a, b, preferred_element_type=F32)


def _dot_nt(a, b):
    return lax.dot_general(a, b, (((1,), (1,)), ((), ())), preferred_element_type=F32)


def _mod_row(i, tile):
    return jnp.where(i < (B * L) // tile, i // (L // tile), B)


def _ada_kernel(a_ref, w_ref, b_ref, o_ref):
    a = _silu(a_ref[...]).astype(BF16)
    o_ref[...] = _dot(a, w_ref[...].astype(BF16)) + b_ref[...]


def _ada(a, w, bias):
    n = w.shape[1]
    tn = 512
    return pl.pallas_call(
        _ada_kernel,
        grid=(n // tn,),
        in_specs=[pl.BlockSpec((8, D), lambda j: (0, 0)),
                  pl.BlockSpec((D, tn), lambda j: (0, j)),
                  pl.BlockSpec((1, tn), lambda j: (0, j))],
        out_specs=pl.BlockSpec((8, tn), lambda j: (0, j)),
        out_shape=jax.ShapeDtypeStruct((8, n), F32),
        compiler_params=_cp("parallel"),
    )(a, w, bias)


def _norm_mod_kernel(s_ref, w_ref, sc_ref, sh_ref, o_ref):
    x = s_ref[...]
    y = x * lax.rsqrt(jnp.mean(x * x, axis=-1, keepdims=True) + RMS_EPS) * w_ref[...]
    o_ref[...] = (y * (1.0 + sc_ref[0]) + sh_ref[0]).astype(o_ref.dtype)


def _norm_mod(s, w, scale, shift, nrows):
    mod_spec = pl.BlockSpec((1, 1, D), lambda i: (_mod_row(i, T), 0, 0))
    return pl.pallas_call(
        _norm_mod_kernel,
        grid=(nrows // T,),
        in_specs=[pl.BlockSpec((T, D), lambda i: (i, 0)),
                  pl.BlockSpec((1, D), lambda i: (0, 0)),
                  mod_spec, mod_spec],
        out_specs=pl.BlockSpec((T, D), lambda i: (i, 0)),
        out_shape=jax.ShapeDtypeStruct((nrows, D), BF16),
        compiler_params=_cp("parallel"),
    )(s, w.reshape(1, D), scale, shift)


def _pack2(lo, hi):
    lo_b = lax.bitcast_convert_type(lo.astype(BF16).astype(F32), jnp.uint32) >> 16
    hi_b = lax.bitcast_convert_type(hi.astype(BF16).astype(F32), jnp.uint32)
    return hi_b | lo_b


def _unpack2(w):
    lo = lax.bitcast_convert_type(w << 16, F32)
    hi = lax.bitcast_convert_type(w & jnp.uint32(0xFFFF0000), F32)
    return lo, hi


def _norm_mod_pack_kernel(s_ref, w_ref, sc_ref, sh_ref, o_ref, p_ref):
    x = s_ref[...]
    y = x * lax.rsqrt(jnp.mean(x * x, axis=-1, keepdims=True) + RMS_EPS) * w_ref[...]
    y = y * (1.0 + sc_ref[0]) + sh_ref[0]
    o_ref[...] = y.astype(o_ref.dtype)
    for p in range(NP):
        p_ref[pl.ds(p, T, stride=NP), :] = _pack2(y[:, p * PACK:p * PACK + LANE], y[:, p * PACK + LANE:(p + 1) * PACK])


def _norm_mod_pack(s, w, scale, shift, nrows):
    mod_spec = pl.BlockSpec((1, 1, D), lambda i: (_mod_row(i, T), 0, 0))
    return pl.pallas_call(
        _norm_mod_pack_kernel,
        grid=(nrows // T,),
        in_specs=[pl.BlockSpec((T, D), lambda i: (i, 0)),
                  pl.BlockSpec((1, D), lambda i: (0, 0)),
                  mod_spec, mod_spec],
        out_specs=[pl.BlockSpec((T, D), lambda i: (i, 0)),
                   pl.BlockSpec((T * NP, LANE), lambda i: (i, 0))],
        out_shape=[jax.ShapeDtypeStruct((nrows, D), BF16),
                   jax.ShapeDtypeStruct((nrows * NP, LANE), jnp.uint32)],
        compiler_params=_cp("parallel"),
    )(s, w.reshape(1, D), scale, shift)


def _rms_kernel(s_ref, w_ref, o_ref):
    x = s_ref[...]
    o_ref[...] = x * lax.rsqrt(jnp.mean(x * x, axis=-1, keepdims=True) + RMS_EPS) * w_ref[...]


def _rms(s, w):
    nrows = s.shape[0]
    return pl.pallas_call(
        _rms_kernel,
        grid=(nrows // T,),
        in_specs=[pl.BlockSpec((T, D), lambda i: (i, 0)),
                  pl.BlockSpec((1, D), lambda i: (0, 0))],
        out_specs=pl.BlockSpec((T, D), lambda i: (i, 0)),
        out_shape=jax.ShapeDtypeStruct((nrows, D), F32),
        compiler_params=_cp("parallel"),
    )(s, w.reshape(1, D))


def _mm_kernel(a_ref, w_ref, o_ref):
    o_ref[...] = _dot(a_ref[...], w_ref[...]).astype(o_ref.dtype)


def _matmul(a, w, tn):
    m, k = a.shape
    n = w.shape[1]
    return pl.pallas_call(
        _mm_kernel,
        grid=(m // TM, n // tn),
        in_specs=[pl.BlockSpec((TM, k), lambda i, j: (i, 0)),
                  pl.BlockSpec((k, tn), lambda i, j: (0, j))],
        out_specs=pl.BlockSpec((TM, tn), lambda i, j: (i, j)),
        out_shape=jax.ShapeDtypeStruct((m, n), BF16),
        compiler_params=_cp("parallel", "arbitrary"),
    )(a, w)


def _norm_mm_kernel(a_ref, nw_ref, w_ref, o_ref):
    x = a_ref[...].astype(F32)
    y = x * lax.rsqrt(jnp.mean(x * x, axis=-1, keepdims=True) + RMS_EPS) * nw_ref[...]
    o_ref[...] = _dot(y.astype(BF16), w_ref[...]).astype(o_ref.dtype)


def _norm_mm(u, acol, kdim, nw, w, tn):
    m = u.shape[0]
    n = w.shape[1]
    return pl.pallas_call(
        _norm_mm_kernel,
        grid=(m // TM, n // tn),
        in_specs=[pl.BlockSpec((TM, kdim), lambda i, j: (i, acol)),
                  pl.BlockSpec((1, kdim), lambda i, j: (0, 0)),
                  pl.BlockSpec((kdim, tn), lambda i, j: (0, j))],
        out_specs=pl.BlockSpec((TM, tn), lambda i, j: (i, j)),
        out_shape=jax.ShapeDtypeStruct((m, n), BF16),
        compiler_params=_cp("parallel", "arbitrary"),
    )(u, nw.reshape(1, kdim), w)


def _mm2_res_kernel(a1_ref, a2_ref, w1_ref, w2_ref, r_ref, g_ref, o_ref):
    acc = _dot(a1_ref[...], w1_ref[...]) + _dot(a2_ref[...], w2_ref[...])
    o_ref[...] = r_ref[...] + g_ref[0] * acc


def _mm2_res(a1, a2, w, resid, gate, nrows):
    k1 = a1.shape[1]
    k2 = a2.shape[1]
    assert k1 == k2 and w.shape == (k1 + k2, D)
    tn = 512
    return pl.pallas_call(
        _mm2_res_kernel,
        grid=(nrows // TM, D // tn),
        in_specs=[pl.BlockSpec((TM, k1), lambda i, j: (i, 0)),
                  pl.BlockSpec((TM, k2), lambda i, j: (i, 0)),
                  pl.BlockSpec((k1, tn), lambda i, j: (0, j)),
                  pl.BlockSpec((k2, tn), lambda i, j: (1, j)),
                  pl.BlockSpec((TM, tn), lambda i, j: (i, j)),
                  pl.BlockSpec((1, 1, tn), lambda i, j: (_mod_row(i, TM), 0, j))],
        out_specs=pl.BlockSpec((TM, tn), lambda i, j: (i, j)),
        out_shape=jax.ShapeDtypeStruct((nrows, D), F32),
        compiler_params=_cp("parallel", "arbitrary"),
    )(a1, a2, w, w, resid, gate)


def _split3(x):
    hi = x.astype(BF16)
    r = x - hi.astype(F32)
    mid = r.astype(BF16)
    lo = (r - mid.astype(F32)).astype(BF16)
    return hi, mid, lo


def _hgrn_dir(q_raw, f_raw, v, lb, st_ref, o_ref, rev):
    c = A_CHUNK
    q = _silu(q_raw.astype(F32)) * (A_DK ** -0.5)
    f = lb + (1.0 - lb) * _sigmoid(f_raw.astype(F32))
    k = 1.0 - f
    logf = jnp.log(f)
    row = lax.broadcasted_iota(jnp.int32, (T, T), 0)
    col = lax.broadcasted_iota(jnp.int32, (T, T), 1)
    same = (row ^ col) < c
    tri = jnp.where(same & ((col >= row) if rev else (col <= row)), 1.0, 0.0).astype(BF16)
    hi, mid, lo = _split3(logf)
    cum = _dot(tri, hi) + _dot(tri, mid) + _dot(tri, lo)
    crow = lax.broadcasted_iota(jnp.int32, (c, c), 0)
    ccol = lax.broadcasted_iota(jnp.int32, (c, c), 1)
    tri_c = (ccol >= crow) if rev else (ccol <= crow)
    st = st_ref[...]
    nchunk = T // c
    for n in (range(nchunk - 1, -1, -1) if rev else range(nchunk)):
        sl = slice(n * c, (n + 1) * c)
        cum_c = cum[sl]
        ref = cum_c[c // 2 - 1:c // 2] if rev else cum_c[c // 2:c // 2 + 1]
        last = cum_c[0:1] if rev else cum_c[c - 1:c]
        qc, kc, vc = q[sl], k[sl], v[sl]
        qs = (qc * jnp.exp(cum_c - ref)).astype(BF16)
        ks = (kc * jnp.exp(ref - cum_c)).astype(BF16)
        att = jnp.where(tri_c, _dot_nt(qs, ks), 0.0)
        o = _dot(att.astype(BF16), vc)
        qd = (qc * jnp.exp(cum_c)).astype(BF16)
        kd = (kc * jnp.exp(last - cum_c)).astype(BF16)
        o = o + _dot_nt(qd, st.astype(BF16))
        st = st * jnp.exp(last) + lax.dot_general(vc, kd, (((0,), (0,)), ((), ())),
                                                  preferred_element_type=F32)
        o_ref[sl, :] = o.astype(o_ref.dtype)
    st_ref[...] = st


def _hgrn_kernel(qf, ff, vf, qb, fb, vb, lbf, lbb, of, ob, stf, stb):
    @pl.when(pl.program_id(2) == 0)
    def _():
        stf[...] = jnp.zeros_like(stf)
        stb[...] = jnp.zeros_like(stb)

    _hgrn_dir(qf[...], ff[...], vf[...], lbf[0], stf, of, False)
    _hgrn_dir(qb[...], fb[...], vb[...], lbb[0], stb, ob, True)


def _hgrn(u, lbf, lbb):
    assert LC == T
    xt = L // T
    n_rows = u.shape[0]
    hh = A_HEADS

    def blk_f(b, n):
        return jnp.where(n == 0, B * xt + b, b * xt + n - 1)

    def blk_b(b, n):
        return jnp.where(n == 0, B * xt + b, b * xt + xt - n)

    def spec(blk, cb):
        return pl.BlockSpec((T, LANE), lambda b, h, n: (blk(b, n), cb * hh + h))

    lb_spec = pl.BlockSpec((1, 1, LANE), lambda b, h, n: (h, 0, 0))
    return pl.pallas_call(
        _hgrn_kernel,
        grid=(B, hh, xt + 1),
        in_specs=[spec(blk_f, 0), spec(blk_f, 1), spec(blk_f, 3),
                  spec(blk_b, 0), spec(blk_b, 2), spec(blk_b, 3),
                  lb_spec, lb_spec],
        out_specs=[spec(blk_f, 0), spec(blk_b, 0)],
        out_shape=[jax.ShapeDtypeStruct((n_rows, A_KEY), BF16)] * 2,
        scratch_shapes=[pltpu.VMEM((LANE, LANE), F32)] * 2,
        compiler_params=_cp("parallel", "parallel", "arbitrary"),
    )(u, u, u, u, u, u, lbf, lbb)


def _hgrn_out_kernel(of_ref, ob_ref, g_ref, w_ref, y_ref):
    for h in range(A_HEADS):
        sl = slice(h * LANE, (h + 1) * LANE)
        o = of_ref[:, sl].astype(F32) + ob_ref[:, sl].astype(F32)
        o = o * lax.rsqrt(jnp.mean(o * o, axis=-1, keepdims=True) + RMS_EPS) * w_ref[...]
        y_ref[:, sl] = (o * _silu(g_ref[:, sl].astype(F32))).astype(y_ref.dtype)


def _hgrn_out(o_f, o_b, u, gnorm_w):
    n_rows = o_f.shape[0]
    blk = pl.BlockSpec((T, A_KEY), lambda i: (i, 0))
    return pl.pallas_call(
        _hgrn_out_kernel,
        grid=(n_rows // T,),
        in_specs=[blk, blk,
                  pl.BlockSpec((T, A_KEY), lambda i: (i, 4)),
                  pl.BlockSpec((1, LANE), lambda i: (0, 0))],
        out_specs=blk,
        out_shape=jax.ShapeDtypeStruct((n_rows, A_KEY), BF16),
        compiler_params=_cp("parallel"),
    )(o_f, o_b, u, gnorm_w.reshape(1, LANE))


def _rope_perm(n):
    return np.concatenate([np.arange(0, n, 2), np.arange(1, n, 2)])


def _rope_tables(rot_dim, with_ctx):
    half = rot_dim // 2
    rows = L // GRID_W
    row = jnp.repeat(jnp.arange(rows), GRID_W).astype(F32)
    col = jnp.tile(jnp.arange(GRID_W), rows).astype(F32)
    n_freq = rot_dim // 4
    inv = ROPE_THETA ** (-jnp.arange(n_freq, dtype=F32) / n_freq)
    ang = jnp.concatenate([row[:, None] * inv, col[:, None] * inv], axis=-1)
    cos, sin = jnp.cos(ang), jnp.sin(ang)
    zpad = jnp.zeros((L, LANE - rot_dim), F32)
    zh = jnp.zeros((L, half), F32)
    c = jnp.concatenate([cos, cos, zpad], axis=-1)
    s1 = jnp.concatenate([-sin, zh, zpad], axis=-1)
    s2 = jnp.concatenate([zh, sin, zpad], axis=-1)
    if with_ctx:
        ident = jnp.ones((T, LANE), F32)
        zero = jnp.zeros((T, LANE), F32)
        c = jnp.concatenate([ident, c], axis=0)
        s1 = jnp.concatenate([zero, s1], axis=0)
        s2 = jnp.concatenate([zero, s2], axis=0)
    return c, s1, s2


def _rope_apply(x, c, s1, s2, half):
    return x * c + pltpu.roll(x, LANE - half, 1) * s1 + pltpu.roll(x, half, 1) * s2


def _tab_blk(i):
    xt = L // T
    return jnp.where(i < B * xt, 1 + i % xt, 0)


def _krope_kernel(x_ref, c_ref, s1_ref, s2_ref, o_ref):
    x = x_ref[...].astype(F32)
    o_ref[...] = _rope_apply(x, c_ref[...], s1_ref[...], s2_ref[...], B_ROPE // 2).astype(o_ref.dtype)


def _krope(ub, tabs):
    n_rows = ub.shape[0]
    cb = ub.shape[1] // LANE - 1
    tspec = pl.BlockSpec((T, LANE), lambda i: (_tab_blk(i), 0))
    return pl.pallas_call(
        _krope_kernel,
        grid=(n_rows // T,),
        in_specs=[pl.BlockSpec((T, LANE), lambda i: (i, cb)), tspec, tspec, tspec],
        out_specs=pl.BlockSpec((T, LANE), lambda i: (i, 0)),
        out_shape=jax.ShapeDtypeStruct((n_rows, LANE), BF16),
        compiler_params=_cp("parallel"),
    )(ub, *tabs)


MLA_KV_CHUNK = 1024


def _softmax_step(q, k, v, m, l, acc):
    s = _dot_nt(q, k)
    m_new = jnp.maximum(m, jnp.max(s, axis=-1, keepdims=True))
    alpha = jnp.exp(m - m_new)
    p = jnp.exp(s - m_new)
    l = alpha * l + jnp.sum(p, axis=-1, keepdims=True)
    acc = alpha * acc + _dot(p.astype(BF16), v)
    return m_new, l, acc


def _mla_attn_kernel(q_ref, c_ref, s1_ref, s2_ref, kxn_ref, vx_ref, krx_ref, kcn_ref, vc_ref, krc_ref,
                     o_ref, kfx, kfc):
    qi = pl.program_id(2)
    xt = L // T

    @pl.when(qi == 0)
    def _():
        kfx[:, :B_NOPE] = kxn_ref[...]
        kfx[:, B_NOPE:] = krx_ref[...]
        kfc[:, :B_NOPE] = kcn_ref[...]
        kfc[:, B_NOPE:] = krc_ref[...]

    qr = _rope_apply(q_ref[:, B_NOPE:].astype(F32), c_ref[...], s1_ref[...], s2_ref[...], B_ROPE // 2)
    q = jnp.concatenate([q_ref[:, :B_NOPE], qr.astype(BF16)], axis=1)
    m0 = jnp.full((T, 1), NEG_BIG, F32)
    l0 = jnp.zeros((T, 1), F32)
    a0 = jnp.zeros((T, B_VDIM), F32)

    @pl.when(qi < xt)
    def _():
        m, l, acc = _softmax_step(q, kfc[...], vc_ref[...], m0, l0, a0)
        for n in range(L // MLA_KV_CHUNK):
            sl = slice(n * MLA_KV_CHUNK, (n + 1) * MLA_KV_CHUNK)
            m, l, acc = _softmax_step(q, kfx[sl, :], vx_ref[sl, :], m, l, acc)
        o_ref[...] = (acc / l).astype(o_ref.dtype)

    @pl.when(qi == xt)
    def _():
        m, l, acc = _softmax_step(q, kfc[...], vc_ref[...], m0, l0, a0)
        o_ref[...] = (acc / l).astype(o_ref.dtype)


def _mla_attn(q, kv, kr, tabs):
    xt = L // T
    n_rows = q.shape[0]
    ctx0 = B * L // LC

    def rowblk(b, qi):
        return jnp.where(qi < xt, b * xt + qi, B * xt + b)

    tspec = pl.BlockSpec((T, LANE), lambda b, h, qi: (jnp.where(qi < xt, 1 + qi, 0), 0))
    return pl.pallas_call(
        _mla_attn_kernel,
        grid=(B, B_HEADS, xt + 1),
        in_specs=[pl.BlockSpec((T, B_QPAD), lambda b, h, qi: (rowblk(b, qi), h)),
                  tspec, tspec, tspec,
                  pl.BlockSpec((L, LANE), lambda b, h, qi: (b, 2 * h)),
                  pl.BlockSpec((L, LANE), lambda b, h, qi: (b, 2 * h + 1)),
                  pl.BlockSpec((L, LANE), lambda b, h, qi: (b, 0)),
                  pl.BlockSpec((LC, LANE), lambda b, h, qi: (ctx0 + b, 2 * h)),
                  pl.BlockSpec((LC, LANE), lambda b, h, qi: (ctx0 + b, 2 * h + 1)),
                  pl.BlockSpec((LC, LANE), lambda b, h, qi: (ctx0 + b, 0))],
        out_specs=pl.BlockSpec((T, B_VDIM), lambda b, h, qi: (rowblk(b, qi), h)),
        out_shape=jax.ShapeDtypeStruct((n_rows, B_HEADS * B_VDIM), BF16),
        scratch_shapes=[pltpu.VMEM((L, B_QPAD), BF16), pltpu.VMEM((LC, B_QPAD), BF16)],
        compiler_params=_cp("parallel", "parallel", "arbitrary"),
    )(q, *tabs, kv, kv, kr, kv, kv, kr)


def _rope_qk_kernel(x_ref, c_ref, s1_ref, s2_ref, o_ref):
    c, s1, s2 = c_ref[...], s1_ref[...], s2_ref[...]
    for h in range(x_ref.shape[1] // LANE):
        sl = slice(h * LANE, (h + 1) * LANE)
        o_ref[:, sl] = _rope_apply(x_ref[:, sl].astype(F32), c, s1, s2, HEAD_DIM // 2).astype(o_ref.dtype)


def _rope_qk(ux, tabs):
    width = C_Q + C_KV
    tn = C_KV
    xt = L // T
    tspec = pl.BlockSpec((T, LANE), lambda i, j: (i % xt, 0))
    return pl.pallas_call(
        _rope_qk_kernel,
        grid=(B * xt, width // tn),
        in_specs=[pl.BlockSpec((T, tn), lambda i, j: (i, j)), tspec, tspec, tspec],
        out_specs=pl.BlockSpec((T, tn), lambda i, j: (i, j)),
        out_shape=jax.ShapeDtypeStruct((B * L, width), BF16),
        compiler_params=_cp("parallel", "parallel"),
    )(ux, *tabs)


def _win_kernel(sink_ref, q_ref, kx_ref, vx_ref, kc_ref, vc_ref, o_ref):
    g = pl.program_id(1)
    t = pl.program_id(2)
    span = 2 * T
    start = pl.multiple_of(jnp.clip(t * T - C_WINDOW, 0, L - span), C_WINDOW)
    kw = kx_ref[pl.ds(start, span), :]
    vw = vx_ref[pl.ds(start, span), :]
    qpos = t * T + lax.broadcasted_iota(jnp.int32, (T, span), 0)
    kpos = start + lax.broadcasted_iota(jnp.int32, (T, span), 1)
    valid = jnp.abs(qpos - kpos) <= C_WINDOW
    kc = kc_ref[...]
    vc = vc_ref[...]
    for r in range(C_GROUP):
        sl = slice(r * HEAD_DIM, (r + 1) * HEAD_DIM)
        q = q_ref[:, sl]
        sw = jnp.where(valid, _dot_nt(q, kw), NEG_BIG)
        sc = _dot_nt(q, kc)
        sk = sink_ref[g * C_GROUP + r]
        m = jnp.maximum(jnp.maximum(jnp.max(sw, axis=-1, keepdims=True),
                                    jnp.max(sc, axis=-1, keepdims=True)), sk)
        pw = jnp.exp(sw - m)
        pc = jnp.exp(sc - m)
        l = jnp.sum(pw, axis=-1, keepdims=True) + jnp.sum(pc, axis=-1, keepdims=True) + jnp.exp(sk - m)
        o = _dot(pw.astype(BF16), vw) + _dot(pc.astype(BF16), vc)
        o_ref[:, sl] = (o / l).astype(o_ref.dtype)


def _win_attn(qk, ux, sink):
    assert T == 2 * C_WINDOW
    xt = L // T
    ctx0 = B * L // LC
    kcol = C_Q // LANE
    vcol = (C_Q + C_KV) // LANE
    gw = C_GROUP * HEAD_DIM
    return pl.pallas_call(
        _win_kernel,
        grid=(B, C_KV_HEADS, xt),
        in_specs=[pl.BlockSpec(memory_space=pltpu.SMEM),
                  pl.BlockSpec((T, gw), lambda b, g, t: (b * xt + t, g)),
                  pl.BlockSpec((L, LANE), lambda b, g, t: (b, kcol + g)),
                  pl.BlockSpec((L, LANE), lambda b, g, t: (b, vcol + g)),
                  pl.BlockSpec((LC, LANE), lambda b, g, t: (ctx0 + b, kcol + g)),
                  pl.BlockSpec((LC, LANE), lambda b, g, t: (ctx0 + b, vcol + g))],
        out_specs=pl.BlockSpec((T, gw), lambda b, g, t: (b * xt + t, g)),
        out_shape=jax.ShapeDtypeStruct((B * L, C_Q), BF16),
        compiler_params=_cp("parallel", "parallel", "arbitrary"),
    )(sink, qk, qk, ux, ux, ux)


CONV_TN = 512
SUBLANE = 8


def _conv_kernel(bg_ref, cg_ref, hh_ref, cgp_ref, hhp_ref, cgn_ref, hhn_ref, w_ref, o_ref):
    xt = L // T
    t = pl.program_id(0) % xt
    z = cg_ref[...].astype(F32) * hh_ref[...].astype(F32)
    zp = cgp_ref[SUBLANE - 1:SUBLANE, :].astype(F32) * hhp_ref[SUBLANE - 1:SUBLANE, :].astype(F32)
    zn = cgn_ref[0:1, :].astype(F32) * hhn_ref[0:1, :].astype(F32)
    zp = jnp.where(t > 0, zp, 0.0)
    zn = jnp.where(t < xt - 1, zn, 0.0)
    rows = lax.broadcasted_iota(jnp.int32, (T, CONV_TN), 0)
    z_prev = jnp.where(rows == 0, zp, pltpu.roll(z, 1, 0))
    z_next = jnp.where(rows == T - 1, zn, pltpu.roll(z, T - 1, 0))
    y = w_ref[0:1, :] * z_prev + w_ref[1:2, :] * z + w_ref[2:3, :] * z_next
    o_ref[...] = (bg_ref[...].astype(F32) * y).astype(o_ref.dtype)


def _conv(ux, conv_w):
    xt = L // T
    nx = B * xt
    c0 = C_IN // CONV_TN
    nc = D_CH // CONV_TN
    rb = T // SUBLANE

    def main(cb):
        return pl.BlockSpec((T, CONV_TN), lambda i, j: (i, c0 + cb * nc + j))

    def prev(cb):
        return pl.BlockSpec((SUBLANE, CONV_TN), lambda i, j: (jnp.maximum(i * rb - 1, 0), c0 + cb * nc + j))

    def nxt(cb):
        return pl.BlockSpec((SUBLANE, CONV_TN),
                            lambda i, j: (jnp.minimum((i + 1) * rb, nx * rb - 1), c0 + cb * nc + j))

    return pl.pallas_call(
        _conv_kernel,
        grid=(nx, nc),
        in_specs=[main(0), main(1), main(2), prev(1), prev(2), nxt(1), nxt(2),
                  pl.BlockSpec((D_CONV, CONV_TN), lambda i, j: (0, j))],
        out_specs=pl.BlockSpec((T, CONV_TN), lambda i, j: (i, j)),
        out_shape=jax.ShapeDtypeStruct((B * L, D_CH), BF16),
        compiler_params=_cp("parallel", "parallel"),
    )(ux, ux, ux, ux, ux, ux, ux, conv_w)


def _router_kernel(h_ref, wr_ref, b_ref, i_ref, g_ref):
    tm = h_ref.shape[0]
    scores = _sigmoid(_dot_nt(wr_ref[...], h_ref[...]))
    biased = scores + b_ref[...]
    ng = N_GROUPS
    sl = [biased[j * ng:(j + 1) * ng] for j in range(GROUP_SIZE)]
    sc = [scores[j * ng:(j + 1) * ng] for j in range(GROUP_SIZE)]
    m1 = sl[0]
    m2 = jnp.full_like(m1, -jnp.inf)
    for v in sl[1:]:
        m2 = jnp.maximum(m2, jnp.minimum(m1, v))
        m1 = jnp.maximum(m1, v)
    gs = m1 + m2
    gidx = lax.broadcasted_iota(jnp.int32, (ng, tm), 0)
    rank = jnp.zeros((ng, tm), jnp.int32)
    for k in range(1, ng):
        r = pltpu.roll(gs, k, 0)
        rank += jnp.where((r > gs) | ((r == gs) & (gidx >= k)), 1, 0)
    keep = rank < TOPK_GROUPS
    ms = [jnp.where(keep, v, -jnp.inf) for v in sl]
    rolled = [[ms[j] if k == 0 else pltpu.roll(ms[j], k, 0) for k in range(ng)] for j in range(GROUP_SIZE)]
    tot = jnp.zeros((ng, tm), F32)
    picked = []
    ranks = []
    for j in range(GROUP_SIZE):
        cnt = jnp.zeros((ng, tm), jnp.int32)
        for jp in range(GROUP_SIZE):
            for k in range(ng):
                r = rolled[jp][k]
                if k == 0:
                    if jp == j:
                        continue
                    beats = (r >= ms[j]) if jp < j else (r > ms[j])
                else:
                    beats = (r > ms[j]) | ((r == ms[j]) & (gidx >= k))
                cnt += jnp.where(beats, 1, 0)
        w = jnp.where(cnt < TOP_K, sc[j], 0.0)
        picked.append(w)
        ranks.append(cnt)
        tot = tot + w
    tot = jnp.sum(tot, axis=0, keepdims=True)
    wn = [p / tot * ROUTE_SCALE for p in picked]
    eid = [(gidx * GROUP_SIZE + j).astype(F32) for j in range(GROUP_SIZE)]
    for k in range(TOP_K):
        ik = jnp.zeros((ng, tm), F32)
        wk = jnp.zeros((ng, tm), F32)
        for j in range(GROUP_SIZE):
            hit = ranks[j] == k
            ik += jnp.where(hit, eid[j], 0.0)
            wk += jnp.where(hit, wn[j], 0.0)
        i_ref[k:k + 1, :] = jnp.sum(ik, axis=0, keepdims=True).astype(jnp.int32)
        g_ref[k:k + 1, :] = jnp.sum(wk, axis=0, keepdims=True)


def _router(h, wr_t, bias, nrows):
    out = pl.BlockSpec((TOP_K, TM), lambda i: (0, i))
    return pl.pallas_call(
        _router_kernel,
        grid=(nrows // TM,),
        in_specs=[pl.BlockSpec((TM, D), lambda i: (i, 0)),
                  pl.BlockSpec((N_EXPERTS, D), lambda i: (0, 0)),
                  pl.BlockSpec((N_EXPERTS, 1), lambda i: (0, 0))],
        out_specs=[out, out],
        out_shape=[jax.ShapeDtypeStruct((TOP_K, nrows), jnp.int32),
                   jax.ShapeDtypeStruct((TOP_K, nrows), F32)],
        compiler_params=_cp("parallel"),
    )(h, wr_t, bias)


def _router_order():
    r = np.arange(N_EXPERTS)
    return (r % N_GROUPS) * GROUP_SIZE + r // N_GROUPS


def _glu_kernel(x_ref, wg_ref, wu_ref, wd_ref, o_ref):
    x = x_ref[...]
    hid = _silu(_dot(x, wg_ref[...])) * _dot(x, wu_ref[...])
    o_ref[...] = _dot(hid.astype(BF16), wd_ref[...]).astype(o_ref.dtype)


def _glu(h, wg, wu, wd, nrows):
    return pl.pallas_call(
        _glu_kernel,
        grid=(nrows // TM,),
        in_specs=[pl.BlockSpec((TM, D), lambda i: (i, 0)),
                  pl.BlockSpec((D, D_EXPERT), lambda i: (0, 0)),
                  pl.BlockSpec((D, D_EXPERT), lambda i: (0, 0)),
                  pl.BlockSpec((D_EXPERT, D), lambda i: (0, 0))],
        out_specs=pl.BlockSpec((TM, D), lambda i: (i, 0)),
        out_shape=jax.ShapeDtypeStruct((nrows, D), BF16),
        compiler_params=_cp("parallel"),
    )(h, wg, wu, wd)


TME = 256
TD = 512
TC = 128


def _route_plan(idx, nrows):
    nt = (nrows * TOP_K) // TME + N_EXPERTS
    onehot = idx[:, :, None] == jnp.arange(N_EXPERTS, dtype=jnp.int32)[None, None, :]
    cs = jnp.cumsum(jnp.any(onehot, axis=1).astype(jnp.int32), axis=0)
    counts = cs[-1]
    padded = ((counts + TME - 1) // TME) * TME
    ends = jnp.cumsum(padded)
    off = ends - padded
    dest = (off[idx] + jnp.take_along_axis(cs, idx, axis=1) - 1).astype(jnp.int32)
    tiles = jnp.arange(nt, dtype=jnp.int32)
    used = ends[-1] // TME
    t_expert = jnp.minimum(jnp.searchsorted(ends // TME, tiles, side="right"), N_EXPERTS - 1).astype(jnp.int32)
    t_valid = (tiles < used).astype(jnp.int32)
    t_first = ((tiles * TME == off[t_expert]) & (tiles < used)).astype(jnp.int32)
    t_src = jnp.minimum(tiles, used - 1).astype(jnp.int32)
    return dict(nt=nt, dest=dest, used=used.reshape(1).astype(jnp.int32),
                pad_start=(off + counts).astype(jnp.int32),
                pad_cnt=(padded - counts).astype(jnp.int32),
                t_expert=t_expert, t_valid=t_valid, t_first=t_first, t_src=t_src)


def _slab(ref, row):
    return ref.at[pl.ds(pl.multiple_of(row * NP, NP), NP), :]


def _dispatch_kernel(ps_ref, pc_ref, used_ref, dest_ref, hp_ref, xs_ref, sem, *, nt):
    i = pl.program_id(0)

    def body(j, c):
        for k in range(TOP_K):
            pltpu.make_async_copy(_slab(hp_ref, i * TD + j), _slab(xs_ref, dest_ref[0, 0, j * TOP_K + k]), sem).start()
        return c

    lax.fori_loop(0, TD, body, 0)
    nrow = TD * TOP_K * NP
    pltpu.make_async_copy(xs_ref.at[pl.ds(0, nrow), :], xs_ref.at[pl.ds(0, nrow), :], sem).wait()

    @pl.when(i == 0)
    def _():
        def pad_copy(e):
            n = pc_ref[e] * NP
            return pltpu.make_async_copy(hp_ref.at[pl.ds(0, n), :],
                                         xs_ref.at[pl.ds(pl.multiple_of(ps_ref[e] * NP, NP), n), :], sem)

        def tail_copy(t):
            n = TME * NP
            return pltpu.make_async_copy(hp_ref.at[pl.ds(0, n), :],
                                         xs_ref.at[pl.ds(pl.multiple_of(t * n, n), n), :], sem)

        def over(lo, hi, copy, pred):
            def start(e, c):
                @pl.when(pred(e))
                def _():
                    copy(e).start()
                return c

            def wait(e, c):
                @pl.when(pred(e))
                def _():
                    copy(e).wait()
                return c

            lax.fori_loop(lo, hi, start, 0)
            lax.fori_loop(lo, hi, wait, 0)

        over(0, N_EXPERTS, pad_copy, lambda e: pc_ref[e] > 0)
        over(0, nt, tail_copy, lambda t: t >= used_ref[0])


def _dispatch(hp, plan, nrows):
    dest = plan["dest"].reshape(nrows // TD, 1, TD * TOP_K)
    return pl.pallas_call(
        functools.partial(_dispatch_kernel, nt=plan["nt"]),
        grid_spec=pltpu.PrefetchScalarGridSpec(
            num_scalar_prefetch=3,
            grid=(nrows // TD,),
            in_specs=[pl.BlockSpec((1, 1, TD * TOP_K), lambda i, ps, pc, us: (i, 0, 0), memory_space=pltpu.SMEM),
                      pl.BlockSpec(memory_space=pl.ANY)],
            out_specs=pl.BlockSpec(memory_space=pl.ANY),
            scratch_shapes=[pltpu.SemaphoreType.DMA(())]),
        out_shape=jax.ShapeDtypeStruct((plan["nt"] * TME * NP, LANE), jnp.uint32),
        compiler_params=pltpu.CompilerParams(dimension_semantics=("arbitrary",), disable_bounds_checks=True),
    )(plan["pad_start"], plan["pad_cnt"], plan["used"], dest, hp)


def _expert_kernel(te_ref, tv_ref, tf_ref, ts_ref, x_ref, wg_ref, wu_ref, wd_ref, o_ref, wgu_b, wd_b):
    i = pl.program_id(0)

    @pl.when(tv_ref[i] == 0)
    def _():
        o_ref[...] = jnp.zeros_like(o_ref)

    @pl.when(tv_ref[i] == 1)
    def _():
        @pl.when(tf_ref[i] == 1)
        def _():
            wgu_b[:, :D_EXPERT] = wg_ref[0].astype(BF16)
            wgu_b[:, D_EXPERT:] = wu_ref[0].astype(BF16)
            wd_b[...] = wd_ref[0].astype(BF16)

        pieces = []
        for p in range(NP):
            lo, hi = _unpack2(x_ref[pl.ds(p, TME, stride=NP), :])
            pieces += [lo.astype(BF16), hi.astype(BF16)]
        a = _dot(jnp.concatenate(pieces, axis=1), wgu_b[...])
        hid = _silu(a[:, :D_EXPERT]) * a[:, D_EXPERT:]
        y = _dot(hid.astype(BF16), wd_b[...])
        for p in range(NP):
            o_ref[pl.ds(p, TME, stride=NP), :] = _pack2(y[:, p * PACK:p * PACK + LANE],
                                                        y[:, p * PACK + LANE:(p + 1) * PACK])


def _experts(xs, plan, wg, wu, wd):
    nt = plan["nt"]
    rows = pl.BlockSpec((TME * NP, LANE), lambda i, te, tv, tf, ts: (i, 0))
    return pl.pallas_call(
        _expert_kernel,
        grid_spec=pltpu.PrefetchScalarGridSpec(
            num_scalar_prefetch=4,
            grid=(nt,),
            in_specs=[pl.BlockSpec((TME * NP, LANE), lambda i, te, tv, tf, ts: (ts[i], 0)),
                      pl.BlockSpec((1, D, D_EXPERT), lambda i, te, tv, tf, ts: (te[i], 0, 0)),
                      pl.BlockSpec((1, D, D_EXPERT), lambda i, te, tv, tf, ts: (te[i], 0, 0)),
                      pl.BlockSpec((1, D_EXPERT, D), lambda i, te, tv, tf, ts: (te[i], 0, 0))],
            out_specs=rows,
            scratch_shapes=[pltpu.VMEM((D, 2 * D_EXPERT), BF16), pltpu.VMEM((D_EXPERT, D), BF16)]),
        out_shape=jax.ShapeDtypeStruct((nt * TME * NP, LANE), jnp.uint32),
        compiler_params=pltpu.CompilerParams(dimension_semantics=("arbitrary",),
                                             vmem_limit_bytes=EXPERT_VMEM_LIMIT_BYTES),
    )(plan["t_expert"], plan["t_valid"], plan["t_first"], plan["t_src"], xs, wg, wu, wd)


def _combine_kernel(dcur_ref, dnxt_ref, w_ref, r_ref, sh_ref, g_ref, ys_ref, o_ref, buf0, buf1, sem):
    i = pl.program_id(0)
    last = pl.num_programs(0) - 1

    def issue(d_ref, buf, slot):
        def body(j, c):
            for k in range(TOP_K):
                pltpu.make_async_copy(_slab(ys_ref, d_ref[0, 0, j * TOP_K + k]), _slab(buf, k * TC + j),
                                      sem.at[slot]).start()
            return c

        lax.fori_loop(0, TC, body, 0)

    def drain(buf, slot):
        pltpu.make_async_copy(ys_ref.at[pl.ds(0, TC * TOP_K * NP), :], buf, sem.at[slot]).wait()

    def reduce(buf):
        gates = [w_ref[:, k:k + 1] for k in range(TOP_K)]
        gmod = g_ref[0]
        for p in range(NP):
            lo_sl = slice(p * PACK, p * PACK + LANE)
            hi_sl = slice(p * PACK + LANE, (p + 1) * PACK)
            lo_acc = sh_ref[:, lo_sl].astype(F32)
            hi_acc = sh_ref[:, hi_sl].astype(F32)
            for k in range(TOP_K):
                lo, hi = _unpack2(buf[pl.ds(k * TC * NP + p, TC, stride=NP), :])
                lo_acc += gates[k] * lo
                hi_acc += gates[k] * hi
            o_ref[:, lo_sl] = r_ref[:, lo_sl] + gmod[:, lo_sl] * lo_acc
            o_ref[:, hi_sl] = r_ref[:, hi_sl] + gmod[:, hi_sl] * hi_acc

    @pl.when(i == 0)
    def _():
        issue(dcur_ref, buf0, 0)

    @pl.when(i % 2 == 0)
    def _():
        @pl.when(i < last)
        def _():
            issue(dnxt_ref, buf1, 1)

        drain(buf0, 0)
        reduce(buf0)

    @pl.when(i % 2 == 1)
    def _():
        @pl.when(i < last)
        def _():
            issue(dnxt_ref, buf0, 0)

        drain(buf1, 1)
        reduce(buf1)


def _combine(resid, ys, shared, w, plan, gate, nrows):
    nsteps = nrows // TC
    dest = plan["dest"].reshape(nsteps, 1, TC * TOP_K)
    blk = pl.BlockSpec((TC, D), lambda i: (i, 0))
    return pl.pallas_call(
        _combine_kernel,
        grid=(nsteps,),
        in_specs=[pl.BlockSpec((1, 1, TC * TOP_K), lambda i: (i, 0, 0), memory_space=pltpu.SMEM),
                  pl.BlockSpec((1, 1, TC * TOP_K), lambda i: (jnp.minimum(i + 1, nsteps - 1), 0, 0),
                               memory_space=pltpu.SMEM),
                  pl.BlockSpec((TC, TOP_K), lambda i: (i, 0)),
                  blk, blk,
                  pl.BlockSpec((1, 1, D), lambda i: (_mod_row(i, TC), 0, 0)),
                  pl.BlockSpec(memory_space=pl.ANY)],
        out_specs=blk,
        out_shape=jax.ShapeDtypeStruct((nrows, D), F32),
        scratch_shapes=[pltpu.VMEM((TC * TOP_K * NP, LANE), jnp.uint32),
                        pltpu.VMEM((TC * TOP_K * NP, LANE), jnp.uint32),
                        pltpu.SemaphoreType.DMA((2,))],
        compiler_params=pltpu.CompilerParams(dimension_semantics=("arbitrary",),
                                             vmem_limit_bytes=VMEM_LIMIT_BYTES, disable_bounds_checks=True),
    )(dest, dest, w, resid, shared, gate, ys)


def _moe(s, h, hp, gate, nrows, router_w, router_bias, wg, wu, wd, sg, su, sd):
    order = _router_order()
    wr_t = router_w.T[order].astype(BF16)
    bias = router_bias[order].reshape(N_EXPERTS, 1).astype(F32)
    idx_t, w_t = _router(h, wr_t, bias, nrows)
    plan = _route_plan(idx_t.T, nrows)
    xs = _dispatch(hp, plan, nrows)
    ys = _experts(xs, plan, wg, wu, wd)
    shared = _glu(h, sg.astype(BF16), su.astype(BF16), sd.astype(BF16), nrows)
    return _combine(s, ys, shared, w_t.T, plan, gate, nrows)


def _mla_in_weights(w_in):
    wb = w_in[:, A_IN:]
    kr = wb[:, B_Q_RANK + B_KV_RANK:][:, _rope_perm(B_ROPE)]
    pad = jnp.zeros((D, LANE - B_ROPE), w_in.dtype)
    return jnp.concatenate([wb[:, :B_Q_RANK + B_KV_RANK], kr, pad], axis=1).astype(BF16)


def _mla_uq_weights(w_uq):
    scale = (B_NOPE + B_ROPE) ** -0.5
    w = (w_uq * scale).reshape(B_Q_RANK, B_HEADS, B_NOPE + B_ROPE)
    rope = w[:, :, B_NOPE:][:, :, _rope_perm(B_ROPE)]
    pad = jnp.zeros((B_Q_RANK, B_HEADS, B_QPAD - B_NOPE - B_ROPE), w.dtype)
    return jnp.concatenate([w[:, :, :B_NOPE], rope, pad], axis=-1).reshape(B_Q_RANK, B_HEADS * B_QPAD).astype(BF16)


def _cd_in_weights(w_in):
    perm = _rope_perm(HEAD_DIM)
    q = (w_in[:, :C_Q] * HEAD_DIM ** -0.5).reshape(D, C_HEADS, HEAD_DIM)[:, :, perm].reshape(D, C_Q)
    k = w_in[:, C_Q:C_Q + C_KV].reshape(D, C_KV_HEADS, HEAD_DIM)[:, :, perm].reshape(D, C_KV)
    return jnp.concatenate([q, k, w_in[:, C_Q + C_KV:]], axis=1).astype(BF16)


def kernel(x, c, ctx, c_ctx, ada_w, ada_b, norm_mix_w, norm_ffn_w, ab_w_in, ab_w_out, hgrn_lb_logits, hgrn_gnorm_w, mla_q_norm_w, mla_w_uq, mla_kv_norm_w, mla_w_ukv, cd_w_in, cd_w_out, gqa_sink, conv_w, router_w, router_bias, exp_w_gate, exp_w_up, exp_w_down, sh_w_gate, sh_w_up, sh_w_down, final_norm_w):
    nx = B * L
    n = nx + B * LC
    s = jnp.concatenate([x.reshape(nx, D), ctx.reshape(B * LC, D)], axis=0)
    lb_all = jnp.cumsum(jax.nn.softmax(hgrn_lb_logits.astype(F32), axis=0), axis=0)
    cond = jnp.concatenate([c, c_ctx[None, :], jnp.zeros((8 - B - 1, D), F32)], axis=0)

    def modulation(i):
        m = _ada(cond, ada_w[i], ada_b[i].reshape(1, N_MOD * D)).reshape(8, N_MOD, D)
        return [m[:B + 1, k, :].reshape(B + 1, 1, D) for k in range(N_MOD)]

    def moe_args(i):
        return (router_w[i], router_bias[i], exp_w_gate[i], exp_w_up[i], exp_w_down[i],
                sh_w_gate[i], sh_w_up[i], sh_w_down[i])

    mod = modulation(0)
    h = _norm_mod(s, norm_mix_w[0], mod[1], mod[0], n)
    ua = _matmul(h, ab_w_in[0][:, :A_IN].astype(BF16), 1024)
    ub = _matmul(h, _mla_in_weights(ab_w_in[0]), B_Q_RANK + B_KV_RANK + LANE)
    lb = lb_all[0].reshape(2, A_HEADS, 1, A_DK)
    o_f, o_b = _hgrn(ua, lb[0], lb[1])
    ya = _hgrn_out(o_f, o_b, ua, hgrn_gnorm_w[0])
    tabs = _rope_tables(B_ROPE, True)
    q = _norm_mm(ub, 0, B_Q_RANK, mla_q_norm_w[0], _mla_uq_weights(mla_w_uq[0]), 1024)
    kv = _norm_mm(ub, B_Q_RANK // B_KV_RANK, B_KV_RANK, mla_kv_norm_w[0], mla_w_ukv[0].astype(BF16), 1024)
    kr = _krope(ub, tabs)
    yb = _mla_attn(q, kv, kr, tabs)
    s = _mm2_res(ya, yb, ab_w_out[0].astype(BF16), s, mod[2], n)
    h, hp = _norm_mod_pack(s, norm_ffn_w[0], mod[4], mod[3], n)
    s = _moe(s, h, hp, mod[5], n, *moe_args(0))

    mod = modulation(1)
    h = _norm_mod(s, norm_mix_w[1], mod[1], mod[0], n)
    ux = _matmul(h, _cd_in_weights(cd_w_in[0]), 1024)
    qk = _rope_qk(ux, _rope_tables(HEAD_DIM, False))
    att = _win_attn(qk, ux, gqa_sink[0].astype(F32))
    cv = _conv(ux, conv_w[0])
    s = _mm2_res(att, cv, cd_w_out[0].astype(BF16), s, mod[2], nx)
    h, hp = _norm_mod_pack(s, norm_ffn_w[1], mod[4], mod[3], nx)
    s = _moe(s, h, hp, mod[5], nx, *moe_args(1))
    return _rms(s, final_norm_w).reshape(B, L, D)
```

```python
import functools

import jax
import jax.numpy as jnp
import numpy as np
from jax import lax
from jax.experimental import pallas as pl
from jax.experimental.pallas import tpu as pltpu

F32 = jnp.float32
BF16 = jnp.bfloat16

D = 4096
B = 2
L = 8192
LC = 256
GRID_W = 64
RMS_EPS = 1e-6
ROPE_THETA = 10000.0
N_MOD = 6

A_HEADS = 16
A_DK = 128
A_KEY = A_HEADS * A_DK
A_CHUNK = 32
A_IN = 5 * A_KEY

B_HEADS = 16
B_Q_RANK = 1024
B_KV_RANK = 512
B_NOPE = 128
B_ROPE = 64
B_VDIM = 128
B_QPAD = 256

C_HEADS = 16
C_KV_HEADS = 4
C_GROUP = C_HEADS // C_KV_HEADS
C_WINDOW = 128
HEAD_DIM = 128
C_Q = C_HEADS * HEAD_DIM
C_KV = C_KV_HEADS * HEAD_DIM
C_IN = C_Q + 2 * C_KV
D_CH = 2048
D_CONV = 3

N_EXPERTS = 64
TOP_K = 8
N_GROUPS = 8
TOPK_GROUPS = 4
GROUP_SIZE = N_EXPERTS // N_GROUPS
D_EXPERT = 256
ROUTE_SCALE = 2.5

T = 256
TM = 512
LANE = 128
VMEM_LIMIT_BYTES = 48 * 1024 * 1024
EXPERT_VMEM_LIMIT_BYTES = 56 * 1024 * 1024
NEG_BIG = -1e30
PACK = 2 * LANE
NP = D // PACK


def _cp(*sem):
    return pltpu.CompilerParams(dimension_semantics=sem, vmem_limit_bytes=VMEM_LIMIT_BYTES)


def _sigmoid(x):
    return 1.0 / (1.0 + jnp.exp(-x))


def _silu(x):
    return x * _sigmoid(x)


def _dot(a, b):
    return jnp.dot(a, b, preferred_element_type=F32)


def _dot_nt(a, b):
    return lax.dot_general(a, b, (((1,), (1,)), ((), ())), preferred_element_type=F32)


def _mod_row(i, tile):
    return jnp.where(i < (B * L) // tile, i // (L // tile), B)


def _ada_kernel(a_ref, w_ref, b_ref, o_ref):
    a = _silu(a_ref[...]).astype(BF16)
    o_ref[...] = _dot(a, w_ref[...].astype(BF16)) + b_ref[...]


def _ada(a, w, bias):
    n = w.shape[1]
    tn = 512
    return pl.pallas_call(
        _ada_kernel,
        grid=(n // tn,),
        in_specs=[pl.BlockSpec((8, D), lambda j: (0, 0)),
                  pl.BlockSpec((D, tn), lambda j: (0, j)),
                  pl.BlockSpec((1, tn), lambda j: (0, j))],
        out_specs=pl.BlockSpec((8, tn), lambda j: (0, j)),
        out_shape=jax.ShapeDtypeStruct((8, n), F32),
        compiler_params=_cp("parallel"),
    )(a, w, bias)


def _norm_mod_kernel(s_ref, w_ref, sc_ref, sh_ref, o_ref):
    x = s_ref[...]
    y = x * lax.rsqrt(jnp.mean(x * x, axis=-1, keepdims=True) + RMS_EPS) * w_ref[...]
    o_ref[...] = (y * (1.0 + sc_ref[0]) + sh_ref[0]).astype(o_ref.dtype)


def _norm_mod(s, w, scale, shift, nrows):
    mod_spec = pl.BlockSpec((1, 1, D), lambda i: (_mod_row(i, T), 0, 0))
    return pl.pallas_call(
        _norm_mod_kernel,
        grid=(nrows // T,),
        in_specs=[pl.BlockSpec((T, D), lambda i: (i, 0)),
                  pl.BlockSpec((1, D), lambda i: (0, 0)),
                  mod_spec, mod_spec],
        out_specs=pl.BlockSpec((T, D), lambda i: (i, 0)),
        out_shape=jax.ShapeDtypeStruct((nrows, D), BF16),
        compiler_params=_cp("parallel"),
    )(s, w.reshape(1, D), scale, shift)


def _pack2(lo, hi):
    lo_b = lax.bitcast_convert_type(lo.astype(BF16).astype(F32), jnp.uint32) >> 16
    hi_b = lax.bitcast_convert_type(hi.astype(BF16).astype(F32), jnp.uint32)
    return hi_b | lo_b


def _unpack2(w):
    lo = lax.bitcast_convert_type(w << 16, F32)
    hi = lax.bitcast_convert_type(w & jnp.uint32(0xFFFF0000), F32)
    return lo, hi


def _norm_mod_pack_kernel(s_ref, w_ref, sc_ref, sh_ref, o_ref, p_ref):
    x = s_ref[...]
    y = x * lax.rsqrt(jnp.mean(x * x, axis=-1, keepdims=True) + RMS_EPS) * w_ref[...]
    y = y * (1.0 + sc_ref[0]) + sh_ref[0]
    o_ref[...] = y.astype(o_ref.dtype)
    for p in range(NP):
        p_ref[pl.ds(p, T, stride=NP), :] = _pack2(y[:, p * PACK:p * PACK + LANE], y[:, p * PACK + LANE:(p + 1) * PACK])


def _norm_mod_pack(s, w, scale, shift, nrows):
    mod_spec = pl.BlockSpec((1, 1, D), lambda i: (_mod_row(i, T), 0, 0))
    return pl.pallas_call(
        _norm_mod_pack_kernel,
        grid=(nrows // T,),
        in_specs=[pl.BlockSpec((T, D), lambda i: (i, 0)),
                  pl.BlockSpec((1, D), lambda i: (0, 0)),
                  mod_spec, mod_spec],
        out_specs=[pl.BlockSpec((T, D), lambda i: (i, 0)),
                   pl.BlockSpec((T * NP, LANE), lambda i: (i, 0))],
        out_shape=[jax.ShapeDtypeStruct((nrows, D), BF16),
                   jax.ShapeDtypeStruct((nrows * NP, LANE), jnp.uint32)],
        compiler_params=_cp("parallel"),
    )(s, w.reshape(1, D), scale, shift)


def _rms_kernel(s_ref, w_ref, o_ref):
    x = s_ref[...]
    o_ref[...] = x * lax.rsqrt(jnp.mean(x * x, axis=-1, keepdims=True) + RMS_EPS) * w_ref[...]


def _rms(s, w):
    nrows = s.shape[0]
    return pl.pallas_call(
        _rms_kernel,
        grid=(nrows // T,),
        in_specs=[pl.BlockSpec((T, D), lambda i: (i, 0)),
                  pl.BlockSpec((1, D), lambda i: (0, 0))],
        out_specs=pl.BlockSpec((T, D), lambda i: (i, 0)),
        out_shape=jax.ShapeDtypeStruct((nrows, D), F32),
        compiler_params=_cp("parallel"),
    )(s, w.reshape(1, D))


def _mm_kernel(a_ref, w_ref, o_ref):
    o_ref[...] = _dot(a_ref[...], w_ref[...]).astype(o_ref.dtype)


def _matmul(a, w, tn):
    m, k = a.shape
    n = w.shape[1]
    return pl.pallas_call(
        _mm_kernel,
        grid=(m // TM, n // tn),
        in_specs=[pl.BlockSpec((TM, k), lambda i, j: (i, 0)),
                  pl.BlockSpec((k, tn), lambda i, j: (0, j))],
        out_specs=pl.BlockSpec((TM, tn), lambda i, j: (i, j)),
        out_shape=jax.ShapeDtypeStruct((m, n), BF16),
        compiler_params=_cp("parallel", "arbitrary"),
    )(a, w)


def _norm_mm_kernel(a_ref, nw_ref, w_ref, o_ref):
    x = a_ref[...].astype(F32)
    y = x * lax.rsqrt(jnp.mean(x * x, axis=-1, keepdims=True) + RMS_EPS) * nw_ref[...]
    o_ref[...] = _dot(y.astype(BF16), w_ref[...]).astype(o_ref.dtype)


def _norm_mm(u, acol, kdim, nw, w, tn):
    m = u.shape[0]
    n = w.shape[1]
    return pl.pallas_call(
        _norm_mm_kernel,
        grid=(m // TM, n // tn),
        in_specs=[pl.BlockSpec((TM, kdim), lambda i, j: (i, acol)),
                  pl.BlockSpec((1, kdim), lambda i, j: (0, 0)),
                  pl.BlockSpec((kdim, tn), lambda i, j: (0, j))],
        out_specs=pl.BlockSpec((TM, tn), lambda i, j: (i, j)),
        out_shape=jax.ShapeDtypeStruct((m, n), BF16),
        compiler_params=_cp("parallel", "arbitrary"),
    )(u, nw.reshape(1, kdim), w)


def _mm2_res_kernel(a1_ref, a2_ref, w1_ref, w2_ref, r_ref, g_ref, o_ref):
    acc = _dot(a1_ref[...], w1_ref[...]) + _dot(a2_ref[...], w2_ref[...])
    o_ref[...] = r_ref[...] + g_ref[0] * acc


def _mm2_res(a1, a2, w, resid, gate, nrows):
    k1 = a1.shape[1]
    k2 = a2.shape[1]
    assert k1 == k2 and w.shape == (k1 + k2, D)
    tn = 512
    return pl.pallas_call(
        _mm2_res_kernel,
        grid=(nrows // TM, D // tn),
        in_specs=[pl.BlockSpec((TM, k1), lambda i, j: (i, 0)),
                  pl.BlockSpec((TM, k2), lambda i, j: (i, 0)),
                  pl.BlockSpec((k1, tn), lambda i, j: (0, j)),
                  pl.BlockSpec((k2, tn), lambda i, j: (1, j)),
                  pl.BlockSpec((TM, tn), lambda i, j: (i, j)),
                  pl.BlockSpec((1, 1, tn), lambda i, j: (_mod_row(i, TM), 0, j))],
        out_specs=pl.BlockSpec((TM, tn), lambda i, j: (i, j)),
        out_shape=jax.ShapeDtypeStruct((nrows, D), F32),
        compiler_params=_cp("parallel", "arbitrary"),
    )(a1, a2, w, w, resid, gate)


def _split3(x):
    hi = x.astype(BF16)
    r = x - hi.astype(F32)
    mid = r.astype(BF16)
    lo = (r - mid.astype(F32)).astype(BF16)
    return hi, mid, lo


def _hgrn_dir(q_raw, f_raw, v, lb, st_ref, o_ref, rev):
    c = A_CHUNK
    q = _silu(q_raw.astype(F32)) * (A_DK ** -0.5)
    f = lb + (1.0 - lb) * _sigmoid(f_raw.astype(F32))
    k = 1.0 - f
    logf = jnp.log(f)
    row = lax.broadcasted_iota(jnp.int32, (T, T), 0)
    col = lax.broadcasted_iota(jnp.int32, (T, T), 1)
    same = (row ^ col) < c
    tri = jnp.where(same & ((col >= row) if rev else (col <= row)), 1.0, 0.0).astype(BF16)
    hi, mid, lo = _split3(logf)
    cum = _dot(tri, hi) + _dot(tri, mid) + _dot(tri, lo)
    crow = lax.broadcasted_iota(jnp.int32, (c, c), 0)
    ccol = lax.broadcasted_iota(jnp.int32, (c, c), 1)
    tri_c = (ccol >= crow) if rev else (ccol <= crow)
    st = st_ref[...]
    nchunk = T // c
    for n in (range(nchunk - 1, -1, -1) if rev else range(nchunk)):
        sl = slice(n * c, (n + 1) * c)
        cum_c = cum[sl]
        ref = cum_c[c // 2 - 1:c // 2] if rev else cum_c[c // 2:c // 2 + 1]
        last = cum_c[0:1] if rev else cum_c[c - 1:c]
        qc, kc, vc = q[sl], k[sl], v[sl]
        qs = (qc * jnp.exp(cum_c - ref)).astype(BF16)
        ks = (kc * jnp.exp(ref - cum_c)).astype(BF16)
        att = jnp.where(tri_c, _dot_nt(qs, ks), 0.0)
        o = _dot(att.astype(BF16), vc)
        qd = (qc * jnp.exp(cum_c)).astype(BF16)
        kd = (kc * jnp.exp(last - cum_c)).astype(BF16)
        o = o + _dot_nt(qd, st.astype(BF16))
        st = st * jnp.exp(last) + lax.dot_general(vc, kd, (((0,), (0,)), ((), ())),
                                                  preferred_element_type=F32)
        o_ref[sl, :] = o.astype(o_ref.dtype)
    st_ref[...] = st


def _hgrn_kernel(qf, ff, vf, qb, fb, vb, lbf, lbb, of, ob, stf, stb):
    @pl.when(pl.program_id(2) == 0)
    def _():
        stf[...] = jnp.zeros_like(stf)
        stb[...] = jnp.zeros_like(stb)

    _hgrn_dir(qf[...], ff[...], vf[...], lbf[0], stf, of, False)
    _hgrn_dir(qb[...], fb[...], vb[...], lbb[0], stb, ob, True)


def _hgrn(u, lbf, lbb):
    assert LC == T
    xt = L // T
    n_rows = u.shape[0]
    hh = A_HEADS

    def blk_f(b, n):
        return jnp.where(n == 0, B * xt + b, b * xt + n - 1)

    def blk_b(b, n):
        return jnp.where(n == 0, B * xt + b, b * xt + xt - n)

    def spec(blk, cb):
        return pl.BlockSpec((T, LANE), lambda b, h, n: (blk(b, n), cb * hh + h))

    lb_spec = pl.BlockSpec((1, 1, LANE), lambda b, h, n: (h, 0, 0))
    return pl.pallas_call(
        _hgrn_kernel,
        grid=(B, hh, xt + 1),
        in_specs=[spec(blk_f, 0), spec(blk_f, 1), spec(blk_f, 3),
                  spec(blk_b, 0), spec(blk_b, 2), spec(blk_b, 3),
                  lb_spec, lb_spec],
        out_specs=[spec(blk_f, 0), spec(blk_b, 0)],
        out_shape=[jax.ShapeDtypeStruct((n_rows, A_KEY), BF16)] * 2,
        scratch_shapes=[pltpu.VMEM((LANE, LANE), F32)] * 2,
        compiler_params=_cp("parallel", "parallel", "arbitrary"),
    )(u, u, u, u, u, u, lbf, lbb)


def _hgrn_out_kernel(of_ref, ob_ref, g_ref, w_ref, y_ref):
    for h in range(A_HEADS):
        sl = slice(h * LANE, (h + 1) * LANE)
        o = of_ref[:, sl].astype(F32) + ob_ref[:, sl].astype(F32)
        o = o * lax.rsqrt(jnp.mean(o * o, axis=-1, keepdims=True) + RMS_EPS) * w_ref[...]
        y_ref[:, sl] = (o * _silu(g_ref[:, sl].astype(F32))).astype(y_ref.dtype)


def _hgrn_out(o_f, o_b, u, gnorm_w):
    n_rows = o_f.shape[0]
    blk = pl.BlockSpec((T, A_KEY), lambda i: (i, 0))
    return pl.pallas_call(
        _hgrn_out_kernel,
        grid=(n_rows // T,),
        in_specs=[blk, blk,
                  pl.BlockSpec((T, A_KEY), lambda i: (i, 4)),
                  pl.BlockSpec((1, LANE), lambda i: (0, 0))],
        out_specs=blk,
        out_shape=jax.ShapeDtypeStruct((n_rows, A_KEY), BF16),
        compiler_params=_cp("parallel"),
    )(o_f, o_b, u, gnorm_w.reshape(1, LANE))


def _rope_perm(n):
    return np.concatenate([np.arange(0, n, 2), np.arange(1, n, 2)])


def _rope_tables(rot_dim, with_ctx):
    half = rot_dim // 2
    rows = L // GRID_W
    row = jnp.repeat(jnp.arange(rows), GRID_W).astype(F32)
    col = jnp.tile(jnp.arange(GRID_W), rows).astype(F32)
    n_freq = rot_dim // 4
    inv = ROPE_THETA ** (-jnp.arange(n_freq, dtype=F32) / n_freq)
    ang = jnp.concatenate([row[:, None] * inv, col[:, None] * inv], axis=-1)
    cos, sin = jnp.cos(ang), jnp.sin(ang)
    zpad = jnp.zeros((L, LANE - rot_dim), F32)
    zh = jnp.zeros((L, half), F32)
    c = jnp.concatenate([cos, cos, zpad], axis=-1)
    s1 = jnp.concatenate([-sin, zh, zpad], axis=-1)
    s2 = jnp.concatenate([zh, sin, zpad], axis=-1)
    if with_ctx:
        ident = jnp.ones((T, LANE), F32)
        zero = jnp.zeros((T, LANE), F32)
        c = jnp.concatenate([ident, c], axis=0)
        s1 = jnp.concatenate([zero, s1], axis=0)
        s2 = jnp.concatenate([zero, s2], axis=0)
    return c, s1, s2


def _rope_apply(x, c, s1, s2, half):
    return x * c + pltpu.roll(x, LANE - half, 1) * s1 + pltpu.roll(x, half, 1) * s2


def _tab_blk(i):
    xt = L // T
    return jnp.where(i < B * xt, 1 + i % xt, 0)


def _krope_kernel(x_ref, c_ref, s1_ref, s2_ref, o_ref):
    x = x_ref[...].astype(F32)
    o_ref[...] = _rope_apply(x, c_ref[...], s1_ref[...], s2_ref[...], B_ROPE // 2).astype(o_ref.dtype)


def _krope(ub, tabs):
    n_rows = ub.shape[0]
    cb = ub.shape[1] // LANE - 1
    tspec = pl.BlockSpec((T, LANE), lambda i: (_tab_blk(i), 0))
    return pl.pallas_call(
        _krope_kernel,
        grid=(n_rows // T,),
        in_specs=[pl.BlockSpec((T, LANE), lambda i: (i, cb)), tspec, tspec, tspec],
        out_specs=pl.BlockSpec((T, LANE), lambda i: (i, 0)),
        out_shape=jax.ShapeDtypeStruct((n_rows, LANE), BF16),
        compiler_params=_cp("parallel"),
    )(ub, *tabs)


MLA_KV_CHUNK = 512
MLA_TQ = 512


def _softmax_step(q, k, v, m, l, acc):
    s = _dot_nt(q, k)
    m_new = jnp.maximum(m, jnp.max(s, axis=-1, keepdims=True))
    alpha = jnp.exp2(m - m_new)
    p = jnp.exp2(s - m_new)
    l = alpha * l + jnp.sum(p, axis=-1, keepdims=True)
    acc = alpha * acc + _dot(p.astype(BF16), v)
    return m_new, l, acc


def _softmax_init(rows):
    return jnp.full((rows, 1), NEG_BIG, F32), jnp.zeros((rows, 1), F32), jnp.zeros((rows, B_VDIM), F32)


def _mla_attn_kernel(q_ref, c_ref, s1_ref, s2_ref, kxn_ref, vx_ref, krx_ref, kcn_ref, vc_ref, krc_ref,
                     o_ref, kfx, kfc):
    @pl.when(pl.program_id(2) == 0)
    def _():
        kfx[:, :B_NOPE] = kxn_ref[...]
        kfx[:, B_NOPE:] = krx_ref[...]
        kfc[:, :B_NOPE] = kcn_ref[...]
        kfc[:, B_NOPE:] = krc_ref[...]

    qr = _rope_apply(q_ref[:, B_NOPE:].astype(F32), c_ref[...], s1_ref[...], s2_ref[...], B_ROPE // 2)
    q = jnp.concatenate([q_ref[:, :B_NOPE], qr.astype(BF16)], axis=1)
    m, l, acc = _softmax_step(q, kfc[...], vc_ref[...], *_softmax_init(MLA_TQ))
    for n in range(L // MLA_KV_CHUNK):
        sl = slice(n * MLA_KV_CHUNK, (n + 1) * MLA_KV_CHUNK)
        m, l, acc = _softmax_step(q, kfx[sl, :], vx_ref[sl, :], m, l, acc)
    o_ref[...] = (acc / l).astype(o_ref.dtype)


def _mla_ctx_kernel(q_ref, kcn_ref, vc_ref, krc_ref, o_ref):
    k = jnp.concatenate([kcn_ref[...], krc_ref[...]], axis=1)
    m, l, acc = _softmax_step(q_ref[...], k, vc_ref[...], *_softmax_init(LC))
    o_ref[...] = (acc / l).astype(o_ref.dtype)


def _mla_attn(q, kv, kr, tabs):
    xq = L // MLA_TQ
    ctx0 = B * L // LC
    tspec = pl.BlockSpec((MLA_TQ, LANE), lambda b, h, qi: (qi, 0))
    kc_spec = pl.BlockSpec((LC, LANE), lambda b, h, *_: (ctx0 + b, 2 * h))
    vc_spec = pl.BlockSpec((LC, LANE), lambda b, h, *_: (ctx0 + b, 2 * h + 1))
    krc_spec = pl.BlockSpec((LC, LANE), lambda b, h, *_: (ctx0 + b, 0))
    y_x = pl.pallas_call(
        _mla_attn_kernel,
        grid=(B, B_HEADS, xq),
        in_specs=[pl.BlockSpec((MLA_TQ, B_QPAD), lambda b, h, qi: (b * xq + qi, h)),
                  tspec, tspec, tspec,
                  pl.BlockSpec((L, LANE), lambda b, h, qi: (b, 2 * h)),
                  pl.BlockSpec((L, LANE), lambda b, h, qi: (b, 2 * h + 1)),
                  pl.BlockSpec((L, LANE), lambda b, h, qi: (b, 0)),
                  kc_spec, vc_spec, krc_spec],
        out_specs=pl.BlockSpec((MLA_TQ, B_VDIM), lambda b, h, qi: (b * xq + qi, h)),
        out_shape=jax.ShapeDtypeStruct((B * L, B_HEADS * B_VDIM), BF16),
        scratch_shapes=[pltpu.VMEM((L, B_QPAD), BF16), pltpu.VMEM((LC, B_QPAD), BF16)],
        compiler_params=_cp("parallel", "parallel", "arbitrary"),
    )(q, *tabs, kv, kv, kr, kv, kv, kr)
    y_c = pl.pallas_call(
        _mla_ctx_kernel,
        grid=(B, B_HEADS),
        in_specs=[pl.BlockSpec((LC, B_QPAD), lambda b, h: (ctx0 + b, h)), kc_spec, vc_spec, krc_spec],
        out_specs=pl.BlockSpec((LC, B_VDIM), lambda b, h: (b, h)),
        out_shape=jax.ShapeDtypeStruct((B * LC, B_HEADS * B_VDIM), BF16),
        compiler_params=_cp("parallel", "parallel"),
    )(q, kv, kv, kr)
    return jnp.concatenate([y_x, y_c], axis=0)


def _rope_qk_kernel(x_ref, c_ref, s1_ref, s2_ref, o_ref):
    c, s1, s2 = c_ref[...], s1_ref[...], s2_ref[...]
    for h in range(x_ref.shape[1] // LANE):
        sl = slice(h * LANE, (h + 1) * LANE)
        o_ref[:, sl] = _rope_apply(x_ref[:, sl].astype(F32), c, s1, s2, HEAD_DIM // 2).astype(o_ref.dtype)


def _rope_qk(ux, tabs):
    width = C_Q + C_KV
    tn = C_KV
    xt = L // T
    tspec = pl.BlockSpec((T, LANE), lambda i, j: (i % xt, 0))
    return pl.pallas_call(
        _rope_qk_kernel,
        grid=(B * xt, width // tn),
        in_specs=[pl.BlockSpec((T, tn), lambda i, j: (i, j)), tspec, tspec, tspec],
        out_specs=pl.BlockSpec((T, tn), lambda i, j: (i, j)),
        out_shape=jax.ShapeDtypeStruct((B * L, width), BF16),
        compiler_params=_cp("parallel", "parallel"),
    )(ux, *tabs)


def _win_kernel(sink_ref, q_ref, kx_ref, vx_ref, kc_ref, vc_ref, o_ref):
    g = pl.program_id(1)
    t = pl.program_id(2)
    span = 2 * T
    start = pl.multiple_of(jnp.clip(t * T - C_WINDOW, 0, L - span), C_WINDOW)
    kw = kx_ref[pl.ds(start, span), :]
    vw = vx_ref[pl.ds(start, span), :]
    qpos = t * T + lax.broadcasted_iota(jnp.int32, (T, span), 0)
    kpos = start + lax.broadcasted_iota(jnp.int32, (T, span), 1)
    valid = jnp.abs(qpos - kpos) <= C_WINDOW
    kc = kc_ref[...]
    vc = vc_ref[...]
    for r in range(C_GROUP):
        sl = slice(r * HEAD_DIM, (r + 1) * HEAD_DIM)
        q = q_ref[:, sl]
        sw = jnp.where(valid, _dot_nt(q, kw), NEG_BIG)
        sc = _dot_nt(q, kc)
        sk = sink_ref[g * C_GROUP + r]
        m = jnp.maximum(jnp.maximum(jnp.max(sw, axis=-1, keepdims=True),
                                    jnp.max(sc, axis=-1, keepdims=True)), sk)
        pw = jnp.exp(sw - m)
        pc = jnp.exp(sc - m)
        l = jnp.sum(pw, axis=-1, keepdims=True) + jnp.sum(pc, axis=-1, keepdims=True) + jnp.exp(sk - m)
        o = _dot(pw.astype(BF16), vw) + _dot(pc.astype(BF16), vc)
        o_ref[:, sl] = (o / l).astype(o_ref.dtype)


def _win_attn(qk, ux, sink):
    assert T == 2 * C_WINDOW
    xt = L // T
    ctx0 = B * L // LC
    kcol = C_Q // LANE
    vcol = (C_Q + C_KV) // LANE
    gw = C_GROUP * HEAD_DIM
    return pl.pallas_call(
        _win_kernel,
        grid=(B, C_KV_HEADS, xt),
        in_specs=[pl.BlockSpec(memory_space=pltpu.SMEM),
                  pl.BlockSpec((T, gw), lambda b, g, t: (b * xt + t, g)),
                  pl.BlockSpec((L, LANE), lambda b, g, t: (b, kcol + g)),
                  pl.BlockSpec((L, LANE), lambda b, g, t: (b, vcol + g)),
                  pl.BlockSpec((LC, LANE), lambda b, g, t: (ctx0 + b, kcol + g)),
                  pl.BlockSpec((LC, LANE), lambda b, g, t: (ctx0 + b, vcol + g))],
        out_specs=pl.BlockSpec((T, gw), lambda b, g, t: (b * xt + t, g)),
        out_shape=jax.ShapeDtypeStruct((B * L, C_Q), BF16),
        compiler_params=_cp("parallel", "parallel", "arbitrary"),
    )(sink, qk, qk, ux, ux, ux)


CONV_TN = 512
SUBLANE = 8


def _conv_kernel(bg_ref, cg_ref, hh_ref, cgp_ref, hhp_ref, cgn_ref, hhn_ref, w_ref, o_ref):
    xt = L // T
    t = pl.program_id(0) % xt
    z = cg_ref[...].astype(F32) * hh_ref[...].astype(F32)
    zp = cgp_ref[SUBLANE - 1:SUBLANE, :].astype(F32) * hhp_ref[SUBLANE - 1:SUBLANE, :].astype(F32)
    zn = cgn_ref[0:1, :].astype(F32) * hhn_ref[0:1, :].astype(F32)
    zp = jnp.where(t > 0, zp, 0.0)
    zn = jnp.where(t < xt - 1, zn, 0.0)
    rows = lax.broadcasted_iota(jnp.int32, (T, CONV_TN), 0)
    z_prev = jnp.where(rows == 0, zp, pltpu.roll(z, 1, 0))
    z_next = jnp.where(rows == T - 1, zn, pltpu.roll(z, T - 1, 0))
    y = w_ref[0:1, :] * z_prev + w_ref[1:2, :] * z + w_ref[2:3, :] * z_next
    o_ref[...] = (bg_ref[...].astype(F32) * y).astype(o_ref.dtype)


def _conv(ux, conv_w):
    xt = L // T
    nx = B * xt
    c0 = C_IN // CONV_TN
    nc = D_CH // CONV_TN
    rb = T // SUBLANE

    def main(cb):
        return pl.BlockSpec((T, CONV_TN), lambda i, j: (i, c0 + cb * nc + j))

    def prev(cb):
        return pl.BlockSpec((SUBLANE, CONV_TN), lambda i, j: (jnp.maximum(i * rb - 1, 0), c0 + cb * nc + j))

    def nxt(cb):
        return pl.BlockSpec((SUBLANE, CONV_TN),
                            lambda i, j: (jnp.minimum((i + 1) * rb, nx * rb - 1), c0 + cb * nc + j))

    return pl.pallas_call(
        _conv_kernel,
        grid=(nx, nc),
        in_specs=[main(0), main(1), main(2), prev(1), prev(2), nxt(1), nxt(2),
                  pl.BlockSpec((D_CONV, CONV_TN), lambda i, j: (0, j))],
        out_specs=pl.BlockSpec((T, CONV_TN), lambda i, j: (i, j)),
        out_shape=jax.ShapeDtypeStruct((B * L, D_CH), BF16),
        compiler_params=_cp("parallel", "parallel"),
    )(ux, ux, ux, ux, ux, ux, ux, conv_w)


def _router_kernel(h_ref, wr_ref, b_ref, i_ref, g_ref):
    tm = h_ref.shape[0]
    scores = _sigmoid(_dot_nt(wr_ref[...], h_ref[...]))
    biased = scores + b_ref[...]
    ng = N_GROUPS
    sl = [biased[j * ng:(j + 1) * ng] for j in range(GROUP_SIZE)]
    sc = [scores[j * ng:(j + 1) * ng] for j in range(GROUP_SIZE)]
    m1 = sl[0]
    m2 = jnp.full_like(m1, -jnp.inf)
    for v in sl[1:]:
        m2 = jnp.maximum(m2, jnp.minimum(m1, v))
        m1 = jnp.maximum(m1, v)
    gs = m1 + m2
    gidx = lax.broadcasted_iota(jnp.int32, (ng, tm), 0)
    rank = jnp.zeros((ng, tm), jnp.int32)
    for k in range(1, ng):
        r = pltpu.roll(gs, k, 0)
        rank += jnp.where((r > gs) | ((r == gs) & (gidx >= k)), 1, 0)
    keep = rank < TOPK_GROUPS
    ms = [jnp.where(keep, v, -jnp.inf) for v in sl]
    rolled = [[ms[j] if k == 0 else pltpu.roll(ms[j], k, 0) for k in range(ng)] for j in range(GROUP_SIZE)]
    tot = jnp.zeros((ng, tm), F32)
    picked = []
    ranks = []
    for j in range(GROUP_SIZE):
        cnt = jnp.zeros((ng, tm), jnp.int32)
        for jp in range(GROUP_SIZE):
            for k in range(ng):
                r = rolled[jp][k]
                if k == 0:
                    if jp == j:
                        continue
                    beats = (r >= ms[j]) if jp < j else (r > ms[j])
                else:
                    beats = (r > ms[j]) | ((r == ms[j]) & (gidx >= k))
                cnt += jnp.where(beats, 1, 0)
        w = jnp.where(cnt < TOP_K, sc[j], 0.0)
        picked.append(w)
        ranks.append(cnt)
        tot = tot + w
    tot = jnp.sum(tot, axis=0, keepdims=True)
    wn = [p / tot * ROUTE_SCALE for p in picked]
    eid = [(gidx * GROUP_SIZE + j).astype(F32) for j in range(GROUP_SIZE)]
    for k in range(TOP_K):
        ik = jnp.zeros((ng, tm), F32)
        wk = jnp.zeros((ng, tm), F32)
        for j in range(GROUP_SIZE):
            hit = ranks[j] == k
            ik += jnp.where(hit, eid[j], 0.0)
            wk += jnp.where(hit, wn[j], 0.0)
        i_ref[k:k + 1, :] = jnp.sum(ik, axis=0, keepdims=True).astype(jnp.int32)
        g_ref[k:k + 1, :] = jnp.sum(wk, axis=0, keepdims=True)


def _router(h, wr_t, bias, nrows):
    out = pl.BlockSpec((TOP_K, TM), lambda i: (0, i))
    return pl.pallas_call(
        _router_kernel,
        grid=(nrows // TM,),
        in_specs=[pl.BlockSpec((TM, D), lambda i: (i, 0)),
                  pl.BlockSpec((N_EXPERTS, D), lambda i: (0, 0)),
                  pl.BlockSpec((N_EXPERTS, 1), lambda i: (0, 0))],
        out_specs=[out, out],
        out_shape=[jax.ShapeDtypeStruct((TOP_K, nrows), jnp.int32),
                   jax.ShapeDtypeStruct((TOP_K, nrows), F32)],
        compiler_params=_cp("parallel"),
    )(h, wr_t, bias)


def _router_order():
    r = np.arange(N_EXPERTS)
    return (r % N_GROUPS) * GROUP_SIZE + r // N_GROUPS


def _glu_kernel(x_ref, wg_ref, wu_ref, wd_ref, o_ref):
    x = x_ref[...]
    hid = _silu(_dot(x, wg_ref[...])) * _dot(x, wu_ref[...])
    o_ref[...] = _dot(hid.astype(BF16), wd_ref[...]).astype(o_ref.dtype)


def _glu(h, wg, wu, wd, nrows):
    return pl.pallas_call(
        _glu_kernel,
        grid=(nrows // TM,),
        in_specs=[pl.BlockSpec((TM, D), lambda i: (i, 0)),
                  pl.BlockSpec((D, D_EXPERT), lambda i: (0, 0)),
                  pl.BlockSpec((D, D_EXPERT), lambda i: (0, 0)),
                  pl.BlockSpec((D_EXPERT, D), lambda i: (0, 0))],
        out_specs=pl.BlockSpec((TM, D), lambda i: (i, 0)),
        out_shape=jax.ShapeDtypeStruct((nrows, D), BF16),
        compiler_params=_cp("parallel"),
    )(h, wg, wu, wd)


TME = 256
TD = 512
TC = 128


def _route_plan(idx, nrows):
    nt = (nrows * TOP_K) // TME + N_EXPERTS
    onehot = idx[:, :, None] == jnp.arange(N_EXPERTS, dtype=jnp.int32)[None, None, :]
    cs = jnp.cumsum(jnp.any(onehot, axis=1).astype(jnp.int32), axis=0)
    counts = cs[-1]
    padded = ((counts + TME - 1) // TME) * TME
    ends = jnp.cumsum(padded)
    off = ends - padded
    dest = (off[idx] + jnp.take_along_axis(cs, idx, axis=1) - 1).astype(jnp.int32)
    tiles = jnp.arange(nt, dtype=jnp.int32)
    used = ends[-1] // TME
    t_expert = jnp.sum((ends // TME)[None, :] <= tiles[:, None], axis=1)
    t_expert = jnp.minimum(t_expert, N_EXPERTS - 1).astype(jnp.int32)
    t_valid = (tiles < used).astype(jnp.int32)
    t_first = ((tiles * TME == off[t_expert]) & (tiles < used)).astype(jnp.int32)
    t_src = jnp.minimum(tiles, used - 1).astype(jnp.int32)
    return dict(nt=nt, dest=dest, used=used.reshape(1).astype(jnp.int32),
                pad_start=(off + counts).astype(jnp.int32),
                pad_cnt=(padded - counts).astype(jnp.int32),
                t_expert=t_expert, t_valid=t_valid, t_first=t_first, t_src=t_src)


def _slab(ref, row):
    return ref.at[pl.ds(pl.multiple_of(row * NP, NP), NP), :]


def _dispatch_kernel(ps_ref, pc_ref, used_ref, dest_ref, hp_ref, xs_ref, sem, *, nt):
    i = pl.program_id(0)

    def body(j, c):
        for k in range(TOP_K):
            pltpu.make_async_copy(_slab(hp_ref, j), _slab(xs_ref, dest_ref[0, 0, j * TOP_K + k]), sem).start()
        return c

    lax.fori_loop(0, TD, body, 0)
    for k in range(TOP_K):
        pltpu.make_async_copy(hp_ref, xs_ref.at[pl.ds(0, TD * NP), :], sem).wait()

    @pl.when(i == 0)
    def _():
        def pad_copy(e):
            n = pc_ref[e] * NP
            return pltpu.make_async_copy(hp_ref.at[pl.ds(0, n), :],
                                         xs_ref.at[pl.ds(pl.multiple_of(ps_ref[e] * NP, NP), n), :], sem)

        def tail_copy(t):
            n = TME * NP
            return pltpu.make_async_copy(hp_ref.at[pl.ds(0, n), :],
                                         xs_ref.at[pl.ds(pl.multiple_of(t * n, n), n), :], sem)

        def over(lo, hi, copy, pred):
            def start(e, c):
                @pl.when(pred(e))
                def _():
                    copy(e).start()
                return c

            def wait(e, c):
                @pl.when(pred(e))
                def _():
                    copy(e).wait()
                return c

            lax.fori_loop(lo, hi, start, 0)
            lax.fori_loop(lo, hi, wait, 0)

        over(0, N_EXPERTS, pad_copy, lambda e: pc_ref[e] > 0)
        over(0, nt, tail_copy, lambda t: t >= used_ref[0])


def _dispatch(hp, plan, nrows):
    dest = plan["dest"].reshape(nrows // TD, 1, TD * TOP_K)
    return pl.pallas_call(
        functools.partial(_dispatch_kernel, nt=plan["nt"]),
        grid_spec=pltpu.PrefetchScalarGridSpec(
            num_scalar_prefetch=3,
            grid=(nrows // TD,),
            in_specs=[pl.BlockSpec((1, 1, TD * TOP_K), lambda i, ps, pc, us: (i, 0, 0), memory_space=pltpu.SMEM),
                      pl.BlockSpec((TD * NP, LANE), lambda i, ps, pc, us: (i, 0))],
            out_specs=pl.BlockSpec(memory_space=pl.ANY),
            scratch_shapes=[pltpu.SemaphoreType.DMA(())]),
        out_shape=jax.ShapeDtypeStruct((plan["nt"] * TME * NP, LANE), jnp.uint32),
        compiler_params=pltpu.CompilerParams(dimension_semantics=("arbitrary",), disable_bounds_checks=True,
                                             vmem_limit_bytes=VMEM_LIMIT_BYTES),
    )(plan["pad_start"], plan["pad_cnt"], plan["used"], dest, hp)


def _expert_kernel(te_ref, tv_ref, tf_ref, ts_ref, x_ref, wg_ref, wu_ref, wd_ref, o_ref, wgu_b, wd_b):
    i = pl.program_id(0)

    @pl.when(tv_ref[i] == 0)
    def _():
        o_ref[...] = jnp.zeros_like(o_ref)

    @pl.when(tv_ref[i] == 1)
    def _():
        @pl.when(tf_ref[i] == 1)
        def _():
            wgu_b[:, :D_EXPERT] = wg_ref[0].astype(BF16)
            wgu_b[:, D_EXPERT:] = wu_ref[0].astype(BF16)
            wd_b[...] = wd_ref[0].astype(BF16)

        pieces = []
        for p in range(NP):
            lo, hi = _unpack2(x_ref[pl.ds(p, TME, stride=NP), :])
            pieces += [lo.astype(BF16), hi.astype(BF16)]
        a = _dot(jnp.concatenate(pieces, axis=1), wgu_b[...])
        hid = _silu(a[:, :D_EXPERT]) * a[:, D_EXPERT:]
        y = _dot(hid.astype(BF16), wd_b[...])
        for p in range(NP):
            o_ref[pl.ds(p, TME, stride=NP), :] = _pack2(y[:, p * PACK:p * PACK + LANE],
                                                        y[:, p * PACK + LANE:(p + 1) * PACK])


def _experts(xs, plan, wg, wu, wd):
    nt = plan["nt"]
    rows = pl.BlockSpec((TME * NP, LANE), lambda i, te, tv, tf, ts: (i, 0))
    return pl.pallas_call(
        _expert_kernel,
        grid_spec=pltpu.PrefetchScalarGridSpec(
            num_scalar_prefetch=4,
            grid=(nt,),
            in_specs=[pl.BlockSpec((TME * NP, LANE), lambda i, te, tv, tf, ts: (ts[i], 0)),
                      pl.BlockSpec((1, D, D_EXPERT), lambda i, te, tv, tf, ts: (te[i], 0, 0)),
                      pl.BlockSpec((1, D, D_EXPERT), lambda i, te, tv, tf, ts: (te[i], 0, 0)),
                      pl.BlockSpec((1, D_EXPERT, D), lambda i, te, tv, tf, ts: (te[i], 0, 0))],
            out_specs=rows,
            scratch_shapes=[pltpu.VMEM((D, 2 * D_EXPERT), BF16), pltpu.VMEM((D_EXPERT, D), BF16)]),
        out_shape=jax.ShapeDtypeStruct((nt * TME * NP, LANE), jnp.uint32),
        compiler_params=pltpu.CompilerParams(dimension_semantics=("arbitrary",),
                                             vmem_limit_bytes=EXPERT_VMEM_LIMIT_BYTES),
    )(plan["t_expert"], plan["t_valid"], plan["t_first"], plan["t_src"], xs, wg, wu, wd)


def _combine_kernel(dcur_ref, dnxt_ref, w_ref, r_ref, sh_ref, g_ref, ys_ref, o_ref, buf0, buf1, sem):
    i = pl.program_id(0)
    last = pl.num_programs(0) - 1

    def issue(d_ref, buf, slot):
        def body(j, c):
            for k in range(TOP_K):
                pltpu.make_async_copy(_slab(ys_ref, d_ref[0, 0, j * TOP_K + k]), _slab(buf, k * TC + j),
                                      sem.at[slot]).start()
            return c

        lax.fori_loop(0, TC, body, 0)

    def drain(buf, slot):
        pltpu.make_async_copy(ys_ref.at[pl.ds(0, TC * TOP_K * NP), :], buf, sem.at[slot]).wait()

    def reduce(buf):
        gates = [w_ref[:, k:k + 1] for k in range(TOP_K)]
        gmod = g_ref[0]
        for p in range(NP):
            lo_sl = slice(p * PACK, p * PACK + LANE)
            hi_sl = slice(p * PACK + LANE, (p + 1) * PACK)
            lo_acc = sh_ref[:, lo_sl].astype(F32)
            hi_acc = sh_ref[:, hi_sl].astype(F32)
            for k in range(TOP_K):
                lo, hi = _unpack2(buf[pl.ds(k * TC * NP + p, TC, stride=NP), :])
                lo_acc += gates[k] * lo
                hi_acc += gates[k] * hi
            o_ref[:, lo_sl] = r_ref[:, lo_sl] + gmod[:, lo_sl] * lo_acc
            o_ref[:, hi_sl] = r_ref[:, hi_sl] + gmod[:, hi_sl] * hi_acc

    @pl.when(i == 0)
    def _():
        issue(dcur_ref, buf0, 0)

    @pl.when(i % 2 == 0)
    def _():
        @pl.when(i < last)
        def _():
            issue(dnxt_ref, buf1, 1)

        drain(buf0, 0)
        reduce(buf0)

    @pl.when(i % 2 == 1)
    def _():
        @pl.when(i < last)
        def _():
            issue(dnxt_ref, buf0, 0)

        drain(buf1, 1)
        reduce(buf1)


def _combine(resid, ys, shared, w, plan, gate, nrows):
    nsteps = nrows // TC
    dest = plan["dest"].reshape(nsteps, 1, TC * TOP_K)
    blk = pl.BlockSpec((TC, D), lambda i: (i, 0))
    return pl.pallas_call(
        _combine_kernel,
        grid=(nsteps,),
        in_specs=[pl.BlockSpec((1, 1, TC * TOP_K), lambda i: (i, 0, 0), memory_space=pltpu.SMEM),
                  pl.BlockSpec((1, 1, TC * TOP_K), lambda i: (jnp.minimum(i + 1, nsteps - 1), 0, 0),
                               memory_space=pltpu.SMEM),
                  pl.BlockSpec((TC, TOP_K), lambda i: (i, 0)),
                  blk, blk,
                  pl.BlockSpec((1, 1, D), lambda i: (_mod_row(i, TC), 0, 0)),
                  pl.BlockSpec(memory_space=pl.ANY)],
        out_specs=blk,
        out_shape=jax.ShapeDtypeStruct((nrows, D), F32),
        scratch_shapes=[pltpu.VMEM((TC * TOP_K * NP, LANE), jnp.uint32),
                        pltpu.VMEM((TC * TOP_K * NP, LANE), jnp.uint32),
                        pltpu.SemaphoreType.DMA((2,))],
        compiler_params=pltpu.CompilerParams(dimension_semantics=("arbitrary",),
                                             vmem_limit_bytes=VMEM_LIMIT_BYTES, disable_bounds_checks=True),
    )(dest, dest, w, resid, shared, gate, ys)


def _moe(s, h, hp, gate, nrows, router_w, router_bias, wg, wu, wd, sg, su, sd):
    order = _router_order()
    wr_t = router_w.T[order].astype(BF16)
    bias = router_bias[order].reshape(N_EXPERTS, 1).astype(F32)
    idx_t, w_t = _router(h, wr_t, bias, nrows)
    plan = _route_plan(idx_t.T, nrows)
    xs = _dispatch(hp, plan, nrows)
    ys = _experts(xs, plan, wg, wu, wd)
    shared = _glu(h, sg.astype(BF16), su.astype(BF16), sd.astype(BF16), nrows)
    return _combine(s, ys, shared, w_t.T, plan, gate, nrows)


def _mla_in_weights(w_in):
    wb = w_in[:, A_IN:]
    kr = wb[:, B_Q_RANK + B_KV_RANK:][:, _rope_perm(B_ROPE)]
    pad = jnp.zeros((D, LANE - B_ROPE), w_in.dtype)
    return jnp.concatenate([wb[:, :B_Q_RANK + B_KV_RANK], kr, pad], axis=1).astype(BF16)


def _mla_uq_weights(w_uq):
    scale = (B_NOPE + B_ROPE) ** -0.5 * np.log2(np.e)
    w = (w_uq * scale).reshape(B_Q_RANK, B_HEADS, B_NOPE + B_ROPE)
    rope = w[:, :, B_NOPE:][:, :, _rope_perm(B_ROPE)]
    pad = jnp.zeros((B_Q_RANK, B_HEADS, B_QPAD - B_NOPE - B_ROPE), w.dtype)
    return jnp.concatenate([w[:, :, :B_NOPE], rope, pad], axis=-1).reshape(B_Q_RANK, B_HEADS * B_QPAD).astype(BF16)


def _cd_in_weights(w_in):
    perm = _rope_perm(HEAD_DIM)
    q = (w_in[:, :C_Q] * HEAD_DIM ** -0.5).reshape(D, C_HEADS, HEAD_DIM)[:, :, perm].reshape(D, C_Q)
    k = w_in[:, C_Q:C_Q + C_KV].reshape(D, C_KV_HEADS, HEAD_DIM)[:, :, perm].reshape(D, C_KV)
    return jnp.concatenate([q, k, w_in[:, C_Q + C_KV:]], axis=1).astype(BF16)


def kernel(x, c, ctx, c_ctx, ada_w, ada_b, norm_mix_w, norm_ffn_w, ab_w_in, ab_w_out, hgrn_lb_logits, hgrn_gnorm_w, mla_q_norm_w, mla_w_uq, mla_kv_norm_w, mla_w_ukv, cd_w_in, cd_w_out, gqa_sink, conv_w, router_w, router_bias, exp_w_gate, exp_w_up, exp_w_down, sh_w_gate, sh_w_up, sh_w_down, final_norm_w):
    nx = B * L
    n = nx + B * LC
    s = jnp.concatenate([x.reshape(nx, D), ctx.reshape(B * LC, D)], axis=0)
    lb_all = jnp.cumsum(jax.nn.softmax(hgrn_lb_logits.astype(F32), axis=0), axis=0)
    cond = jnp.concatenate([c, c_ctx[None, :], jnp.zeros((8 - B - 1, D), F32)], axis=0)

    def modulation(i):
        m = _ada(cond, ada_w[i], ada_b[i].reshape(1, N_MOD * D)).reshape(8, N_MOD, D)
        return [m[:B + 1, k, :].reshape(B + 1, 1, D) for k in range(N_MOD)]

    def moe_args(i):
        return (router_w[i], router_bias[i], exp_w_gate[i], exp_w_up[i], exp_w_down[i],
                sh_w_gate[i], sh_w_up[i], sh_w_down[i])

    mod = modulation(0)
    h = _norm_mod(s, norm_mix_w[0], mod[1], mod[0], n)
    ua = _matmul(h, ab_w_in[0][:, :A_IN].astype(BF16), 1024)
    ub = _matmul(h, _mla_in_weights(ab_w_in[0]), B_Q_RANK + B_KV_RANK + LANE)
    lb = lb_all[0].reshape(2, A_HEADS, 1, A_DK)
    o_f, o_b = _hgrn(ua, lb[0], lb[1])
    ya = _hgrn_out(o_f, o_b, ua, hgrn_gnorm_w[0])
    tabs = _rope_tables(B_ROPE, True)
    q = _norm_mm(ub, 0, B_Q_RANK, mla_q_norm_w[0], _mla_uq_weights(mla_w_uq[0]), 1024)
    kv = _norm_mm(ub, B_Q_RANK // B_KV_RANK, B_KV_RANK, mla_kv_norm_w[0], mla_w_ukv[0].astype(BF16), 1024)
    kr = _krope(ub, tabs)
    yb = _mla_attn(q, kv, kr, tuple(t[T:] for t in tabs))
    s = _mm2_res(ya, yb, ab_w_out[0].astype(BF16), s, mod[2], n)
    h, hp = _norm_mod_pack(s, norm_ffn_w[0], mod[4], mod[3], n)
    s = _moe(s, h, hp, mod[5], n, *moe_args(0))

    mod = modulation(1)
    h = _norm_mod(s, norm_mix_w[1], mod[1], mod[0], n)
    ux = _matmul(h, _cd_in_weights(cd_w_in[0]), 1024)
    qk = _rope_qk(ux, _rope_tables(HEAD_DIM, False))
    att = _win_attn(qk, ux, gqa_sink[0].astype(F32))
    cv = _conv(ux, conv_w[0])
    s = _mm2_res(att, cv, cd_w_out[0].astype(BF16), s, mod[2], nx)
    h, hp = _norm_mod_pack(s, norm_ffn_w[1], mod[4], mod[3], nx)
    s = _moe(s, h, hp, mod[5], nx, *moe_args(1))
    return _rms(s, final_norm_w).reshape(B, L, D)
```

```python
import functools

import jax
import jax.numpy as jnp
import numpy as np
from jax import lax
from jax.experimental import pallas as pl
from jax.experimental.pallas import tpu as pltpu

F32 = jnp.float32
BF16 = jnp.bfloat16

D = 4096
B = 2
L = 8192
LC = 256
GRID_W = 64
RMS_EPS = 1e-6
ROPE_THETA = 10000.0
N_MOD = 6

A_HEADS = 16
A_DK = 128
A_KEY = A_HEADS * A_DK
A_CHUNK = 32
A_IN = 5 * A_KEY

B_HEADS = 16
B_Q_RANK = 1024
B_KV_RANK = 512
B_NOPE = 128
B_ROPE = 64
B_VDIM = 128
B_QPAD = 256

C_HEADS = 16
C_KV_HEADS = 4
C_GROUP = C_HEADS // C_KV_HEADS
C_WINDOW = 128
HEAD_DIM = 128
C_Q = C_HEADS * HEAD_DIM
C_KV = C_KV_HEADS * HEAD_DIM
C_IN = C_Q + 2 * C_KV
D_CH = 2048
D_CONV = 3

N_EXPERTS = 64
TOP_K = 8
N_GROUPS = 8
TOPK_GROUPS = 4
GROUP_SIZE = N_EXPERTS // N_GROUPS
D_EXPERT = 256
ROUTE_SCALE = 2.5

T = 256
TM = 512
LANE = 128
VMEM_LIMIT_BYTES = 48 * 1024 * 1024
EXPERT_VMEM_LIMIT_BYTES = 56 * 1024 * 1024
NEG_BIG = -1e30
PACK = 2 * LANE
NP = D // PACK


def _cp(*sem):
    return pltpu.CompilerParams(dimension_semantics=sem, vmem_limit_bytes=VMEM_LIMIT_BYTES)


def _sigmoid(x):
    return 1.0 / (1.0 + jnp.exp(-x))


def _silu(x):
    return x * _sigmoid(x)


def _dot(a, b):
    return jnp.dot(a, b, preferred_element_type=F32)


def _dot_nt(a, b):
    return lax.dot_general(a, b, (((1,), (1,)), ((), ())), preferred_element_type=F32)


def _mod_row(i, tile):
    return jnp.where(i < (B * L) // tile, i // (L // tile), B)


def _ada_kernel(a_ref, w_ref, b_ref, o_ref):
    a = _silu(a_ref[...]).astype(BF16)
    o_ref[...] = _dot(a, w_ref[...].astype(BF16)) + b_ref[...]


def _ada(a, w, bias):
    n = w.shape[1]
    tn = 512
    return pl.pallas_call(
        _ada_kernel,
        grid=(n // tn,),
        in_specs=[pl.BlockSpec((8, D), lambda j: (0, 0)),
                  pl.BlockSpec((D, tn), lambda j: (0, j)),
                  pl.BlockSpec((1, tn), lambda j: (0, j))],
        out_specs=pl.BlockSpec((8, tn), lambda j: (0, j)),
        out_shape=jax.ShapeDtypeStruct((8, n), F32),
        compiler_params=_cp("parallel"),
    )(a, w, bias)


def _norm_mod_kernel(s_ref, w_ref, sc_ref, sh_ref, o_ref):
    x = s_ref[...]
    y = x * lax.rsqrt(jnp.mean(x * x, axis=-1, keepdims=True) + RMS_EPS) * w_ref[...]
    o_ref[...] = (y * (1.0 + sc_ref[0]) + sh_ref[0]).astype(o_ref.dtype)


def _norm_mod(s, w, scale, shift, nrows):
    mod_spec = pl.BlockSpec((1, 1, D), lambda i: (_mod_row(i, T), 0, 0))
    return pl.pallas_call(
        _norm_mod_kernel,
        grid=(nrows // T,),
        in_specs=[pl.BlockSpec((T, D), lambda i: (i, 0)),
                  pl.BlockSpec((1, D), lambda i: (0, 0)),
                  mod_spec, mod_spec],
        out_specs=pl.BlockSpec((T, D), lambda i: (i, 0)),
        out_shape=jax.ShapeDtypeStruct((nrows, D), BF16),
        compiler_params=_cp("parallel"),
    )(s, w.reshape(1, D), scale, shift)


def _pack2(lo, hi):
    lo_b = lax.bitcast_convert_type(lo.astype(BF16).astype(F32), jnp.uint32) >> 16
    hi_b = lax.bitcast_convert_type(hi.astype(BF16).astype(F32), jnp.uint32)
    return hi_b | lo_b


def _unpack2(w):
    lo = lax.bitcast_convert_type(w << 16, F32)
    hi = lax.bitcast_convert_type(w & jnp.uint32(0xFFFF0000), F32)
    return lo, hi


def _norm_mod_pack_kernel(s_ref, w_ref, sc_ref, sh_ref, o_ref, p_ref):
    x = s_ref[...]
    y = x * lax.rsqrt(jnp.mean(x * x, axis=-1, keepdims=True) + RMS_EPS) * w_ref[...]
    y = y * (1.0 + sc_ref[0]) + sh_ref[0]
    o_ref[...] = y.astype(o_ref.dtype)
    for p in range(NP):
        p_ref[pl.ds(p, T, stride=NP), :] = _pack2(y[:, p * PACK:p * PACK + LANE], y[:, p * PACK + LANE:(p + 1) * PACK])


def _norm_mod_pack(s, w, scale, shift, nrows):
    mod_spec = pl.BlockSpec((1, 1, D), lambda i: (_mod_row(i, T), 0, 0))
    return pl.pallas_call(
        _norm_mod_pack_kernel,
        grid=(nrows // T,),
        in_specs=[pl.BlockSpec((T, D), lambda i: (i, 0)),
                  pl.BlockSpec((1, D), lambda i: (0, 0)),
                  mod_spec, mod_spec],
        out_specs=[pl.BlockSpec((T, D), lambda i: (i, 0)),
                   pl.BlockSpec((T * NP, LANE), lambda i: (i, 0))],
        out_shape=[jax.ShapeDtypeStruct((nrows, D), BF16),
                   jax.ShapeDtypeStruct((nrows * NP, LANE), jnp.uint32)],
        compiler_params=_cp("parallel"),
    )(s, w.reshape(1, D), scale, shift)


def _rms_kernel(s_ref, w_ref, o_ref):
    x = s_ref[...]
    o_ref[...] = x * lax.rsqrt(jnp.mean(x * x, axis=-1, keepdims=True) + RMS_EPS) * w_ref[...]


def _rms(s, w):
    nrows = s.shape[0]
    return pl.pallas_call(
        _rms_kernel,
        grid=(nrows // T,),
        in_specs=[pl.BlockSpec((T, D), lambda i: (i, 0)),
                  pl.BlockSpec((1, D), lambda i: (0, 0))],
        out_specs=pl.BlockSpec((T, D), lambda i: (i, 0)),
        out_shape=jax.ShapeDtypeStruct((nrows, D), F32),
        compiler_params=_cp("parallel"),
    )(s, w.reshape(1, D))


def _mm_kernel(a_ref, w_ref, o_ref):
    o_ref[...] = _dot(a_ref[...], w_ref[...]).astype(o_ref.dtype)


def _matmul(a, w, tn):
    m, k = a.shape
    n = w.shape[1]
    return pl.pallas_call(
        _mm_kernel,
        grid=(m // TM, n // tn),
        in_specs=[pl.BlockSpec((TM, k), lambda i, j: (i, 0)),
                  pl.BlockSpec((k, tn), lambda i, j: (0, j))],
        out_specs=pl.BlockSpec((TM, tn), lambda i, j: (i, j)),
        out_shape=jax.ShapeDtypeStruct((m, n), BF16),
        compiler_params=_cp("parallel", "arbitrary"),
    )(a, w)


def _mm_wres_kernel(a_ref, w_ref, o_ref, wb):
    @pl.when(pl.program_id(1) == 0)
    def _():
        wb[...] = w_ref[...].astype(BF16)

    o_ref[...] = _dot(a_ref[...], wb[...]).astype(o_ref.dtype)


def _matmul_wres(a, w, ncols):
    m, k = a.shape
    tn = 512
    return pl.pallas_call(
        _mm_wres_kernel,
        grid=(ncols // tn, m // TM),
        in_specs=[pl.BlockSpec((TM, k), lambda j, i: (i, 0)),
                  pl.BlockSpec((k, tn), lambda j, i: (0, j))],
        out_specs=pl.BlockSpec((TM, tn), lambda j, i: (i, j)),
        out_shape=jax.ShapeDtypeStruct((m, ncols), BF16),
        scratch_shapes=[pltpu.VMEM((k, tn), BF16)],
        compiler_params=_cp("parallel", "arbitrary"),
    )(a, w)


def _norm_mm_kernel(a_ref, nw_ref, w_ref, o_ref):
    x = a_ref[...].astype(F32)
    y = x * lax.rsqrt(jnp.mean(x * x, axis=-1, keepdims=True) + RMS_EPS) * nw_ref[...]
    o_ref[...] = _dot(y.astype(BF16), w_ref[...]).astype(o_ref.dtype)


def _norm_mm(u, acol, kdim, nw, w, tn):
    m = u.shape[0]
    n = w.shape[1]
    return pl.pallas_call(
        _norm_mm_kernel,
        grid=(m // TM, n // tn),
        in_specs=[pl.BlockSpec((TM, kdim), lambda i, j: (i, acol)),
                  pl.BlockSpec((1, kdim), lambda i, j: (0, 0)),
                  pl.BlockSpec((kdim, tn), lambda i, j: (0, j))],
        out_specs=pl.BlockSpec((TM, tn), lambda i, j: (i, j)),
        out_shape=jax.ShapeDtypeStruct((m, n), BF16),
        compiler_params=_cp("parallel", "arbitrary"),
    )(u, nw.reshape(1, kdim), w)


def _mm2_res_kernel(a1_ref, a2_ref, w1_ref, w2_ref, r_ref, g_ref, o_ref):
    acc = _dot(a1_ref[...], w1_ref[...]) + _dot(a2_ref[...], w2_ref[...])
    o_ref[...] = r_ref[...] + g_ref[0] * acc


def _mm2_res(a1, a2, w, resid, gate, nrows):
    k1 = a1.shape[1]
    k2 = a2.shape[1]
    assert k1 == k2 and w.shape == (k1 + k2, D)
    tn = 512
    return pl.pallas_call(
        _mm2_res_kernel,
        grid=(nrows // TM, D // tn),
        in_specs=[pl.BlockSpec((TM, k1), lambda i, j: (i, 0)),
                  pl.BlockSpec((TM, k2), lambda i, j: (i, 0)),
                  pl.BlockSpec((k1, tn), lambda i, j: (0, j)),
                  pl.BlockSpec((k2, tn), lambda i, j: (1, j)),
                  pl.BlockSpec((TM, tn), lambda i, j: (i, j)),
                  pl.BlockSpec((1, 1, tn), lambda i, j: (_mod_row(i, TM), 0, j))],
        out_specs=pl.BlockSpec((TM, tn), lambda i, j: (i, j)),
        out_shape=jax.ShapeDtypeStruct((nrows, D), F32),
        compiler_params=_cp("parallel", "arbitrary"),
    )(a1, a2, w, w, resid, gate)


def _split3(x):
    hi = x.astype(BF16)
    r = x - hi.astype(F32)
    mid = r.astype(BF16)
    lo = (r - mid.astype(F32)).astype(BF16)
    return hi, mid, lo


def _hgrn_dir(q_raw, f_raw, v, lb, st_ref, o_ref, rev):
    c = A_CHUNK
    q = _silu(q_raw.astype(F32)) * (A_DK ** -0.5)
    f = lb + (1.0 - lb) * _sigmoid(f_raw.astype(F32))
    k = 1.0 - f
    logf = jnp.log(f)
    row = lax.broadcasted_iota(jnp.int32, (T, T), 0)
    col = lax.broadcasted_iota(jnp.int32, (T, T), 1)
    mask = ((row ^ col) < c) & ((col >= row) if rev else (col <= row))
    tri = jnp.where(mask, 1.0, 0.0).astype(BF16)
    hi, mid, lo = _split3(logf)
    cum = _dot(tri, hi) + _dot(tri, mid) + _dot(tri, lo)
    nchunk = T // c
    cum3 = cum.reshape(nchunk, c, LANE)
    ref = cum3[:, c // 2 - 1:c // 2] if rev else cum3[:, c // 2:c // 2 + 1]
    last = cum3[:, 0:1] if rev else cum3[:, c - 1:c]
    q3 = q.reshape(nchunk, c, LANE)
    k3 = k.reshape(nchunk, c, LANE)
    qs = (q3 * jnp.exp(cum3 - ref)).reshape(T, LANE).astype(BF16)
    ks = (k3 * jnp.exp(ref - cum3)).reshape(T, LANE).astype(BF16)
    qd = (q * jnp.exp(cum)).astype(BF16)
    kd = (k3 * jnp.exp(last - cum3)).reshape(T, LANE).astype(BF16)
    decay = jnp.exp(last)
    att = jnp.where(mask, _dot_nt(qs, ks), 0.0)
    o = _dot(att.astype(BF16), v)
    kv = [lax.dot_general(v[n * c:(n + 1) * c], kd[n * c:(n + 1) * c], (((0,), (0,)), ((), ())),
                          preferred_element_type=F32) for n in range(nchunk)]
    st = st_ref[...]
    o_inter = [None] * nchunk
    for n in (range(nchunk - 1, -1, -1) if rev else range(nchunk)):
        o_inter[n] = _dot_nt(qd[n * c:(n + 1) * c], st.astype(BF16))
        st = st * decay[n] + kv[n]
    st_ref[...] = st
    o_ref[...] = (o + jnp.concatenate(o_inter, axis=0)).astype(o_ref.dtype)


def _hgrn_kernel(qf, ff, vf, qb, fb, vb, lbf, lbb, of, ob, stf, stb):
    @pl.when(pl.program_id(2) == 0)
    def _():
        stf[...] = jnp.zeros_like(stf)
        stb[...] = jnp.zeros_like(stb)

    _hgrn_dir(qf[...], ff[...], vf[...], lbf[0], stf, of, False)
    _hgrn_dir(qb[...], fb[...], vb[...], lbb[0], stb, ob, True)


def _hgrn(u, lbf, lbb):
    assert LC == T
    xt = L // T
    n_rows = u.shape[0]
    hh = A_HEADS

    def blk_f(b, n):
        return jnp.where(n == 0, B * xt + b, b * xt + n - 1)

    def blk_b(b, n):
        return jnp.where(n == 0, B * xt + b, b * xt + xt - n)

    def spec(blk, cb):
        return pl.BlockSpec((T, LANE), lambda b, h, n: (blk(b, n), cb * hh + h))

    lb_spec = pl.BlockSpec((1, 1, LANE), lambda b, h, n: (h, 0, 0))
    return pl.pallas_call(
        _hgrn_kernel,
        grid=(B, hh, xt + 1),
        in_specs=[spec(blk_f, 0), spec(blk_f, 1), spec(blk_f, 3),
                  spec(blk_b, 0), spec(blk_b, 2), spec(blk_b, 3),
                  lb_spec, lb_spec],
        out_specs=[spec(blk_f, 0), spec(blk_b, 0)],
        out_shape=[jax.ShapeDtypeStruct((n_rows, A_KEY), BF16)] * 2,
        scratch_shapes=[pltpu.VMEM((LANE, LANE), F32)] * 2,
        compiler_params=_cp("parallel", "parallel", "arbitrary"),
    )(u, u, u, u, u, u, lbf, lbb)


def _hgrn_out_kernel(of_ref, ob_ref, g_ref, w_ref, y_ref):
    for h in range(A_HEADS):
        sl = slice(h * LANE, (h + 1) * LANE)
        o = of_ref[:, sl].astype(F32) + ob_ref[:, sl].astype(F32)
        o = o * lax.rsqrt(jnp.mean(o * o, axis=-1, keepdims=True) + RMS_EPS) * w_ref[...]
        y_ref[:, sl] = (o * _silu(g_ref[:, sl].astype(F32))).astype(y_ref.dtype)


def _hgrn_out(o_f, o_b, u, gnorm_w):
    n_rows = o_f.shape[0]
    blk = pl.BlockSpec((T, A_KEY), lambda i: (i, 0))
    return pl.pallas_call(
        _hgrn_out_kernel,
        grid=(n_rows // T,),
        in_specs=[blk, blk,
                  pl.BlockSpec((T, A_KEY), lambda i: (i, 4)),
                  pl.BlockSpec((1, LANE), lambda i: (0, 0))],
        out_specs=blk,
        out_shape=jax.ShapeDtypeStruct((n_rows, A_KEY), BF16),
        compiler_params=_cp("parallel"),
    )(o_f, o_b, u, gnorm_w.reshape(1, LANE))


def _rope_perm(n):
    return np.concatenate([np.arange(0, n, 2), np.arange(1, n, 2)])


def _rope_cos_sin(rot_dim):
    rows = L // GRID_W
    row = jnp.repeat(jnp.arange(rows), GRID_W).astype(F32)
    col = jnp.tile(jnp.arange(GRID_W), rows).astype(F32)
    n_freq = rot_dim // 4
    inv = ROPE_THETA ** (-jnp.arange(n_freq, dtype=F32) / n_freq)
    ang = jnp.concatenate([row[:, None] * inv, col[:, None] * inv], axis=-1)
    return jnp.cos(ang), jnp.sin(ang)


def _rope_tables_pairs(scale):
    cos, sin = _rope_cos_sin(LANE)
    zero = jnp.zeros_like(sin)
    c = jnp.repeat(cos, 2, axis=-1) * scale
    s1 = jnp.stack([-sin, zero], axis=-1).reshape(L, LANE) * scale
    s2 = jnp.stack([zero, sin], axis=-1).reshape(L, LANE) * scale
    return c, s1, s2


def _rope_tables(rot_dim, with_ctx):
    half = rot_dim // 2
    cos, sin = _rope_cos_sin(rot_dim)
    zpad = jnp.zeros((L, LANE - rot_dim), F32)
    zh = jnp.zeros((L, half), F32)
    c = jnp.concatenate([cos, cos, zpad], axis=-1)
    s1 = jnp.concatenate([-sin, zh, zpad], axis=-1)
    s2 = jnp.concatenate([zh, sin, zpad], axis=-1)
    if with_ctx:
        ident = jnp.ones((T, LANE), F32)
        zero = jnp.zeros((T, LANE), F32)
        c = jnp.concatenate([ident, c], axis=0)
        s1 = jnp.concatenate([zero, s1], axis=0)
        s2 = jnp.concatenate([zero, s2], axis=0)
    return c, s1, s2


def _rope_apply(x, c, s1, s2, half):
    return x * c + pltpu.roll(x, LANE - half, 1) * s1 + pltpu.roll(x, half, 1) * s2


def _tab_blk(i):
    xt = L // T
    return jnp.where(i < B * xt, 1 + i % xt, 0)


def _krope_kernel(x_ref, c_ref, s1_ref, s2_ref, o_ref):
    x = x_ref[...].astype(F32)
    o_ref[...] = _rope_apply(x, c_ref[...], s1_ref[...], s2_ref[...], B_ROPE // 2).astype(o_ref.dtype)


def _krope(ub, tabs):
    n_rows = ub.shape[0]
    cb = ub.shape[1] // LANE - 1
    tspec = pl.BlockSpec((T, LANE), lambda i: (_tab_blk(i), 0))
    return pl.pallas_call(
        _krope_kernel,
        grid=(n_rows // T,),
        in_specs=[pl.BlockSpec((T, LANE), lambda i: (i, cb)), tspec, tspec, tspec],
        out_specs=pl.BlockSpec((T, LANE), lambda i: (i, 0)),
        out_shape=jax.ShapeDtypeStruct((n_rows, LANE), BF16),
        compiler_params=_cp("parallel"),
    )(ub, *tabs)


MLA_KV_CHUNK = 512
MLA_TQ = 512


def _softmax_step(q, k, v, m, l, acc):
    s = _dot_nt(q, k)
    m_new = jnp.maximum(m, jnp.max(s, axis=-1, keepdims=True))
    alpha = jnp.exp2(m - m_new)
    p = jnp.exp2(s - m_new)
    l = alpha * l + jnp.sum(p, axis=-1, keepdims=True)
    acc = alpha * acc + _dot(p.astype(BF16), v)
    return m_new, l, acc


def _softmax_init(rows):
    return jnp.full((rows, 1), NEG_BIG, F32), jnp.zeros((rows, 1), F32), jnp.zeros((rows, B_VDIM), F32)


def _mla_attn_kernel(q_ref, c_ref, s1_ref, s2_ref, kxn_ref, vx_ref, krx_ref, kcn_ref, vc_ref, krc_ref,
                     o_ref, kfx, kfc):
    @pl.when(pl.program_id(2) == 0)
    def _():
        kfx[:, :B_NOPE] = kxn_ref[...]
        kfx[:, B_NOPE:] = krx_ref[...]
        kfc[:, :B_NOPE] = kcn_ref[...]
        kfc[:, B_NOPE:] = krc_ref[...]

    qr = _rope_apply(q_ref[:, B_NOPE:].astype(F32), c_ref[...], s1_ref[...], s2_ref[...], B_ROPE // 2)
    q = jnp.concatenate([q_ref[:, :B_NOPE], qr.astype(BF16)], axis=1)
    m, l, acc = _softmax_step(q, kfc[...], vc_ref[...], *_softmax_init(MLA_TQ))
    for n in range(L // MLA_KV_CHUNK):
        sl = slice(n * MLA_KV_CHUNK, (n + 1) * MLA_KV_CHUNK)
        m, l, acc = _softmax_step(q, kfx[sl, :], vx_ref[sl, :], m, l, acc)
    o_ref[...] = (acc / l).astype(o_ref.dtype)


def _mla_ctx_kernel(q_ref, kcn_ref, vc_ref, krc_ref, o_ref):
    k = jnp.concatenate([kcn_ref[...], krc_ref[...]], axis=1)
    m, l, acc = _softmax_step(q_ref[...], k, vc_ref[...], *_softmax_init(LC))
    o_ref[...] = (acc / l).astype(o_ref.dtype)


def _mla_attn(q, kv, kr, tabs):
    xq = L // MLA_TQ
    ctx0 = B * L // LC
    tspec = pl.BlockSpec((MLA_TQ, LANE), lambda b, h, qi: (qi, 0))
    kc_spec = pl.BlockSpec((LC, LANE), lambda b, h, *_: (ctx0 + b, 2 * h))
    vc_spec = pl.BlockSpec((LC, LANE), lambda b, h, *_: (ctx0 + b, 2 * h + 1))
    krc_spec = pl.BlockSpec((LC, LANE), lambda b, h, *_: (ctx0 + b, 0))
    y_x = pl.pallas_call(
        _mla_attn_kernel,
        grid=(B, B_HEADS, xq),
        in_specs=[pl.BlockSpec((MLA_TQ, B_QPAD), lambda b, h, qi: (b * xq + qi, h)),
                  tspec, tspec, tspec,
                  pl.BlockSpec((L, LANE), lambda b, h, qi: (b, 2 * h)),
                  pl.BlockSpec((L, LANE), lambda b, h, qi: (b, 2 * h + 1)),
                  pl.BlockSpec((L, LANE), lambda b, h, qi: (b, 0)),
                  kc_spec, vc_spec, krc_spec],
        out_specs=pl.BlockSpec((MLA_TQ, B_VDIM), lambda b, h, qi: (b * xq + qi, h)),
        out_shape=jax.ShapeDtypeStruct((B * L, B_HEADS * B_VDIM), BF16),
        scratch_shapes=[pltpu.VMEM((L, B_QPAD), BF16), pltpu.VMEM((LC, B_QPAD), BF16)],
        compiler_params=_cp("parallel", "parallel", "arbitrary"),
    )(q, *tabs, kv, kv, kr, kv, kv, kr)
    y_c = pl.pallas_call(
        _mla_ctx_kernel,
        grid=(B, B_HEADS),
        in_specs=[pl.BlockSpec((LC, B_QPAD), lambda b, h: (ctx0 + b, h)), kc_spec, vc_spec, krc_spec],
        out_specs=pl.BlockSpec((LC, B_VDIM), lambda b, h: (b, h)),
        out_shape=jax.ShapeDtypeStruct((B * LC, B_HEADS * B_VDIM), BF16),
        compiler_params=_cp("parallel", "parallel"),
    )(q, kv, kv, kr)
    return jnp.concatenate([y_x, y_c], axis=0)


def _rope_qk_kernel(x_ref, c_ref, s1_ref, s2_ref, o_ref):
    c, s1, s2 = c_ref[...], s1_ref[...], s2_ref[...]
    for h in range(x_ref.shape[1] // LANE):
        sl = slice(h * LANE, (h + 1) * LANE)
        o_ref[:, sl] = _rope_apply(x_ref[:, sl].astype(F32), c, s1, s2, 1).astype(o_ref.dtype)


def _rope_qk(ux):
    width = C_Q + C_KV
    tn = C_KV
    xt = L // T
    nq = C_Q // tn
    tabs = [jnp.concatenate([tq, tk], axis=0)
            for tq, tk in zip(_rope_tables_pairs(HEAD_DIM ** -0.5), _rope_tables_pairs(1.0))]
    tspec = pl.BlockSpec((T, LANE), lambda i, j: (i % xt + jnp.where(j < nq, 0, xt), 0))
    return pl.pallas_call(
        _rope_qk_kernel,
        grid=(B * xt, width // tn),
        in_specs=[pl.BlockSpec((T, tn), lambda i, j: (i, j)), tspec, tspec, tspec],
        out_specs=pl.BlockSpec((T, tn), lambda i, j: (i, j)),
        out_shape=jax.ShapeDtypeStruct((B * L, width), BF16),
        compiler_params=_cp("parallel", "parallel"),
    )(ux, *tabs)


def _win_kernel(sink_ref, q_ref, kx_ref, vx_ref, kc_ref, vc_ref, o_ref):
    g = pl.program_id(1)
    t = pl.program_id(2)
    span = 2 * T
    start = pl.multiple_of(jnp.clip(t * T - C_WINDOW, 0, L - span), C_WINDOW)
    kw = kx_ref[pl.ds(start, span), :]
    vw = vx_ref[pl.ds(start, span), :]
    qpos = t * T + lax.broadcasted_iota(jnp.int32, (T, span), 0)
    kpos = start + lax.broadcasted_iota(jnp.int32, (T, span), 1)
    valid = jnp.abs(qpos - kpos) <= C_WINDOW
    kc = kc_ref[...]
    vc = vc_ref[...]
    for r in range(C_GROUP):
        sl = slice(r * HEAD_DIM, (r + 1) * HEAD_DIM)
        q = q_ref[:, sl]
        sw = jnp.where(valid, _dot_nt(q, kw), NEG_BIG)
        sc = _dot_nt(q, kc)
        sk = sink_ref[g * C_GROUP + r]
        m = jnp.maximum(jnp.maximum(jnp.max(sw, axis=-1, keepdims=True),
                                    jnp.max(sc, axis=-1, keepdims=True)), sk)
        pw = jnp.exp(sw - m)
        pc = jnp.exp(sc - m)
        l = jnp.sum(pw, axis=-1, keepdims=True) + jnp.sum(pc, axis=-1, keepdims=True) + jnp.exp(sk - m)
        o = _dot(pw.astype(BF16), vw) + _dot(pc.astype(BF16), vc)
        o_ref[:, sl] = (o / l).astype(o_ref.dtype)


def _win_attn(qk, ux, sink):
    assert T == 2 * C_WINDOW
    xt = L // T
    ctx0 = B * L // LC
    kcol = C_Q // LANE
    vcol = (C_Q + C_KV) // LANE
    gw = C_GROUP * HEAD_DIM
    return pl.pallas_call(
        _win_kernel,
        grid=(B, C_KV_HEADS, xt),
        in_specs=[pl.BlockSpec(memory_space=pltpu.SMEM),
                  pl.BlockSpec((T, gw), lambda b, g, t: (b * xt + t, g)),
                  pl.BlockSpec((L, LANE), lambda b, g, t: (b, kcol + g)),
                  pl.BlockSpec((L, LANE), lambda b, g, t: (b, vcol + g)),
                  pl.BlockSpec((LC, LANE), lambda b, g, t: (ctx0 + b, kcol + g)),
                  pl.BlockSpec((LC, LANE), lambda b, g, t: (ctx0 + b, vcol + g))],
        out_specs=pl.BlockSpec((T, gw), lambda b, g, t: (b * xt + t, g)),
        out_shape=jax.ShapeDtypeStruct((B * L, C_Q), BF16),
        compiler_params=_cp("parallel", "parallel", "arbitrary"),
    )(sink, qk, qk, ux, ux, ux)


CONV_TN = 512
SUBLANE = 8


def _conv_kernel(bg_ref, cg_ref, hh_ref, cgp_ref, hhp_ref, cgn_ref, hhn_ref, w_ref, o_ref):
    xt = L // T
    t = pl.program_id(0) % xt
    z = cg_ref[...].astype(F32) * hh_ref[...].astype(F32)
    zp = cgp_ref[SUBLANE - 1:SUBLANE, :].astype(F32) * hhp_ref[SUBLANE - 1:SUBLANE, :].astype(F32)
    zn = cgn_ref[0:1, :].astype(F32) * hhn_ref[0:1, :].astype(F32)
    zp = jnp.where(t > 0, zp, 0.0)
    zn = jnp.where(t < xt - 1, zn, 0.0)
    rows = lax.broadcasted_iota(jnp.int32, (T, CONV_TN), 0)
    z_prev = jnp.where(rows == 0, zp, pltpu.roll(z, 1, 0))
    z_next = jnp.where(rows == T - 1, zn, pltpu.roll(z, T - 1, 0))
    y = w_ref[0:1, :] * z_prev + w_ref[1:2, :] * z + w_ref[2:3, :] * z_next
    o_ref[...] = (bg_ref[...].astype(F32) * y).astype(o_ref.dtype)


def _conv(ux, conv_w):
    xt = L // T
    nx = B * xt
    c0 = C_IN // CONV_TN
    nc = D_CH // CONV_TN
    rb = T // SUBLANE

    def main(cb):
        return pl.BlockSpec((T, CONV_TN), lambda i, j: (i, c0 + cb * nc + j))

    def prev(cb):
        return pl.BlockSpec((SUBLANE, CONV_TN), lambda i, j: (jnp.maximum(i * rb - 1, 0), c0 + cb * nc + j))

    def nxt(cb):
        return pl.BlockSpec((SUBLANE, CONV_TN),
                            lambda i, j: (jnp.minimum((i + 1) * rb, nx * rb - 1), c0 + cb * nc + j))

    return pl.pallas_call(
        _conv_kernel,
        grid=(nx, nc),
        in_specs=[main(0), main(1), main(2), prev(1), prev(2), nxt(1), nxt(2),
                  pl.BlockSpec((D_CONV, CONV_TN), lambda i, j: (0, j))],
        out_specs=pl.BlockSpec((T, CONV_TN), lambda i, j: (i, j)),
        out_shape=jax.ShapeDtypeStruct((B * L, D_CH), BF16),
        compiler_params=_cp("parallel", "parallel"),
    )(ux, ux, ux, ux, ux, ux, ux, conv_w)


def _router_kernel(h_ref, wr_ref, b_ref, i_ref, g_ref):
    tm = h_ref.shape[0]
    scores = _sigmoid(_dot_nt(wr_ref[...], h_ref[...]))
    biased = scores + b_ref[...]
    ng = N_GROUPS
    sl = [biased[j * ng:(j + 1) * ng] for j in range(GROUP_SIZE)]
    sc = [scores[j * ng:(j + 1) * ng] for j in range(GROUP_SIZE)]
    m1 = sl[0]
    m2 = jnp.full_like(m1, -jnp.inf)
    for v in sl[1:]:
        m2 = jnp.maximum(m2, jnp.minimum(m1, v))
        m1 = jnp.maximum(m1, v)
    gs = m1 + m2
    gidx = lax.broadcasted_iota(jnp.int32, (ng, tm), 0)
    rank = jnp.zeros((ng, tm), jnp.int32)
    for k in range(1, ng):
        r = pltpu.roll(gs, k, 0)
        rank += jnp.where((r > gs) | ((r == gs) & (gidx >= k)), 1, 0)
    keep = rank < TOPK_GROUPS
    ms = [jnp.where(keep, v, -jnp.inf) for v in sl]
    rolled = [[ms[j] if k == 0 else pltpu.roll(ms[j], k, 0) for k in range(ng)] for j in range(GROUP_SIZE)]
    tot = jnp.zeros((ng, tm), F32)
    picked = []
    ranks = []
    for j in range(GROUP_SIZE):
        cnt = jnp.zeros((ng, tm), jnp.int32)
        for jp in range(GROUP_SIZE):
            for k in range(ng):
                r = rolled[jp][k]
                if k == 0:
                    if jp == j:
                        continue
                    beats = (r >= ms[j]) if jp < j else (r > ms[j])
                else:
                    beats = (r > ms[j]) | ((r == ms[j]) & (gidx >= k))
                cnt += jnp.where(beats, 1, 0)
        w = jnp.where(cnt < TOP_K, sc[j], 0.0)
        picked.append(w)
        ranks.append(cnt)
        tot = tot + w
    tot = jnp.sum(tot, axis=0, keepdims=True)
    wn = [p / tot * ROUTE_SCALE for p in picked]
    eid = [(gidx * GROUP_SIZE + j).astype(F32) for j in range(GROUP_SIZE)]
    for k in range(TOP_K):
        ik = jnp.zeros((ng, tm), F32)
        wk = jnp.zeros((ng, tm), F32)
        for j in range(GROUP_SIZE):
            hit = ranks[j] == k
            ik += jnp.where(hit, eid[j], 0.0)
            wk += jnp.where(hit, wn[j], 0.0)
        i_ref[k:k + 1, :] = jnp.sum(ik, axis=0, keepdims=True).astype(jnp.int32)
        g_ref[k:k + 1, :] = jnp.sum(wk, axis=0, keepdims=True)


def _router(h, wr_t, bias, nrows):
    out = pl.BlockSpec((TOP_K, TM), lambda i: (0, i))
    return pl.pallas_call(
        _router_kernel,
        grid=(nrows // TM,),
        in_specs=[pl.BlockSpec((TM, D), lambda i: (i, 0)),
                  pl.BlockSpec((N_EXPERTS, D), lambda i: (0, 0)),
                  pl.BlockSpec((N_EXPERTS, 1), lambda i: (0, 0))],
        out_specs=[out, out],
        out_shape=[jax.ShapeDtypeStruct((TOP_K, nrows), jnp.int32),
                   jax.ShapeDtypeStruct((TOP_K, nrows), F32)],
        compiler_params=_cp("parallel"),
    )(h, wr_t, bias)


def _router_order():
    r = np.arange(N_EXPERTS)
    return (r % N_GROUPS) * GROUP_SIZE + r // N_GROUPS


def _glu_kernel(x_ref, wg_ref, wu_ref, wd_ref, o_ref):
    x = x_ref[...]
    hid = _silu(_dot(x, wg_ref[...])) * _dot(x, wu_ref[...])
    o_ref[...] = _dot(hid.astype(BF16), wd_ref[...]).astype(o_ref.dtype)


def _glu(h, wg, wu, wd, nrows):
    return pl.pallas_call(
        _glu_kernel,
        grid=(nrows // TM,),
        in_specs=[pl.BlockSpec((TM, D), lambda i: (i, 0)),
                  pl.BlockSpec((D, D_EXPERT), lambda i: (0, 0)),
                  pl.BlockSpec((D, D_EXPERT), lambda i: (0, 0)),
                  pl.BlockSpec((D_EXPERT, D), lambda i: (0, 0))],
        out_specs=pl.BlockSpec((TM, D), lambda i: (i, 0)),
        out_shape=jax.ShapeDtypeStruct((nrows, D), BF16),
        compiler_params=_cp("parallel"),
    )(h, wg, wu, wd)


TME = 256
TD = 512
TC = 128


def _route_plan(idx, nrows):
    nt = (nrows * TOP_K) // TME + N_EXPERTS
    onehot = idx[:, :, None] == jnp.arange(N_EXPERTS, dtype=jnp.int32)[None, None, :]
    cs = jnp.cumsum(jnp.any(onehot, axis=1).astype(jnp.int32), axis=0)
    counts = cs[-1]
    padded = ((counts + TME - 1) // TME) * TME
    ends = jnp.cumsum(padded)
    off = ends - padded
    dest = (off[idx] + jnp.take_along_axis(cs, idx, axis=1) - 1).astype(jnp.int32)
    tiles = jnp.arange(nt, dtype=jnp.int32)
    used = ends[-1] // TME
    t_expert = jnp.sum((ends // TME)[None, :] <= tiles[:, None], axis=1)
    t_expert = jnp.minimum(t_expert, N_EXPERTS - 1).astype(jnp.int32)
    t_valid = (tiles < used).astype(jnp.int32)
    t_first = ((tiles * TME == off[t_expert]) & (tiles < used)).astype(jnp.int32)
    t_src = jnp.minimum(tiles, used - 1).astype(jnp.int32)
    return dict(nt=nt, dest=dest, used=used.reshape(1).astype(jnp.int32),
                pad_start=(off + counts).astype(jnp.int32),
                pad_cnt=(padded - counts).astype(jnp.int32),
                t_expert=t_expert, t_valid=t_valid, t_first=t_first, t_src=t_src)


def _slab(ref, row):
    return ref.at[pl.ds(pl.multiple_of(row * NP, NP), NP), :]


def _dispatch_kernel(ps_ref, pc_ref, used_ref, dest_ref, hp_ref, xs_ref, sem, *, nt):
    i = pl.program_id(0)

    def body(j, c):
        for k in range(TOP_K):
            pltpu.make_async_copy(_slab(hp_ref, j), _slab(xs_ref, dest_ref[0, 0, j * TOP_K + k]), sem).start()
        return c

    lax.fori_loop(0, TD, body, 0)
    for k in range(TOP_K):
        pltpu.make_async_copy(hp_ref, xs_ref.at[pl.ds(0, TD * NP), :], sem).wait()

    @pl.when(i == 0)
    def _():
        def pad_copy(e):
            n = pc_ref[e] * NP
            return pltpu.make_async_copy(hp_ref.at[pl.ds(0, n), :],
                                         xs_ref.at[pl.ds(pl.multiple_of(ps_ref[e] * NP, NP), n), :], sem)

        def tail_copy(t):
            n = TME * NP
            return pltpu.make_async_copy(hp_ref.at[pl.ds(0, n), :],
                                         xs_ref.at[pl.ds(pl.multiple_of(t * n, n), n), :], sem)

        def over(lo, hi, copy, pred):
            def start(e, c):
                @pl.when(pred(e))
                def _():
                    copy(e).start()
                return c

            def wait(e, c):
                @pl.when(pred(e))
                def _():
                    copy(e).wait()
                return c

            lax.fori_loop(lo, hi, start, 0)
            lax.fori_loop(lo, hi, wait, 0)

        over(0, N_EXPERTS, pad_copy, lambda e: pc_ref[e] > 0)
        over(0, nt, tail_copy, lambda t: t >= used_ref[0])


def _dispatch(hp, plan, nrows):
    dest = plan["dest"].reshape(nrows // TD, 1, TD * TOP_K)
    return pl.pallas_call(
        functools.partial(_dispatch_kernel, nt=plan["nt"]),
        grid_spec=pltpu.PrefetchScalarGridSpec(
            num_scalar_prefetch=3,
            grid=(nrows // TD,),
            in_specs=[pl.BlockSpec((1, 1, TD * TOP_K), lambda i, ps, pc, us: (i, 0, 0), memory_space=pltpu.SMEM),
                      pl.BlockSpec((TD * NP, LANE), lambda i, ps, pc, us: (i, 0))],
            out_specs=pl.BlockSpec(memory_space=pl.ANY),
            scratch_shapes=[pltpu.SemaphoreType.DMA(())]),
        out_shape=jax.ShapeDtypeStruct((plan["nt"] * TME * NP, LANE), jnp.uint32),
        compiler_params=pltpu.CompilerParams(dimension_semantics=("arbitrary",), disable_bounds_checks=True,
                                             vmem_limit_bytes=VMEM_LIMIT_BYTES),
    )(plan["pad_start"], plan["pad_cnt"], plan["used"], dest, hp)


def _expert_kernel(te_ref, tv_ref, tf_ref, ts_ref, x_ref, wg_ref, wu_ref, wd_ref, o_ref, wgu_b, wd_b):
    i = pl.program_id(0)

    @pl.when(tv_ref[i] == 0)
    def _():
        o_ref[...] = jnp.zeros_like(o_ref)

    @pl.when(tv_ref[i] == 1)
    def _():
        @pl.when(tf_ref[i] == 1)
        def _():
            wgu_b[:, :D_EXPERT] = wg_ref[0].astype(BF16)
            wgu_b[:, D_EXPERT:] = wu_ref[0].astype(BF16)
            wd_b[...] = wd_ref[0].astype(BF16)

        pieces = []
        for p in range(NP):
            lo, hi = _unpack2(x_ref[pl.ds(p, TME, stride=NP), :])
            pieces += [lo.astype(BF16), hi.astype(BF16)]
        a = _dot(jnp.concatenate(pieces, axis=1), wgu_b[...])
        hid = _silu(a[:, :D_EXPERT]) * a[:, D_EXPERT:]
        y = _dot(hid.astype(BF16), wd_b[...])
        for p in range(NP):
            o_ref[pl.ds(p, TME, stride=NP), :] = _pack2(y[:, p * PACK:p * PACK + LANE],
                                                        y[:, p * PACK + LANE:(p + 1) * PACK])


def _experts(xs, plan, wg, wu, wd):
    nt = plan["nt"]
    rows = pl.BlockSpec((TME * NP, LANE), lambda i, te, tv, tf, ts: (i, 0))
    return pl.pallas_call(
        _expert_kernel,
        grid_spec=pltpu.PrefetchScalarGridSpec(
            num_scalar_prefetch=4,
            grid=(nt,),
            in_specs=[pl.BlockSpec((TME * NP, LANE), lambda i, te, tv, tf, ts: (ts[i], 0)),
                      pl.BlockSpec((1, D, D_EXPERT), lambda i, te, tv, tf, ts: (te[i], 0, 0)),
                      pl.BlockSpec((1, D, D_EXPERT), lambda i, te, tv, tf, ts: (te[i], 0, 0)),
                      pl.BlockSpec((1, D_EXPERT, D), lambda i, te, tv, tf, ts: (te[i], 0, 0))],
            out_specs=rows,
            scratch_shapes=[pltpu.VMEM((D, 2 * D_EXPERT), BF16), pltpu.VMEM((D_EXPERT, D), BF16)]),
        out_shape=jax.ShapeDtypeStruct((nt * TME * NP, LANE), jnp.uint32),
        compiler_params=pltpu.CompilerParams(dimension_semantics=("arbitrary",),
                                             vmem_limit_bytes=EXPERT_VMEM_LIMIT_BYTES),
    )(plan["t_expert"], plan["t_valid"], plan["t_first"], plan["t_src"], xs, wg, wu, wd)


def _combine_kernel(dcur_ref, dnxt_ref, w_ref, r_ref, sh_ref, g_ref, ys_ref, o_ref, buf0, buf1, sem):
    i = pl.program_id(0)
    last = pl.num_programs(0) - 1

    def issue(d_ref, buf, slot):
        def body(j, c):
            for k in range(TOP_K):
                pltpu.make_async_copy(_slab(ys_ref, d_ref[0, 0, j * TOP_K + k]), _slab(buf, k * TC + j),
                                      sem.at[slot]).start()
            return c

        lax.fori_loop(0, TC, body, 0)

    def drain(buf, slot):
        pltpu.make_async_copy(ys_ref.at[pl.ds(0, TC * TOP_K * NP), :], buf, sem.at[slot]).wait()

    def reduce(buf):
        gates = [w_ref[:, k:k + 1] for k in range(TOP_K)]
        gmod = g_ref[0]
        for p in range(NP):
            lo_sl = slice(p * PACK, p * PACK + LANE)
            hi_sl = slice(p * PACK + LANE, (p + 1) * PACK)
            lo_acc = sh_ref[:, lo_sl].astype(F32)
            hi_acc = sh_ref[:, hi_sl].astype(F32)
            for k in range(TOP_K):
                lo, hi = _unpack2(buf[pl.ds(k * TC * NP + p, TC, stride=NP), :])
                lo_acc += gates[k] * lo
                hi_acc += gates[k] * hi
            o_ref[:, lo_sl] = r_ref[:, lo_sl] + gmod[:, lo_sl] * lo_acc
            o_ref[:, hi_sl] = r_ref[:, hi_sl] + gmod[:, hi_sl] * hi_acc

    @pl.when(i == 0)
    def _():
        issue(dcur_ref, buf0, 0)

    @pl.when(i % 2 == 0)
    def _():
        @pl.when(i < last)
        def _():
            issue(dnxt_ref, buf1, 1)

        drain(buf0, 0)
        reduce(buf0)

    @pl.when(i % 2 == 1)
    def _():
        @pl.when(i < last)
        def _():
            issue(dnxt_ref, buf0, 0)

        drain(buf1, 1)
        reduce(buf1)


def _combine(resid, ys, shared, w, plan, gate, nrows):
    nsteps = nrows // TC
    dest = plan["dest"].reshape(nsteps, 1, TC * TOP_K)
    blk = pl.BlockSpec((TC, D), lambda i: (i, 0))
    return pl.pallas_call(
        _combine_kernel,
        grid=(nsteps,),
        in_specs=[pl.BlockSpec((1, 1, TC * TOP_K), lambda i: (i, 0, 0), memory_space=pltpu.SMEM),
                  pl.BlockSpec((1, 1, TC * TOP_K), lambda i: (jnp.minimum(i + 1, nsteps - 1), 0, 0),
                               memory_space=pltpu.SMEM),
                  pl.BlockSpec((TC, TOP_K), lambda i: (i, 0)),
                  blk, blk,
                  pl.BlockSpec((1, 1, D), lambda i: (_mod_row(i, TC), 0, 0)),
                  pl.BlockSpec(memory_space=pl.ANY)],
        out_specs=blk,
        out_shape=jax.ShapeDtypeStruct((nrows, D), F32),
        scratch_shapes=[pltpu.VMEM((TC * TOP_K * NP, LANE), jnp.uint32),
                        pltpu.VMEM((TC * TOP_K * NP, LANE), jnp.uint32),
                        pltpu.SemaphoreType.DMA((2,))],
        compiler_params=pltpu.CompilerParams(dimension_semantics=("arbitrary",),
                                             vmem_limit_bytes=VMEM_LIMIT_BYTES, disable_bounds_checks=True),
    )(dest, dest, w, resid, shared, gate, ys)


def _moe(s, h, hp, gate, nrows, router_w, router_bias, wg, wu, wd, sg, su, sd):
    order = _router_order()
    wr_t = router_w.T[order].astype(BF16)
    bias = router_bias[order].reshape(N_EXPERTS, 1).astype(F32)
    idx_t, w_t = _router(h, wr_t, bias, nrows)
    plan = _route_plan(idx_t.T, nrows)
    xs = _dispatch(hp, plan, nrows)
    ys = _experts(xs, plan, wg, wu, wd)
    shared = _glu(h, sg.astype(BF16), su.astype(BF16), sd.astype(BF16), nrows)
    return _combine(s, ys, shared, w_t.T, plan, gate, nrows)


def _mla_in_weights(w_in):
    wb = w_in[:, A_IN:]
    kr = wb[:, B_Q_RANK + B_KV_RANK:][:, _rope_perm(B_ROPE)]
    pad = jnp.zeros((D, LANE - B_ROPE), w_in.dtype)
    return jnp.concatenate([wb[:, :B_Q_RANK + B_KV_RANK], kr, pad], axis=1).astype(BF16)


def _mla_uq_weights(w_uq):
    scale = (B_NOPE + B_ROPE) ** -0.5 * np.log2(np.e)
    w = (w_uq * scale).reshape(B_Q_RANK, B_HEADS, B_NOPE + B_ROPE)
    rope = w[:, :, B_NOPE:][:, :, _rope_perm(B_ROPE)]
    pad = jnp.zeros((B_Q_RANK, B_HEADS, B_QPAD - B_NOPE - B_ROPE), w.dtype)
    return jnp.concatenate([w[:, :, :B_NOPE], rope, pad], axis=-1).reshape(B_Q_RANK, B_HEADS * B_QPAD).astype(BF16)


def kernel(x, c, ctx, c_ctx, ada_w, ada_b, norm_mix_w, norm_ffn_w, ab_w_in, ab_w_out, hgrn_lb_logits, hgrn_gnorm_w, mla_q_norm_w, mla_w_uq, mla_kv_norm_w, mla_w_ukv, cd_w_in, cd_w_out, gqa_sink, conv_w, router_w, router_bias, exp_w_gate, exp_w_up, exp_w_down, sh_w_gate, sh_w_up, sh_w_down, final_norm_w):
    nx = B * L
    n = nx + B * LC
    s = jnp.concatenate([x.reshape(nx, D), ctx.reshape(B * LC, D)], axis=0)
    lb_all = jnp.cumsum(jax.nn.softmax(hgrn_lb_logits.astype(F32), axis=0), axis=0)
    cond = jnp.concatenate([c, c_ctx[None, :], jnp.zeros((8 - B - 1, D), F32)], axis=0)

    def modulation(i):
        m = _ada(cond, ada_w[i], ada_b[i].reshape(1, N_MOD * D)).reshape(8, N_MOD, D)
        return [m[:B + 1, k, :].reshape(B + 1, 1, D) for k in range(N_MOD)]

    def moe_args(i):
        return (router_w[i], router_bias[i], exp_w_gate[i], exp_w_up[i], exp_w_down[i],
                sh_w_gate[i], sh_w_up[i], sh_w_down[i])

    mod = modulation(0)
    h = _norm_mod(s, norm_mix_w[0], mod[1], mod[0], n)
    ua = _matmul_wres(h, ab_w_in[0], A_IN)
    ub = _matmul(h, _mla_in_weights(ab_w_in[0]), B_Q_RANK + B_KV_RANK + LANE)
    lb = lb_all[0].reshape(2, A_HEADS, 1, A_DK)
    o_f, o_b = _hgrn(ua, lb[0], lb[1])
    ya = _hgrn_out(o_f, o_b, ua, hgrn_gnorm_w[0])
    tabs = _rope_tables(B_ROPE, True)
    q = _norm_mm(ub, 0, B_Q_RANK, mla_q_norm_w[0], _mla_uq_weights(mla_w_uq[0]), 1024)
    kv = _norm_mm(ub, B_Q_RANK // B_KV_RANK, B_KV_RANK, mla_kv_norm_w[0], mla_w_ukv[0].astype(BF16), 1024)
    kr = _krope(ub, tabs)
    yb = _mla_attn(q, kv, kr, tuple(t[T:] for t in tabs))
    s = _mm2_res(ya, yb, ab_w_out[0].astype(BF16), s, mod[2], n)
    h, hp = _norm_mod_pack(s, norm_ffn_w[0], mod[4], mod[3], n)
    s = _moe(s, h, hp, mod[5], n, *moe_args(0))

    mod = modulation(1)
    h = _norm_mod(s, norm_mix_w[1], mod[1], mod[0], n)
    ux = _matmul_wres(h, cd_w_in[0], C_IN + 3 * D_CH)
    qk = _rope_qk(ux)
    att = _win_attn(qk, ux, gqa_sink[0].astype(F32))
    cv = _conv(ux, conv_w[0])
    s = _mm2_res(att, cv, cd_w_out[0].astype(BF16), s, mod[2], nx)
    h, hp = _norm_mod_pack(s, norm_ffn_w[1], mod[4], mod[3], nx)
    s = _moe(s, h, hp, mod[5], nx, *moe_args(1))
    return _rms(s, final_norm_w).reshape(B, L, D)
```

```python
import functools

import jax
import jax.numpy as jnp
import numpy as np
from jax import lax
from jax.experimental import pallas as pl
from jax.experimental.pallas import tpu as pltpu

F32 = jnp.float32
BF16 = jnp.bfloat16

D = 4096
B = 2
L = 8192
LC = 256
GRID_W = 64
RMS_EPS = 1e-6
ROPE_THETA = 10000.0
N_MOD = 6

A_HEADS = 16
A_DK = 128
A_KEY = A_HEADS * A_DK
A_CHUNK = 32
A_IN = 5 * A_KEY

B_HEADS = 16
B_Q_RANK = 1024
B_KV_RANK = 512
B_NOPE = 128
B_ROPE = 64
B_VDIM = 128
B_QPAD = 256

C_HEADS = 16
C_KV_HEADS = 4
C_GROUP = C_HEADS // C_KV_HEADS
C_WINDOW = 128
HEAD_DIM = 128
C_Q = C_HEADS * HEAD_DIM
C_KV = C_KV_HEADS * HEAD_DIM
C_IN = C_Q + 2 * C_KV
D_CH = 2048
D_CONV = 3

N_EXPERTS = 64
TOP_K = 8
N_GROUPS = 8
TOPK_GROUPS = 4
GROUP_SIZE = N_EXPERTS // N_GROUPS
D_EXPERT = 256
ROUTE_SCALE = 2.5

T = 256
TM = 512
LANE = 128
VMEM_LIMIT_BYTES = 48 * 1024 * 1024
EXPERT_VMEM_LIMIT_BYTES = 56 * 1024 * 1024
NEG_BIG = -1e30
PACK = 2 * LANE
NP = D // PACK


def _cp(*sem):
    return pltpu.CompilerParams(dimension_semantics=sem, vmem_limit_bytes=VMEM_LIMIT_BYTES)


def _sigmoid(x):
    return 1.0 / (1.0 + jnp.exp(-x))


def _silu(x):
    return x * _sigmoid(x)


def _dot(a, b):
    return jnp.dot(a, b, preferred_element_type=F32)


def _dot_nt(a, b):
    return lax.dot_general(a, b, (((1,), (1,)), ((), ())), preferred_element_type=F32)


def _mod_row(i, tile):
    return jnp.where(i < (B * L) // tile, i // (L // tile), B)


def _ada_kernel(a_ref, w_ref, b_ref, o_ref):
    a = _silu(a_ref[...]).astype(BF16)
    o_ref[...] = _dot(a, w_ref[0].astype(BF16)) + b_ref[...]


def _ada(a, w, bias, layer):
    n = w.shape[2]
    tn = 512
    return pl.pallas_call(
        _ada_kernel,
        grid=(n // tn,),
        in_specs=[pl.BlockSpec((8, D), lambda j: (0, 0)),
                  pl.BlockSpec((1, D, tn), lambda j: (layer, 0, j)),
                  pl.BlockSpec((1, tn), lambda j: (0, j))],
        out_specs=pl.BlockSpec((8, tn), lambda j: (0, j)),
        out_shape=jax.ShapeDtypeStruct((8, n), F32),
        compiler_params=_cp("parallel"),
    )(a, w, bias)


def _norm_mod_kernel(s_ref, w_ref, sc_ref, sh_ref, o_ref):
    x = s_ref[...]
    y = x * lax.rsqrt(jnp.mean(x * x, axis=-1, keepdims=True) + RMS_EPS) * w_ref[...]
    o_ref[...] = (y * (1.0 + sc_ref[0]) + sh_ref[0]).astype(o_ref.dtype)


def _norm_mod(s, w, scale, shift, nrows):
    mod_spec = pl.BlockSpec((1, 1, D), lambda i: (_mod_row(i, T), 0, 0))
    return pl.pallas_call(
        _norm_mod_kernel,
        grid=(nrows // T,),
        in_specs=[pl.BlockSpec((T, D), lambda i: (i, 0)),
                  pl.BlockSpec((1, D), lambda i: (0, 0)),
                  mod_spec, mod_spec],
        out_specs=pl.BlockSpec((T, D), lambda i: (i, 0)),
        out_shape=jax.ShapeDtypeStruct((nrows, D), BF16),
        compiler_params=_cp("parallel"),
    )(s, w.reshape(1, D), scale, shift)


def _pack2(lo, hi):
    lo_b = lax.bitcast_convert_type(lo.astype(BF16).astype(F32), jnp.uint32) >> 16
    hi_b = lax.bitcast_convert_type(hi.astype(BF16).astype(F32), jnp.uint32)
    return hi_b | lo_b


def _unpack2(w):
    lo = lax.bitcast_convert_type(w << 16, F32)
    hi = lax.bitcast_convert_type(w & jnp.uint32(0xFFFF0000), F32)
    return lo, hi


def _norm_mod_pack_kernel(s_ref, w_ref, sc_ref, sh_ref, o_ref, p_ref):
    x = s_ref[...]
    y = x * lax.rsqrt(jnp.mean(x * x, axis=-1, keepdims=True) + RMS_EPS) * w_ref[...]
    y = y * (1.0 + sc_ref[0]) + sh_ref[0]
    o_ref[...] = y.astype(o_ref.dtype)
    for p in range(NP):
        p_ref[pl.ds(p, T, stride=NP), :] = _pack2(y[:, p * PACK:p * PACK + LANE], y[:, p * PACK + LANE:(p + 1) * PACK])


def _norm_mod_pack(s, w, scale, shift, nrows):
    mod_spec = pl.BlockSpec((1, 1, D), lambda i: (_mod_row(i, T), 0, 0))
    return pl.pallas_call(
        _norm_mod_pack_kernel,
        grid=(nrows // T,),
        in_specs=[pl.BlockSpec((T, D), lambda i: (i, 0)),
                  pl.BlockSpec((1, D), lambda i: (0, 0)),
                  mod_spec, mod_spec],
        out_specs=[pl.BlockSpec((T, D), lambda i: (i, 0)),
                   pl.BlockSpec((T * NP, LANE), lambda i: (i, 0))],
        out_shape=[jax.ShapeDtypeStruct((nrows, D), BF16),
                   jax.ShapeDtypeStruct((nrows * NP, LANE), jnp.uint32)],
        compiler_params=_cp("parallel"),
    )(s, w.reshape(1, D), scale, shift)


def _rms_kernel(s_ref, w_ref, o_ref):
    x = s_ref[...]
    o_ref[...] = x * lax.rsqrt(jnp.mean(x * x, axis=-1, keepdims=True) + RMS_EPS) * w_ref[...]


def _rms(s, w):
    nrows = s.shape[0]
    return pl.pallas_call(
        _rms_kernel,
        grid=(nrows // T,),
        in_specs=[pl.BlockSpec((T, D), lambda i: (i, 0)),
                  pl.BlockSpec((1, D), lambda i: (0, 0))],
        out_specs=pl.BlockSpec((T, D), lambda i: (i, 0)),
        out_shape=jax.ShapeDtypeStruct((nrows, D), F32),
        compiler_params=_cp("parallel"),
    )(s, w.reshape(1, D))


def _mm_kernel(a_ref, w_ref, o_ref):
    o_ref[...] = _dot(a_ref[...], w_ref[...]).astype(o_ref.dtype)


def _matmul(a, w, tn):
    m, k = a.shape
    n = w.shape[1]
    return pl.pallas_call(
        _mm_kernel,
        grid=(m // TM, n // tn),
        in_specs=[pl.BlockSpec((TM, k), lambda i, j: (i, 0)),
                  pl.BlockSpec((k, tn), lambda i, j: (0, j))],
        out_specs=pl.BlockSpec((TM, tn), lambda i, j: (i, j)),
        out_shape=jax.ShapeDtypeStruct((m, n), BF16),
        compiler_params=_cp("parallel", "arbitrary"),
    )(a, w)


def _norm_mm_kernel(a_ref, nw_ref, w_ref, o_ref):
    x = a_ref[...].astype(F32)
    y = x * lax.rsqrt(jnp.mean(x * x, axis=-1, keepdims=True) + RMS_EPS) * nw_ref[...]
    o_ref[...] = _dot(y.astype(BF16), w_ref[...]).astype(o_ref.dtype)


def _norm_mm(u, acol, kdim, nw, w, tn):
    m = u.shape[0]
    n = w.shape[1]
    return pl.pallas_call(
        _norm_mm_kernel,
        grid=(m // TM, n // tn),
        in_specs=[pl.BlockSpec((TM, kdim), lambda i, j: (i, acol)),
                  pl.BlockSpec((1, kdim), lambda i, j: (0, 0)),
                  pl.BlockSpec((kdim, tn), lambda i, j: (0, j))],
        out_specs=pl.BlockSpec((TM, tn), lambda i, j: (i, j)),
        out_shape=jax.ShapeDtypeStruct((m, n), BF16),
        compiler_params=_cp("parallel", "arbitrary"),
    )(u, nw.reshape(1, kdim), w)


def _mm2_res_kernel(a1_ref, a2_ref, w1_ref, w2_ref, r_ref, g_ref, o_ref):
    acc = _dot(a1_ref[...], w1_ref[...]) + _dot(a2_ref[...], w2_ref[...])
    o_ref[...] = r_ref[...] + g_ref[0] * acc


def _mm2_res(a1, a2, w, resid, gate, nrows):
    k1 = a1.shape[1]
    k2 = a2.shape[1]
    assert k1 == k2 and w.shape == (k1 + k2, D)
    tn = 512
    return pl.pallas_call(
        _mm2_res_kernel,
        grid=(nrows // TM, D // tn),
        in_specs=[pl.BlockSpec((TM, k1), lambda i, j: (i, 0)),
                  pl.BlockSpec((TM, k2), lambda i, j: (i, 0)),
                  pl.BlockSpec((k1, tn), lambda i, j: (0, j)),
                  pl.BlockSpec((k2, tn), lambda i, j: (1, j)),
                  pl.BlockSpec((TM, tn), lambda i, j: (i, j)),
                  pl.BlockSpec((1, 1, tn), lambda i, j: (_mod_row(i, TM), 0, j))],
        out_specs=pl.BlockSpec((TM, tn), lambda i, j: (i, j)),
        out_shape=jax.ShapeDtypeStruct((nrows, D), F32),
        compiler_params=_cp("parallel", "arbitrary"),
    )(a1, a2, w, w, resid, gate)


def _split3(x):
    hi = x.astype(BF16)
    r = x - hi.astype(F32)
    mid = r.astype(BF16)
    lo = (r - mid.astype(F32)).astype(BF16)
    return hi, mid, lo


def _hgrn_dir(q_raw, f_raw, v, lb, st_ref, o_ref, rev):
    c = A_CHUNK
    q = _silu(q_raw.astype(F32)) * (A_DK ** -0.5)
    f = lb + (1.0 - lb) * _sigmoid(f_raw.astype(F32))
    k = 1.0 - f
    logf = jnp.log(f)
    row = lax.broadcasted_iota(jnp.int32, (T, T), 0)
    col = lax.broadcasted_iota(jnp.int32, (T, T), 1)
    mask = ((row ^ col) < c) & ((col >= row) if rev else (col <= row))
    tri = jnp.where(mask, 1.0, 0.0).astype(BF16)
    hi, mid, lo = _split3(logf)
    cum = _dot(tri, hi) + _dot(tri, mid) + _dot(tri, lo)
    nchunk = T // c
    cum3 = cum.reshape(nchunk, c, LANE)
    ref = cum3[:, c // 2 - 1:c // 2] if rev else cum3[:, c // 2:c // 2 + 1]
    last = cum3[:, 0:1] if rev else cum3[:, c - 1:c]
    q3 = q.reshape(nchunk, c, LANE)
    k3 = k.reshape(nchunk, c, LANE)
    qs = (q3 * jnp.exp(cum3 - ref)).reshape(T, LANE).astype(BF16)
    ks = (k3 * jnp.exp(ref - cum3)).reshape(T, LANE).astype(BF16)
    qd = (q * jnp.exp(cum)).astype(BF16)
    kd = (k3 * jnp.exp(last - cum3)).reshape(T, LANE).astype(BF16)
    decay = jnp.exp(last)
    att = jnp.where(mask, _dot_nt(qs, ks), 0.0)
    o = _dot(att.astype(BF16), v)
    kv = [lax.dot_general(v[n * c:(n + 1) * c], kd[n * c:(n + 1) * c], (((0,), (0,)), ((), ())),
                          preferred_element_type=F32) for n in range(nchunk)]
    st = st_ref[...]
    o_inter = [None] * nchunk
    for n in (range(nchunk - 1, -1, -1) if rev else range(nchunk)):
        o_inter[n] = _dot_nt(qd[n * c:(n + 1) * c], st.astype(BF16))
        st = st * decay[n] + kv[n]
    st_ref[...] = st
    o_ref[...] = (o + jnp.concatenate(o_inter, axis=0)).astype(o_ref.dtype)


def _hgrn_kernel(qf, ff, vf, qb, fb, vb, lbf, lbb, of, ob, stf, stb):
    @pl.when(pl.program_id(2) == 0)
    def _():
        stf[...] = jnp.zeros_like(stf)
        stb[...] = jnp.zeros_like(stb)

    _hgrn_dir(qf[...], ff[...], vf[...], lbf[0], stf, of, False)
    _hgrn_dir(qb[...], fb[...], vb[...], lbb[0], stb, ob, True)


def _hgrn(u, lbf, lbb):
    assert LC == T
    xt = L // T
    n_rows = u.shape[0]
    hh = A_HEADS

    def blk_f(b, n):
        return jnp.where(n == 0, B * xt + b, b * xt + n - 1)

    def blk_b(b, n):
        return jnp.where(n == 0, B * xt + b, b * xt + xt - n)

    def spec(blk, cb):
        return pl.BlockSpec((T, LANE), lambda b, h, n: (blk(b, n), cb * hh + h))

    lb_spec = pl.BlockSpec((1, 1, LANE), lambda b, h, n: (h, 0, 0))
    return pl.pallas_call(
        _hgrn_kernel,
        grid=(B, hh, xt + 1),
        in_specs=[spec(blk_f, 0), spec(blk_f, 1), spec(blk_f, 3),
                  spec(blk_b, 0), spec(blk_b, 2), spec(blk_b, 3),
                  lb_spec, lb_spec],
        out_specs=[spec(blk_f, 0), spec(blk_b, 0)],
        out_shape=[jax.ShapeDtypeStruct((n_rows, A_KEY), BF16)] * 2,
        scratch_shapes=[pltpu.VMEM((LANE, LANE), F32)] * 2,
        compiler_params=_cp("parallel", "parallel", "arbitrary"),
    )(u, u, u, u, u, u, lbf, lbb)


def _hgrn_out_kernel(of_ref, ob_ref, g_ref, w_ref, y_ref):
    for h in range(A_HEADS):
        sl = slice(h * LANE, (h + 1) * LANE)
        o = of_ref[:, sl].astype(F32) + ob_ref[:, sl].astype(F32)
        o = o * lax.rsqrt(jnp.mean(o * o, axis=-1, keepdims=True) + RMS_EPS) * w_ref[...]
        y_ref[:, sl] = (o * _silu(g_ref[:, sl].astype(F32))).astype(y_ref.dtype)


def _hgrn_out(o_f, o_b, u, gnorm_w):
    n_rows = o_f.shape[0]
    blk = pl.BlockSpec((T, A_KEY), lambda i: (i, 0))
    return pl.pallas_call(
        _hgrn_out_kernel,
        grid=(n_rows // T,),
        in_specs=[blk, blk,
                  pl.BlockSpec((T, A_KEY), lambda i: (i, 4)),
                  pl.BlockSpec((1, LANE), lambda i: (0, 0))],
        out_specs=blk,
        out_shape=jax.ShapeDtypeStruct((n_rows, A_KEY), BF16),
        compiler_params=_cp("parallel"),
    )(o_f, o_b, u, gnorm_w.reshape(1, LANE))


def _rope_perm(n):
    return np.concatenate([np.arange(0, n, 2), np.arange(1, n, 2)])


def _rope_cos_sin(rot_dim):
    rows = L // GRID_W
    row = jnp.repeat(jnp.arange(rows), GRID_W).astype(F32)
    col = jnp.tile(jnp.arange(GRID_W), rows).astype(F32)
    n_freq = rot_dim // 4
    inv = ROPE_THETA ** (-jnp.arange(n_freq, dtype=F32) / n_freq)
    ang = jnp.concatenate([row[:, None] * inv, col[:, None] * inv], axis=-1)
    return jnp.cos(ang), jnp.sin(ang)


def _rope_tables_pairs(scale):
    cos, sin = _rope_cos_sin(LANE)
    zero = jnp.zeros_like(sin)
    c = jnp.repeat(cos, 2, axis=-1) * scale
    s1 = jnp.stack([-sin, zero], axis=-1).reshape(L, LANE) * scale
    s2 = jnp.stack([zero, sin], axis=-1).reshape(L, LANE) * scale
    return c, s1, s2


def _rope_tables(rot_dim, with_ctx):
    half = rot_dim // 2
    cos, sin = _rope_cos_sin(rot_dim)
    zpad = jnp.zeros((L, LANE - rot_dim), F32)
    zh = jnp.zeros((L, half), F32)
    c = jnp.concatenate([cos, cos, zpad], axis=-1)
    s1 = jnp.concatenate([-sin, zh, zpad], axis=-1)
    s2 = jnp.concatenate([zh, sin, zpad], axis=-1)
    if with_ctx:
        ident = jnp.ones((T, LANE), F32)
        zero = jnp.zeros((T, LANE), F32)
        c = jnp.concatenate([ident, c], axis=0)
        s1 = jnp.concatenate([zero, s1], axis=0)
        s2 = jnp.concatenate([zero, s2], axis=0)
    return c, s1, s2


def _rope_apply(x, c, s1, s2, half):
    return x * c + pltpu.roll(x, LANE - half, 1) * s1 + pltpu.roll(x, half, 1) * s2


def _tab_blk(i):
    xt = L // T
    return jnp.where(i < B * xt, 1 + i % xt, 0)


def _krope_kernel(x_ref, c_ref, s1_ref, s2_ref, o_ref):
    x = x_ref[...].astype(F32)
    o_ref[...] = _rope_apply(x, c_ref[...], s1_ref[...], s2_ref[...], B_ROPE // 2).astype(o_ref.dtype)


def _krope(ub, tabs):
    n_rows = ub.shape[0]
    cb = ub.shape[1] // LANE - 1
    tspec = pl.BlockSpec((T, LANE), lambda i: (_tab_blk(i), 0))
    return pl.pallas_call(
        _krope_kernel,
        grid=(n_rows // T,),
        in_specs=[pl.BlockSpec((T, LANE), lambda i: (i, cb)), tspec, tspec, tspec],
        out_specs=pl.BlockSpec((T, LANE), lambda i: (i, 0)),
        out_shape=jax.ShapeDtypeStruct((n_rows, LANE), BF16),
        compiler_params=_cp("parallel"),
    )(ub, *tabs)


MLA_KV_CHUNK = 512
MLA_TQ = 512


def _softmax_step(q, k, v1, m, acc):
    s = _dot_nt(q, k)
    m_new = jnp.maximum(m, jnp.max(s, axis=-1, keepdims=True))
    p = jnp.exp2((s - m_new).astype(BF16))
    acc = jnp.exp2(m - m_new) * acc + _dot(p, v1)
    return m_new, acc


def _softmax_init(rows):
    return jnp.full((rows, 1), NEG_BIG, F32), jnp.zeros((rows, 2 * B_VDIM), F32)


def _softmax_finish(acc):
    return acc[:, :B_VDIM] / acc[:, B_VDIM:B_VDIM + 1]


def _mla_attn_kernel(q_ref, c_ref, s1_ref, s2_ref, kxn_ref, vx_ref, krx_ref, kcn_ref, vc_ref, krc_ref,
                     o_ref, kfx, kfc, vfx, vfc):
    @pl.when(pl.program_id(2) == 0)
    def _():
        kfx[:, :B_NOPE] = kxn_ref[...]
        kfx[:, B_NOPE:] = krx_ref[...]
        kfc[:, :B_NOPE] = kcn_ref[...]
        kfc[:, B_NOPE:] = krc_ref[...]
        vfx[:, :B_VDIM] = vx_ref[...]
        vfx[:, B_VDIM:] = jnp.ones((L, B_VDIM), BF16)
        vfc[:, :B_VDIM] = vc_ref[...]
        vfc[:, B_VDIM:] = jnp.ones((LC, B_VDIM), BF16)

    qr = _rope_apply(q_ref[:, B_NOPE:].astype(F32), c_ref[...], s1_ref[...], s2_ref[...], B_ROPE // 2)
    q = jnp.concatenate([q_ref[:, :B_NOPE], qr.astype(BF16)], axis=1)
    m, acc = _softmax_step(q, kfc[...], vfc[...], *_softmax_init(MLA_TQ))
    for n in range(L // MLA_KV_CHUNK):
        sl = slice(n * MLA_KV_CHUNK, (n + 1) * MLA_KV_CHUNK)
        m, acc = _softmax_step(q, kfx[sl, :], vfx[sl, :], m, acc)
    o_ref[...] = _softmax_finish(acc).astype(o_ref.dtype)


def _mla_ctx_kernel(q_ref, kcn_ref, vc_ref, krc_ref, o_ref):
    k = jnp.concatenate([kcn_ref[...], krc_ref[...]], axis=1)
    v1 = jnp.concatenate([vc_ref[...], jnp.ones((LC, B_VDIM), BF16)], axis=1)
    m, acc = _softmax_step(q_ref[...], k, v1, *_softmax_init(LC))
    o_ref[...] = _softmax_finish(acc).astype(o_ref.dtype)


def _mla_attn(q, kv, kr, tabs):
    xq = L // MLA_TQ
    ctx0 = B * L // LC
    tspec = pl.BlockSpec((MLA_TQ, LANE), lambda b, h, qi: (qi, 0))
    kc_spec = pl.BlockSpec((LC, LANE), lambda b, h, *_: (ctx0 + b, 2 * h))
    vc_spec = pl.BlockSpec((LC, LANE), lambda b, h, *_: (ctx0 + b, 2 * h + 1))
    krc_spec = pl.BlockSpec((LC, LANE), lambda b, h, *_: (ctx0 + b, 0))
    y_x = pl.pallas_call(
        _mla_attn_kernel,
        grid=(B, B_HEADS, xq),
        in_specs=[pl.BlockSpec((MLA_TQ, B_QPAD), lambda b, h, qi: (b * xq + qi, h)),
                  tspec, tspec, tspec,
                  pl.BlockSpec((L, LANE), lambda b, h, qi: (b, 2 * h)),
                  pl.BlockSpec((L, LANE), lambda b, h, qi: (b, 2 * h + 1)),
                  pl.BlockSpec((L, LANE), lambda b, h, qi: (b, 0)),
                  kc_spec, vc_spec, krc_spec],
        out_specs=pl.BlockSpec((MLA_TQ, B_VDIM), lambda b, h, qi: (b * xq + qi, h)),
        out_shape=jax.ShapeDtypeStruct((B * L, B_HEADS * B_VDIM), BF16),
        scratch_shapes=[pltpu.VMEM((L, B_QPAD), BF16), pltpu.VMEM((LC, B_QPAD), BF16),
                        pltpu.VMEM((L, 2 * B_VDIM), BF16), pltpu.VMEM((LC, 2 * B_VDIM), BF16)],
        compiler_params=_cp("parallel", "parallel", "arbitrary"),
    )(q, *tabs, kv, kv, kr, kv, kv, kr)
    y_c = pl.pallas_call(
        _mla_ctx_kernel,
        grid=(B, B_HEADS),
        in_specs=[pl.BlockSpec((LC, B_QPAD), lambda b, h: (ctx0 + b, h)), kc_spec, vc_spec, krc_spec],
        out_specs=pl.BlockSpec((LC, B_VDIM), lambda b, h: (b, h)),
        out_shape=jax.ShapeDtypeStruct((B * LC, B_HEADS * B_VDIM), BF16),
        compiler_params=_cp("parallel", "parallel"),
    )(q, kv, kv, kr)
    return jnp.concatenate([y_x, y_c], axis=0)


def _rope_qk_kernel(x_ref, c_ref, s1_ref, s2_ref, o_ref):
    c, s1, s2 = c_ref[...], s1_ref[...], s2_ref[...]
    for h in range(x_ref.shape[1] // LANE):
        sl = slice(h * LANE, (h + 1) * LANE)
        o_ref[:, sl] = _rope_apply(x_ref[:, sl].astype(F32), c, s1, s2, 1).astype(o_ref.dtype)


def _rope_qk(ux):
    width = C_Q + C_KV
    tn = C_KV
    xt = L // T
    nq = C_Q // tn
    tabs = [jnp.concatenate([tq, tk], axis=0)
            for tq, tk in zip(_rope_tables_pairs(HEAD_DIM ** -0.5), _rope_tables_pairs(1.0))]
    tspec = pl.BlockSpec((T, LANE), lambda i, j: (i % xt + jnp.where(j < nq, 0, xt), 0))
    return pl.pallas_call(
        _rope_qk_kernel,
        grid=(B * xt, width // tn),
        in_specs=[pl.BlockSpec((T, tn), lambda i, j: (i, j)), tspec, tspec, tspec],
        out_specs=pl.BlockSpec((T, tn), lambda i, j: (i, j)),
        out_shape=jax.ShapeDtypeStruct((B * L, width), BF16),
        compiler_params=_cp("parallel", "parallel"),
    )(ux, *tabs)


def _win_kernel(sink_ref, q_ref, kx_ref, vx_ref, kc_ref, vc_ref, o_ref):
    g = pl.program_id(1)
    t = pl.program_id(2)
    span = 2 * T
    start = pl.multiple_of(jnp.clip(t * T - C_WINDOW, 0, L - span), C_WINDOW)
    kw = kx_ref[pl.ds(start, span), :]
    vw = vx_ref[pl.ds(start, span), :]
    qpos = t * T + lax.broadcasted_iota(jnp.int32, (T, span), 0)
    kpos = start + lax.broadcasted_iota(jnp.int32, (T, span), 1)
    valid = jnp.abs(qpos - kpos) <= C_WINDOW
    kc = kc_ref[...]
    vc = vc_ref[...]
    for r in range(C_GROUP):
        sl = slice(r * HEAD_DIM, (r + 1) * HEAD_DIM)
        q = q_ref[:, sl]
        sw = jnp.where(valid, _dot_nt(q, kw), NEG_BIG)
        sc = _dot_nt(q, kc)
        sk = sink_ref[g * C_GROUP + r]
        m = jnp.maximum(jnp.maximum(jnp.max(sw, axis=-1, keepdims=True),
                                    jnp.max(sc, axis=-1, keepdims=True)), sk)
        pw = jnp.exp(sw - m)
        pc = jnp.exp(sc - m)
        l = jnp.sum(pw, axis=-1, keepdims=True) + jnp.sum(pc, axis=-1, keepdims=True) + jnp.exp(sk - m)
        o = _dot(pw.astype(BF16), vw) + _dot(pc.astype(BF16), vc)
        o_ref[:, sl] = (o / l).astype(o_ref.dtype)


def _win_attn(qk, ux, sink):
    assert T == 2 * C_WINDOW
    xt = L // T
    ctx0 = B * L // LC
    kcol = C_Q // LANE
    vcol = (C_Q + C_KV) // LANE
    gw = C_GROUP * HEAD_DIM
    return pl.pallas_call(
        _win_kernel,
        grid=(B, C_KV_HEADS, xt),
        in_specs=[pl.BlockSpec(memory_space=pltpu.SMEM),
                  pl.BlockSpec((T, gw), lambda b, g, t: (b * xt + t, g)),
                  pl.BlockSpec((L, LANE), lambda b, g, t: (b, kcol + g)),
                  pl.BlockSpec((L, LANE), lambda b, g, t: (b, vcol + g)),
                  pl.BlockSpec((LC, LANE), lambda b, g, t: (ctx0 + b, kcol + g)),
                  pl.BlockSpec((LC, LANE), lambda b, g, t: (ctx0 + b, vcol + g))],
        out_specs=pl.BlockSpec((T, gw), lambda b, g, t: (b * xt + t, g)),
        out_shape=jax.ShapeDtypeStruct((B * L, C_Q), BF16),
        compiler_params=_cp("parallel", "parallel", "arbitrary"),
    )(sink, qk, qk, ux, ux, ux)


CONV_TN = 512
SUBLANE = 8


def _conv_kernel(bg_ref, cg_ref, hh_ref, cgp_ref, hhp_ref, cgn_ref, hhn_ref, w_ref, o_ref):
    xt = L // T
    t = pl.program_id(0) % xt
    z = cg_ref[...].astype(F32) * hh_ref[...].astype(F32)
    zp = cgp_ref[SUBLANE - 1:SUBLANE, :].astype(F32) * hhp_ref[SUBLANE - 1:SUBLANE, :].astype(F32)
    zn = cgn_ref[0:1, :].astype(F32) * hhn_ref[0:1, :].astype(F32)
    zp = jnp.where(t > 0, zp, 0.0)
    zn = jnp.where(t < xt - 1, zn, 0.0)
    rows = lax.broadcasted_iota(jnp.int32, (T, CONV_TN), 0)
    z_prev = jnp.where(rows == 0, zp, pltpu.roll(z, 1, 0))
    z_next = jnp.where(rows == T - 1, zn, pltpu.roll(z, T - 1, 0))
    y = w_ref[0:1, :] * z_prev + w_ref[1:2, :] * z + w_ref[2:3, :] * z_next
    o_ref[...] = (bg_ref[...].astype(F32) * y).astype(o_ref.dtype)


def _conv(ux, conv_w):
    xt = L // T
    nx = B * xt
    c0 = C_IN // CONV_TN
    nc = D_CH // CONV_TN
    rb = T // SUBLANE

    def main(cb):
        return pl.BlockSpec((T, CONV_TN), lambda i, j: (i, c0 + cb * nc + j))

    def prev(cb):
        return pl.BlockSpec((SUBLANE, CONV_TN), lambda i, j: (jnp.maximum(i * rb - 1, 0), c0 + cb * nc + j))

    def nxt(cb):
        return pl.BlockSpec((SUBLANE, CONV_TN),
                            lambda i, j: (jnp.minimum((i + 1) * rb, nx * rb - 1), c0 + cb * nc + j))

    return pl.pallas_call(
        _conv_kernel,
        grid=(nx, nc),
        in_specs=[main(0), main(1), main(2), prev(1), prev(2), nxt(1), nxt(2),
                  pl.BlockSpec((D_CONV, CONV_TN), lambda i, j: (0, j))],
        out_specs=pl.BlockSpec((T, CONV_TN), lambda i, j: (i, j)),
        out_shape=jax.ShapeDtypeStruct((B * L, D_CH), BF16),
        compiler_params=_cp("parallel", "parallel"),
    )(ux, ux, ux, ux, ux, ux, ux, conv_w)


def _router_kernel(h_ref, wr_ref, b_ref, i_ref, g_ref):
    tm = h_ref.shape[0]
    scores = _sigmoid(_dot_nt(wr_ref[...], h_ref[...]))
    biased = scores + b_ref[...]
    ng = N_GROUPS
    sl = [biased[j * ng:(j + 1) * ng] for j in range(GROUP_SIZE)]
    sc = [scores[j * ng:(j + 1) * ng] for j in range(GROUP_SIZE)]
    m1 = sl[0]
    m2 = jnp.full_like(m1, -jnp.inf)
    for v in sl[1:]:
        m2 = jnp.maximum(m2, jnp.minimum(m1, v))
        m1 = jnp.maximum(m1, v)
    gs = m1 + m2
    gidx = lax.broadcasted_iota(jnp.int32, (ng, tm), 0)
    rank = jnp.zeros((ng, tm), jnp.int32)
    for k in range(1, ng):
        r = pltpu.roll(gs, k, 0)
        rank += jnp.where((r > gs) | ((r == gs) & (gidx >= k)), 1, 0)
    keep = rank < TOPK_GROUPS
    ms = [jnp.where(keep, v, -jnp.inf) for v in sl]
    rolled = [[ms[j] if k == 0 else pltpu.roll(ms[j], k, 0) for k in range(ng)] for j in range(GROUP_SIZE)]
    tot = jnp.zeros((ng, tm), F32)
    picked = []
    ranks = []
    for j in range(GROUP_SIZE):
        cnt = jnp.zeros((ng, tm), jnp.int32)
        for jp in range(GROUP_SIZE):
            for k in range(ng):
                r = rolled[jp][k]
                if k == 0:
                    if jp == j:
                        continue
                    beats = (r >= ms[j]) if jp < j else (r > ms[j])
                else:
                    beats = (r > ms[j]) | ((r == ms[j]) & (gidx >= k))
                cnt += jnp.where(beats, 1, 0)
        w = jnp.where(cnt < TOP_K, sc[j], 0.0)
        picked.append(w)
        ranks.append(cnt)
        tot = tot + w
    tot = jnp.sum(tot, axis=0, keepdims=True)
    wn = [p / tot * ROUTE_SCALE for p in picked]
    eid = [(gidx * GROUP_SIZE + j).astype(F32) for j in range(GROUP_SIZE)]
    for k in range(TOP_K):
        ik = jnp.zeros((ng, tm), F32)
        wk = jnp.zeros((ng, tm), F32)
        for j in range(GROUP_SIZE):
            hit = ranks[j] == k
            ik += jnp.where(hit, eid[j], 0.0)
            wk += jnp.where(hit, wn[j], 0.0)
        i_ref[k:k + 1, :] = jnp.sum(ik, axis=0, keepdims=True).astype(jnp.int32)
        g_ref[k:k + 1, :] = jnp.sum(wk, axis=0, keepdims=True)


def _router(h, wr_t, bias, nrows):
    out = pl.BlockSpec((TOP_K, TM), lambda i: (0, i))
    return pl.pallas_call(
        _router_kernel,
        grid=(nrows // TM,),
        in_specs=[pl.BlockSpec((TM, D), lambda i: (i, 0)),
                  pl.BlockSpec((N_EXPERTS, D), lambda i: (0, 0)),
                  pl.BlockSpec((N_EXPERTS, 1), lambda i: (0, 0))],
        out_specs=[out, out],
        out_shape=[jax.ShapeDtypeStruct((TOP_K, nrows), jnp.int32),
                   jax.ShapeDtypeStruct((TOP_K, nrows), F32)],
        compiler_params=_cp("parallel"),
    )(h, wr_t, bias)


def _router_order():
    r = np.arange(N_EXPERTS)
    return (r % N_GROUPS) * GROUP_SIZE + r // N_GROUPS


def _glu_kernel(x_ref, wg_ref, wu_ref, wd_ref, o_ref):
    x = x_ref[...]
    hid = _silu(_dot(x, wg_ref[...])) * _dot(x, wu_ref[...])
    o_ref[...] = _dot(hid.astype(BF16), wd_ref[...]).astype(o_ref.dtype)


def _glu(h, wg, wu, wd, nrows):
    return pl.pallas_call(
        _glu_kernel,
        grid=(nrows // TM,),
        in_specs=[pl.BlockSpec((TM, D), lambda i: (i, 0)),
                  pl.BlockSpec((D, D_EXPERT), lambda i: (0, 0)),
                  pl.BlockSpec((D, D_EXPERT), lambda i: (0, 0)),
                  pl.BlockSpec((D_EXPERT, D), lambda i: (0, 0))],
        out_specs=pl.BlockSpec((TM, D), lambda i: (i, 0)),
        out_shape=jax.ShapeDtypeStruct((nrows, D), BF16),
        compiler_params=_cp("parallel"),
    )(h, wg, wu, wd)


TME = 256
TD = 512
TC = 128


def _route_plan(idx, nrows):
    nt = (nrows * TOP_K) // TME + N_EXPERTS
    onehot = idx[:, :, None] == jnp.arange(N_EXPERTS, dtype=jnp.int32)[None, None, :]
    cs = jnp.cumsum(jnp.any(onehot, axis=1).astype(jnp.int32), axis=0)
    counts = cs[-1]
    padded = ((counts + TME - 1) // TME) * TME
    ends = jnp.cumsum(padded)
    off = ends - padded
    dest = (off[idx] + jnp.take_along_axis(cs, idx, axis=1) - 1).astype(jnp.int32)
    tiles = jnp.arange(nt, dtype=jnp.int32)
    used = ends[-1] // TME
    t_expert = jnp.sum((ends // TME)[None, :] <= tiles[:, None], axis=1)
    t_expert = jnp.minimum(t_expert, N_EXPERTS - 1).astype(jnp.int32)
    t_valid = (tiles < used).astype(jnp.int32)
    t_first = ((tiles * TME == off[t_expert]) & (tiles < used)).astype(jnp.int32)
    t_src = jnp.minimum(tiles, used - 1).astype(jnp.int32)
    return dict(nt=nt, dest=dest, used=used.reshape(1).astype(jnp.int32),
                pad_start=(off + counts).astype(jnp.int32),
                pad_cnt=(padded - counts).astype(jnp.int32),
                t_expert=t_expert, t_valid=t_valid, t_first=t_first, t_src=t_src)


def _slab(ref, row):
    return ref.at[pl.ds(pl.multiple_of(row * NP, NP), NP), :]


def _dispatch_kernel(ps_ref, pc_ref, used_ref, dest_ref, hp_ref, xs_ref, sem, *, nt):
    i = pl.program_id(0)

    def body(j, c):
        for k in range(TOP_K):
            pltpu.make_async_copy(_slab(hp_ref, j), _slab(xs_ref, dest_ref[0, 0, j * TOP_K + k]), sem).start()
        return c

    lax.fori_loop(0, TD, body, 0)
    for k in range(TOP_K):
        pltpu.make_async_copy(hp_ref, xs_ref.at[pl.ds(0, TD * NP), :], sem).wait()

    @pl.when(i == 0)
    def _():
        def pad_copy(e):
            n = pc_ref[e] * NP
            return pltpu.make_async_copy(hp_ref.at[pl.ds(0, n), :],
                                         xs_ref.at[pl.ds(pl.multiple_of(ps_ref[e] * NP, NP), n), :], sem)

        def tail_copy(t):
            n = TME * NP
            return pltpu.make_async_copy(hp_ref.at[pl.ds(0, n), :],
                                         xs_ref.at[pl.ds(pl.multiple_of(t * n, n), n), :], sem)

        def over(lo, hi, copy, pred):
            def start(e, c):
                @pl.when(pred(e))
                def _():
                    copy(e).start()
                return c

            def wait(e, c):
                @pl.when(pred(e))
                def _():
                    copy(e).wait()
                return c

            lax.fori_loop(lo, hi, start, 0)
            lax.fori_loop(lo, hi, wait, 0)

        over(0, N_EXPERTS, pad_copy, lambda e: pc_ref[e] > 0)
        over(0, nt, tail_copy, lambda t: t >= used_ref[0])


def _dispatch(hp, plan, nrows):
    dest = plan["dest"].reshape(nrows // TD, 1, TD * TOP_K)
    return pl.pallas_call(
        functools.partial(_dispatch_kernel, nt=plan["nt"]),
        grid_spec=pltpu.PrefetchScalarGridSpec(
            num_scalar_prefetch=3,
            grid=(nrows // TD,),
            in_specs=[pl.BlockSpec((1, 1, TD * TOP_K), lambda i, ps, pc, us: (i, 0, 0), memory_space=pltpu.SMEM),
                      pl.BlockSpec((TD * NP, LANE), lambda i, ps, pc, us: (i, 0))],
            out_specs=pl.BlockSpec(memory_space=pl.ANY),
            scratch_shapes=[pltpu.SemaphoreType.DMA(())]),
        out_shape=jax.ShapeDtypeStruct((plan["nt"] * TME * NP, LANE), jnp.uint32),
        compiler_params=pltpu.CompilerParams(dimension_semantics=("arbitrary",), disable_bounds_checks=True,
                                             vmem_limit_bytes=VMEM_LIMIT_BYTES),
    )(plan["pad_start"], plan["pad_cnt"], plan["used"], dest, hp)


def _expert_kernel(te_ref, tv_ref, tf_ref, ts_ref, x_ref, wg_ref, wu_ref, wd_ref, o_ref, wgu_b, wd_b):
    i = pl.program_id(0)

    @pl.when(tv_ref[i] == 0)
    def _():
        o_ref[...] = jnp.zeros_like(o_ref)

    @pl.when(tv_ref[i] == 1)
    def _():
        @pl.when(tf_ref[i] == 1)
        def _():
            wgu_b[:, :D_EXPERT] = wg_ref[0, 0].astype(BF16)
            wgu_b[:, D_EXPERT:] = wu_ref[0, 0].astype(BF16)
            wd_b[...] = wd_ref[0, 0].astype(BF16)

        pieces = []
        for p in range(NP):
            lo, hi = _unpack2(x_ref[pl.ds(p, TME, stride=NP), :])
            pieces += [lo.astype(BF16), hi.astype(BF16)]
        a = _dot(jnp.concatenate(pieces, axis=1), wgu_b[...])
        hid = _silu(a[:, :D_EXPERT]) * a[:, D_EXPERT:]
        y = _dot(hid.astype(BF16), wd_b[...])
        for p in range(NP):
            o_ref[pl.ds(p, TME, stride=NP), :] = _pack2(y[:, p * PACK:p * PACK + LANE],
                                                        y[:, p * PACK + LANE:(p + 1) * PACK])


def _experts(xs, plan, wg, wu, wd, layer):
    nt = plan["nt"]
    rows = pl.BlockSpec((TME * NP, LANE), lambda i, te, tv, tf, ts: (i, 0))
    return pl.pallas_call(
        _expert_kernel,
        grid_spec=pltpu.PrefetchScalarGridSpec(
            num_scalar_prefetch=4,
            grid=(nt,),
            in_specs=[pl.BlockSpec((TME * NP, LANE), lambda i, te, tv, tf, ts: (ts[i], 0)),
                      pl.BlockSpec((1, 1, D, D_EXPERT), lambda i, te, tv, tf, ts: (layer, te[i], 0, 0)),
                      pl.BlockSpec((1, 1, D, D_EXPERT), lambda i, te, tv, tf, ts: (layer, te[i], 0, 0)),
                      pl.BlockSpec((1, 1, D_EXPERT, D), lambda i, te, tv, tf, ts: (layer, te[i], 0, 0))],
            out_specs=rows,
            scratch_shapes=[pltpu.VMEM((D, 2 * D_EXPERT), BF16), pltpu.VMEM((D_EXPERT, D), BF16)]),
        out_shape=jax.ShapeDtypeStruct((nt * TME * NP, LANE), jnp.uint32),
        compiler_params=pltpu.CompilerParams(dimension_semantics=("arbitrary",),
                                             vmem_limit_bytes=EXPERT_VMEM_LIMIT_BYTES),
    )(plan["t_expert"], plan["t_valid"], plan["t_first"], plan["t_src"], xs, wg, wu, wd)


def _combine_kernel(dcur_ref, dnxt_ref, w_ref, r_ref, sh_ref, g_ref, ys_ref, o_ref, buf0, buf1, sem):
    i = pl.program_id(0)
    last = pl.num_programs(0) - 1

    def issue(d_ref, buf, slot):
        def body(j, c):
            for k in range(TOP_K):
                pltpu.make_async_copy(_slab(ys_ref, d_ref[0, 0, j * TOP_K + k]), _slab(buf, k * TC + j),
                                      sem.at[slot]).start()
            return c

        lax.fori_loop(0, TC, body, 0)

    def drain(buf, slot):
        pltpu.make_async_copy(ys_ref.at[pl.ds(0, TC * TOP_K * NP), :], buf, sem.at[slot]).wait()

    def reduce(buf):
        gates = [w_ref[:, k:k + 1] for k in range(TOP_K)]
        gmod = g_ref[0]
        for p in range(NP):
            lo_sl = slice(p * PACK, p * PACK + LANE)
            hi_sl = slice(p * PACK + LANE, (p + 1) * PACK)
            lo_acc = sh_ref[:, lo_sl].astype(F32)
            hi_acc = sh_ref[:, hi_sl].astype(F32)
            for k in range(TOP_K):
                lo, hi = _unpack2(buf[pl.ds(k * TC * NP + p, TC, stride=NP), :])
                lo_acc += gates[k] * lo
                hi_acc += gates[k] * hi
            o_ref[:, lo_sl] = r_ref[:, lo_sl] + gmod[:, lo_sl] * lo_acc
            o_ref[:, hi_sl] = r_ref[:, hi_sl] + gmod[:, hi_sl] * hi_acc

    @pl.when(i == 0)
    def _():
        issue(dcur_ref, buf0, 0)

    @pl.when(i % 2 == 0)
    def _():
        @pl.when(i < last)
        def _():
            issue(dnxt_ref, buf1, 1)

        drain(buf0, 0)
        reduce(buf0)

    @pl.when(i % 2 == 1)
    def _():
        @pl.when(i < last)
        def _():
            issue(dnxt_ref, buf0, 0)

        drain(buf1, 1)
        reduce(buf1)


def _combine(resid, ys, shared, w, plan, gate, nrows):
    nsteps = nrows // TC
    dest = plan["dest"].reshape(nsteps, 1, TC * TOP_K)
    blk = pl.BlockSpec((TC, D), lambda i: (i, 0))
    return pl.pallas_call(
        _combine_kernel,
        grid=(nsteps,),
        in_specs=[pl.BlockSpec((1, 1, TC * TOP_K), lambda i: (i, 0, 0), memory_space=pltpu.SMEM),
                  pl.BlockSpec((1, 1, TC * TOP_K), lambda i: (jnp.minimum(i + 1, nsteps - 1), 0, 0),
                               memory_space=pltpu.SMEM),
                  pl.BlockSpec((TC, TOP_K), lambda i: (i, 0)),
                  blk, blk,
                  pl.BlockSpec((1, 1, D), lambda i: (_mod_row(i, TC), 0, 0)),
                  pl.BlockSpec(memory_space=pl.ANY)],
        out_specs=blk,
        out_shape=jax.ShapeDtypeStruct((nrows, D), F32),
        scratch_shapes=[pltpu.VMEM((TC * TOP_K * NP, LANE), jnp.uint32),
                        pltpu.VMEM((TC * TOP_K * NP, LANE), jnp.uint32),
                        pltpu.SemaphoreType.DMA((2,))],
        compiler_params=pltpu.CompilerParams(dimension_semantics=("arbitrary",),
                                             vmem_limit_bytes=VMEM_LIMIT_BYTES, disable_bounds_checks=True),
    )(dest, dest, w, resid, shared, gate, ys)


def _moe(s, h, hp, gate, nrows, layer, router_w, router_bias, wg, wu, wd, sg, su, sd):
    order = _router_order()
    wr_t = router_w.T[order].astype(BF16)
    bias = router_bias[order].reshape(N_EXPERTS, 1).astype(F32)
    idx_t, w_t = _router(h, wr_t, bias, nrows)
    plan = _route_plan(idx_t.T, nrows)
    xs = _dispatch(hp, plan, nrows)
    ys = _experts(xs, plan, wg, wu, wd, layer)
    shared = _glu(h, sg.astype(BF16), su.astype(BF16), sd.astype(BF16), nrows)
    return _combine(s, ys, shared, w_t.T, plan, gate, nrows)


def _mla_in_weights(w_in):
    wb = w_in[:, A_IN:]
    kr = wb[:, B_Q_RANK + B_KV_RANK:][:, _rope_perm(B_ROPE)]
    pad = jnp.zeros((D, LANE - B_ROPE), w_in.dtype)
    return jnp.concatenate([wb[:, :B_Q_RANK + B_KV_RANK], kr, pad], axis=1).astype(BF16)


def _mla_uq_weights(w_uq):
    scale = (B_NOPE + B_ROPE) ** -0.5 * np.log2(np.e)
    w = (w_uq * scale).reshape(B_Q_RANK, B_HEADS, B_NOPE + B_ROPE)
    rope = w[:, :, B_NOPE:][:, :, _rope_perm(B_ROPE)]
    pad = jnp.zeros((B_Q_RANK, B_HEADS, B_QPAD - B_NOPE - B_ROPE), w.dtype)
    return jnp.concatenate([w[:, :, :B_NOPE], rope, pad], axis=-1).reshape(B_Q_RANK, B_HEADS * B_QPAD).astype(BF16)


def kernel(x, c, ctx, c_ctx, ada_w, ada_b, norm_mix_w, norm_ffn_w, ab_w_in, ab_w_out, hgrn_lb_logits, hgrn_gnorm_w, mla_q_norm_w, mla_w_uq, mla_kv_norm_w, mla_w_ukv, cd_w_in, cd_w_out, gqa_sink, conv_w, router_w, router_bias, exp_w_gate, exp_w_up, exp_w_down, sh_w_gate, sh_w_up, sh_w_down, final_norm_w):
    nx = B * L
    n = nx + B * LC
    s = jnp.concatenate([x.reshape(nx, D), ctx.reshape(B * LC, D)], axis=0)
    lb_all = jnp.cumsum(jax.nn.softmax(hgrn_lb_logits.astype(F32), axis=0), axis=0)
    cond = jnp.concatenate([c, c_ctx[None, :], jnp.zeros((8 - B - 1, D), F32)], axis=0)

    def modulation(i):
        m = _ada(cond, ada_w, ada_b[i].reshape(1, N_MOD * D), i).reshape(8, N_MOD, D)
        return [m[:B + 1, k, :].reshape(B + 1, 1, D) for k in range(N_MOD)]

    def moe_args(i):
        return (i, router_w[i], router_bias[i], exp_w_gate, exp_w_up, exp_w_down,
                sh_w_gate[i], sh_w_up[i], sh_w_down[i])

    mod = modulation(0)
    h = _norm_mod(s, norm_mix_w[0], mod[1], mod[0], n)
    ua = _matmul(h, ab_w_in[0][:, :A_IN].astype(BF16), 1024)
    ub = _matmul(h, _mla_in_weights(ab_w_in[0]), B_Q_RANK + B_KV_RANK + LANE)
    lb = lb_all[0].reshape(2, A_HEADS, 1, A_DK)
    o_f, o_b = _hgrn(ua, lb[0], lb[1])
    ya = _hgrn_out(o_f, o_b, ua, hgrn_gnorm_w[0])
    tabs = _rope_tables(B_ROPE, True)
    q = _norm_mm(ub, 0, B_Q_RANK, mla_q_norm_w[0], _mla_uq_weights(mla_w_uq[0]), 1024)
    kv = _norm_mm(ub, B_Q_RANK // B_KV_RANK, B_KV_RANK, mla_kv_norm_w[0], mla_w_ukv[0].astype(BF16), 1024)
    kr = _krope(ub, tabs)
    yb = _mla_attn(q, kv, kr, tuple(t[T:] for t in tabs))
    s = _mm2_res(ya, yb, ab_w_out[0].astype(BF16), s, mod[2], n)
    h, hp = _norm_mod_pack(s, norm_ffn_w[0], mod[4], mod[3], n)
    s = _moe(s, h, hp, mod[5], n, *moe_args(0))

    mod = modulation(1)
    h = _norm_mod(s, norm_mix_w[1], mod[1], mod[0], n)
    ux = _matmul(h, cd_w_in[0].astype(BF16), 1024)
    qk = _rope_qk(ux)
    att = _win_attn(qk, ux, gqa_sink[0].astype(F32))
    cv = _conv(ux, conv_w[0])
    s = _mm2_res(att, cv, cd_w_out[0].astype(BF16), s, mod[2], nx)
    h, hp = _norm_mod_pack(s, norm_ffn_w[1], mod[4], mod[3], nx)
    s = _moe(s, h, hp, mod[5], nx, *moe_args(1))
    return _rms(s, final_norm_w).reshape(B, L, D)
```

```python
import functools

import jax
import jax.numpy as jnp
import numpy as np
from jax import lax
from jax.experimental import pallas as pl
from jax.experimental.pallas import tpu as pltpu

F32 = jnp.float32
BF16 = jnp.bfloat16

D = 4096
B = 2
L = 8192
LC = 256
GRID_W = 64
RMS_EPS = 1e-6
ROPE_THETA = 10000.0
N_MOD = 6

A_HEADS = 16
A_DK = 128
A_KEY = A_HEADS * A_DK
A_CHUNK = 32
A_IN = 5 * A_KEY

B_HEADS = 16
B_Q_RANK = 1024
B_KV_RANK = 512
B_NOPE = 128
B_ROPE = 64
B_VDIM = 128
B_QPAD = 256

C_HEADS = 16
C_KV_HEADS = 4
C_GROUP = C_HEADS // C_KV_HEADS
C_WINDOW = 128
HEAD_DIM = 128
C_Q = C_HEADS * HEAD_DIM
C_KV = C_KV_HEADS * HEAD_DIM
C_IN = C_Q + 2 * C_KV
D_CH = 2048
D_CONV = 3

N_EXPERTS = 64
TOP_K = 8
N_GROUPS = 8
TOPK_GROUPS = 4
GROUP_SIZE = N_EXPERTS // N_GROUPS
D_EXPERT = 256
ROUTE_SCALE = 2.5

T = 256
TM = 512
LANE = 128
VMEM_LIMIT_BYTES = 48 * 1024 * 1024
EXPERT_VMEM_LIMIT_BYTES = 56 * 1024 * 1024
NEG_BIG = -1e30
PACK = 2 * LANE
NP = D // PACK


def _cp(*sem):
    return pltpu.CompilerParams(dimension_semantics=sem, vmem_limit_bytes=VMEM_LIMIT_BYTES)


def _sigmoid(x):
    return 1.0 / (1.0 + jnp.exp(-x))


def _silu(x):
    return x * _sigmoid(x)


def _dot(a, b):
    return jnp.dot(a, b, preferred_element_type=F32)


def _dot_nt(a, b):
    return lax.dot_general(a, b, (((1,), (1,)), ((), ())), preferred_element_type=F32)


def _mod_row(i, tile):
    return jnp.where(i < (B * L) // tile, i // (L // tile), B)


def _ada_kernel(a_ref, w_ref, b_ref, o_ref):
    a = _silu(a_ref[...]).astype(BF16)
    o_ref[...] = _dot(a, w_ref[0].astype(BF16)) + b_ref[...]


def _ada(a, w, bias, layer):
    n = w.shape[2]
    tn = 512
    return pl.pallas_call(
        _ada_kernel,
        grid=(n // tn,),
        in_specs=[pl.BlockSpec((8, D), lambda j: (0, 0)),
                  pl.BlockSpec((1, D, tn), lambda j: (layer, 0, j)),
                  pl.BlockSpec((1, tn), lambda j: (0, j))],
        out_specs=pl.BlockSpec((8, tn), lambda j: (0, j)),
        out_shape=jax.ShapeDtypeStruct((8, n), F32),
        compiler_params=_cp("parallel"),
    )(a, w, bias)


def _norm_mod_kernel(s_ref, w_ref, sc_ref, sh_ref, o_ref):
    x = s_ref[...]
    y = x * lax.rsqrt(jnp.mean(x * x, axis=-1, keepdims=True) + RMS_EPS) * w_ref[...]
    o_ref[...] = (y * (1.0 + sc_ref[0]) + sh_ref[0]).astype(o_ref.dtype)


def _norm_mod(s, w, scale, shift, nrows):
    mod_spec = pl.BlockSpec((1, 1, D), lambda i: (_mod_row(i, T), 0, 0))
    return pl.pallas_call(
        _norm_mod_kernel,
        grid=(nrows // T,),
        in_specs=[pl.BlockSpec((T, D), lambda i: (i, 0)),
                  pl.BlockSpec((1, D), lambda i: (0, 0)),
                  mod_spec, mod_spec],
        out_specs=pl.BlockSpec((T, D), lambda i: (i, 0)),
        out_shape=jax.ShapeDtypeStruct((nrows, D), BF16),
        compiler_params=_cp("parallel"),
    )(s, w.reshape(1, D), scale, shift)


def _pack2(lo, hi):
    lo_b = lax.bitcast_convert_type(lo.astype(BF16).astype(F32), jnp.uint32) >> 16
    hi_b = lax.bitcast_convert_type(hi.astype(BF16).astype(F32), jnp.uint32)
    return hi_b | lo_b


def _unpack2(w):
    lo = lax.bitcast_convert_type(w << 16, F32)
    hi = lax.bitcast_convert_type(w & jnp.uint32(0xFFFF0000), F32)
    return lo, hi


def _norm_mod_pack_kernel(s_ref, w_ref, sc_ref, sh_ref, o_ref, p_ref):
    x = s_ref[...]
    y = x * lax.rsqrt(jnp.mean(x * x, axis=-1, keepdims=True) + RMS_EPS) * w_ref[...]
    y = y * (1.0 + sc_ref[0]) + sh_ref[0]
    o_ref[...] = y.astype(o_ref.dtype)
    for p in range(NP):
        p_ref[pl.ds(p, T, stride=NP), :] = _pack2(y[:, p * PACK:p * PACK + LANE], y[:, p * PACK + LANE:(p + 1) * PACK])


def _norm_mod_pack(s, w, scale, shift, nrows):
    mod_spec = pl.BlockSpec((1, 1, D), lambda i: (_mod_row(i, T), 0, 0))
    return pl.pallas_call(
        _norm_mod_pack_kernel,
        grid=(nrows // T,),
        in_specs=[pl.BlockSpec((T, D), lambda i: (i, 0)),
                  pl.BlockSpec((1, D), lambda i: (0, 0)),
                  mod_spec, mod_spec],
        out_specs=[pl.BlockSpec((T, D), lambda i: (i, 0)),
                   pl.BlockSpec((T * NP, LANE), lambda i: (i, 0))],
        out_shape=[jax.ShapeDtypeStruct((nrows, D), BF16),
                   jax.ShapeDtypeStruct((nrows * NP, LANE), jnp.uint32)],
        compiler_params=_cp("parallel"),
    )(s, w.reshape(1, D), scale, shift)


def _rms_kernel(s_ref, w_ref, o_ref):
    x = s_ref[...]
    o_ref[...] = x * lax.rsqrt(jnp.mean(x * x, axis=-1, keepdims=True) + RMS_EPS) * w_ref[...]


def _rms(s, w):
    nrows = s.shape[0]
    return pl.pallas_call(
        _rms_kernel,
        grid=(nrows // T,),
        in_specs=[pl.BlockSpec((T, D), lambda i: (i, 0)),
                  pl.BlockSpec((1, D), lambda i: (0, 0))],
        out_specs=pl.BlockSpec((T, D), lambda i: (i, 0)),
        out_shape=jax.ShapeDtypeStruct((nrows, D), F32),
        compiler_params=_cp("parallel"),
    )(s, w.reshape(1, D))


def _mm_kernel(a_ref, w_ref, o_ref):
    o_ref[...] = _dot(a_ref[...], w_ref[...]).astype(o_ref.dtype)


def _matmul(a, w, tn):
    m, k = a.shape
    n = w.shape[1]
    return pl.pallas_call(
        _mm_kernel,
        grid=(m // TM, n // tn),
        in_specs=[pl.BlockSpec((TM, k), lambda i, j: (i, 0)),
                  pl.BlockSpec((k, tn), lambda i, j: (0, j))],
        out_specs=pl.BlockSpec((TM, tn), lambda i, j: (i, j)),
        out_shape=jax.ShapeDtypeStruct((m, n), BF16),
        compiler_params=_cp("parallel", "arbitrary"),
    )(a, w)


def _norm_mm_kernel(a_ref, nw_ref, w_ref, o_ref):
    x = a_ref[...].astype(F32)
    y = x * lax.rsqrt(jnp.mean(x * x, axis=-1, keepdims=True) + RMS_EPS) * nw_ref[...]
    o_ref[...] = _dot(y.astype(BF16), w_ref[...]).astype(o_ref.dtype)


def _norm_mm(u, acol, kdim, nw, w, tn):
    m = u.shape[0]
    n = w.shape[1]
    return pl.pallas_call(
        _norm_mm_kernel,
        grid=(m // TM, n // tn),
        in_specs=[pl.BlockSpec((TM, kdim), lambda i, j: (i, acol)),
                  pl.BlockSpec((1, kdim), lambda i, j: (0, 0)),
                  pl.BlockSpec((kdim, tn), lambda i, j: (0, j))],
        out_specs=pl.BlockSpec((TM, tn), lambda i, j: (i, j)),
        out_shape=jax.ShapeDtypeStruct((m, n), BF16),
        compiler_params=_cp("parallel", "arbitrary"),
    )(u, nw.reshape(1, kdim), w)


def _mm2_res_kernel(a1_ref, a2_ref, w1_ref, w2_ref, r_ref, g_ref, o_ref):
    acc = _dot(a1_ref[...], w1_ref[...]) + _dot(a2_ref[...], w2_ref[...])
    o_ref[...] = r_ref[...] + g_ref[0] * acc


def _mm2_res(a1, a2, w, resid, gate, nrows):
    k1 = a1.shape[1]
    k2 = a2.shape[1]
    assert k1 == k2 and w.shape == (k1 + k2, D)
    tn = 512
    return pl.pallas_call(
        _mm2_res_kernel,
        grid=(nrows // TM, D // tn),
        in_specs=[pl.BlockSpec((TM, k1), lambda i, j: (i, 0)),
                  pl.BlockSpec((TM, k2), lambda i, j: (i, 0)),
                  pl.BlockSpec((k1, tn), lambda i, j: (0, j)),
                  pl.BlockSpec((k2, tn), lambda i, j: (1, j)),
                  pl.BlockSpec((TM, tn), lambda i, j: (i, j)),
                  pl.BlockSpec((1, 1, tn), lambda i, j: (_mod_row(i, TM), 0, j))],
        out_specs=pl.BlockSpec((TM, tn), lambda i, j: (i, j)),
        out_shape=jax.ShapeDtypeStruct((nrows, D), F32),
        compiler_params=_cp("parallel", "arbitrary"),
    )(a1, a2, w, w, resid, gate)


def _split3(x):
    hi = x.astype(BF16)
    r = x - hi.astype(F32)
    mid = r.astype(BF16)
    lo = (r - mid.astype(F32)).astype(BF16)
    return hi, mid, lo


def _hgrn_dir(q_raw, f_raw, v, lb, st_ref, o_ref, rev):
    c = A_CHUNK
    q = _silu(q_raw.astype(F32)) * (A_DK ** -0.5)
    f = lb + (1.0 - lb) * _sigmoid(f_raw.astype(F32))
    k = 1.0 - f
    logf = jnp.log(f)
    row = lax.broadcasted_iota(jnp.int32, (T, T), 0)
    col = lax.broadcasted_iota(jnp.int32, (T, T), 1)
    mask = ((row ^ col) < c) & ((col >= row) if rev else (col <= row))
    tri = jnp.where(mask, 1.0, 0.0).astype(BF16)
    hi, mid, lo = _split3(logf)
    cum = _dot(tri, hi) + _dot(tri, mid) + _dot(tri, lo)
    nchunk = T // c
    cum3 = cum.reshape(nchunk, c, LANE)
    ref = cum3[:, c // 2 - 1:c // 2] if rev else cum3[:, c // 2:c // 2 + 1]
    last = cum3[:, 0:1] if rev else cum3[:, c - 1:c]
    q3 = q.reshape(nchunk, c, LANE)
    k3 = k.reshape(nchunk, c, LANE)
    qs = (q3 * jnp.exp(cum3 - ref)).reshape(T, LANE).astype(BF16)
    ks = (k3 * jnp.exp(ref - cum3)).reshape(T, LANE).astype(BF16)
    qd = (q * jnp.exp(cum)).astype(BF16)
    kd = (k3 * jnp.exp(last - cum3)).reshape(T, LANE).astype(BF16)
    decay = jnp.exp(last)
    att = jnp.where(mask, _dot_nt(qs, ks), 0.0)
    o = _dot(att.astype(BF16), v)
    kv = [lax.dot_general(v[n * c:(n + 1) * c], kd[n * c:(n + 1) * c], (((0,), (0,)), ((), ())),
                          preferred_element_type=F32) for n in range(nchunk)]
    st = st_ref[...]
    o_inter = [None] * nchunk
    for n in (range(nchunk - 1, -1, -1) if rev else range(nchunk)):
        o_inter[n] = _dot_nt(qd[n * c:(n + 1) * c], st.astype(BF16))
        st = st * decay[n] + kv[n]
    st_ref[...] = st
    o_ref[...] = (o + jnp.concatenate(o_inter, axis=0)).astype(o_ref.dtype)


def _hgrn_kernel(qf, ff, vf, qb, fb, vb, lbf, lbb, of, ob, stf, stb):
    @pl.when(pl.program_id(2) == 0)
    def _():
        stf[...] = jnp.zeros_like(stf)
        stb[...] = jnp.zeros_like(stb)

    _hgrn_dir(qf[...], ff[...], vf[...], lbf[0], stf, of, False)
    _hgrn_dir(qb[...], fb[...], vb[...], lbb[0], stb, ob, True)


def _hgrn(u, lbf, lbb):
    assert LC == T
    xt = L // T
    n_rows = u.shape[0]
    hh = A_HEADS

    def blk_f(b, n):
        return jnp.where(n == 0, B * xt + b, b * xt + n - 1)

    def blk_b(b, n):
        return jnp.where(n == 0, B * xt + b, b * xt + xt - n)

    def spec(blk, cb):
        return pl.BlockSpec((T, LANE), lambda b, h, n: (blk(b, n), cb * hh + h))

    lb_spec = pl.BlockSpec((1, 1, LANE), lambda b, h, n: (h, 0, 0))
    return pl.pallas_call(
        _hgrn_kernel,
        grid=(B, hh, xt + 1),
        in_specs=[spec(blk_f, 0), spec(blk_f, 1), spec(blk_f, 3),
                  spec(blk_b, 0), spec(blk_b, 2), spec(blk_b, 3),
                  lb_spec, lb_spec],
        out_specs=[spec(blk_f, 0), spec(blk_b, 0)],
        out_shape=[jax.ShapeDtypeStruct((n_rows, A_KEY), BF16)] * 2,
        scratch_shapes=[pltpu.VMEM((LANE, LANE), F32)] * 2,
        compiler_params=_cp("parallel", "parallel", "arbitrary"),
    )(u, u, u, u, u, u, lbf, lbb)


def _hgrn_out_kernel(of_ref, ob_ref, g_ref, w_ref, y_ref):
    for h in range(A_HEADS):
        sl = slice(h * LANE, (h + 1) * LANE)
        o = of_ref[:, sl].astype(F32) + ob_ref[:, sl].astype(F32)
        o = o * lax.rsqrt(jnp.mean(o * o, axis=-1, keepdims=True) + RMS_EPS) * w_ref[...]
        y_ref[:, sl] = (o * _silu(g_ref[:, sl].astype(F32))).astype(y_ref.dtype)


def _hgrn_out(o_f, o_b, u, gnorm_w):
    n_rows = o_f.shape[0]
    blk = pl.BlockSpec((T, A_KEY), lambda i: (i, 0))
    return pl.pallas_call(
        _hgrn_out_kernel,
        grid=(n_rows // T,),
        in_specs=[blk, blk,
                  pl.BlockSpec((T, A_KEY), lambda i: (i, 4)),
                  pl.BlockSpec((1, LANE), lambda i: (0, 0))],
        out_specs=blk,
        out_shape=jax.ShapeDtypeStruct((n_rows, A_KEY), BF16),
        compiler_params=_cp("parallel"),
    )(o_f, o_b, u, gnorm_w.reshape(1, LANE))


def _rope_perm(n):
    return np.concatenate([np.arange(0, n, 2), np.arange(1, n, 2)])


def _rope_cos_sin(rot_dim):
    rows = L // GRID_W
    row = jnp.repeat(jnp.arange(rows), GRID_W).astype(F32)
    col = jnp.tile(jnp.arange(GRID_W), rows).astype(F32)
    n_freq = rot_dim // 4
    inv = ROPE_THETA ** (-jnp.arange(n_freq, dtype=F32) / n_freq)
    ang = jnp.concatenate([row[:, None] * inv, col[:, None] * inv], axis=-1)
    return jnp.cos(ang), jnp.sin(ang)


def _rope_tables_pairs(scale):
    cos, sin = _rope_cos_sin(LANE)
    zero = jnp.zeros_like(sin)
    c = jnp.repeat(cos, 2, axis=-1) * scale
    s1 = jnp.stack([-sin, zero], axis=-1).reshape(L, LANE) * scale
    s2 = jnp.stack([zero, sin], axis=-1).reshape(L, LANE) * scale
    return c, s1, s2


def _rope_tables(rot_dim, with_ctx):
    half = rot_dim // 2
    cos, sin = _rope_cos_sin(rot_dim)
    zpad = jnp.zeros((L, LANE - rot_dim), F32)
    zh = jnp.zeros((L, half), F32)
    c = jnp.concatenate([cos, cos, zpad], axis=-1)
    s1 = jnp.concatenate([-sin, zh, zpad], axis=-1)
    s2 = jnp.concatenate([zh, sin, zpad], axis=-1)
    if with_ctx:
        ident = jnp.ones((T, LANE), F32)
        zero = jnp.zeros((T, LANE), F32)
        c = jnp.concatenate([ident, c], axis=0)
        s1 = jnp.concatenate([zero, s1], axis=0)
        s2 = jnp.concatenate([zero, s2], axis=0)
    return c, s1, s2


def _rope_apply(x, c, s1, s2, half):
    return x * c + pltpu.roll(x, LANE - half, 1) * s1 + pltpu.roll(x, half, 1) * s2


def _tab_blk(i):
    xt = L // T
    return jnp.where(i < B * xt, 1 + i % xt, 0)


def _krope_kernel(x_ref, c_ref, s1_ref, s2_ref, o_ref):
    x = x_ref[...].astype(F32)
    o_ref[...] = _rope_apply(x, c_ref[...], s1_ref[...], s2_ref[...], B_ROPE // 2).astype(o_ref.dtype)


def _krope(ub, tabs):
    n_rows = ub.shape[0]
    cb = ub.shape[1] // LANE - 1
    tspec = pl.BlockSpec((T, LANE), lambda i: (_tab_blk(i), 0))
    return pl.pallas_call(
        _krope_kernel,
        grid=(n_rows // T,),
        in_specs=[pl.BlockSpec((T, LANE), lambda i: (i, cb)), tspec, tspec, tspec],
        out_specs=pl.BlockSpec((T, LANE), lambda i: (i, 0)),
        out_shape=jax.ShapeDtypeStruct((n_rows, LANE), BF16),
        compiler_params=_cp("parallel"),
    )(ub, *tabs)


MLA_KV_CHUNK = 512
MLA_TQ = 512


def _softmax_step(q, k, v1, m, acc):
    s = _dot_nt(q, k)
    m_new = jnp.maximum(m, jnp.max(s, axis=-1, keepdims=True))
    p = jnp.exp2((s - m_new).astype(BF16))
    acc = jnp.exp2(m - m_new) * acc + _dot(p, v1)
    return m_new, acc


def _softmax_init(rows):
    return jnp.full((rows, 1), NEG_BIG, F32), jnp.zeros((rows, 2 * B_VDIM), F32)


def _softmax_finish(acc):
    return acc[:, :B_VDIM] / acc[:, B_VDIM:B_VDIM + 1]


def _mla_attn_kernel(q_ref, c_ref, s1_ref, s2_ref, kxn_ref, vx_ref, krx_ref, kcn_ref, vc_ref, krc_ref,
                     o_ref, kfx, kfc, vfx, vfc):
    @pl.when(pl.program_id(2) == 0)
    def _():
        kfx[:, :B_NOPE] = kxn_ref[...]
        kfx[:, B_NOPE:] = krx_ref[...]
        kfc[:, :B_NOPE] = kcn_ref[...]
        kfc[:, B_NOPE:] = krc_ref[...]
        vfx[:, :B_VDIM] = vx_ref[...]
        vfx[:, B_VDIM:] = jnp.ones((L, B_VDIM), BF16)
        vfc[:, :B_VDIM] = vc_ref[...]
        vfc[:, B_VDIM:] = jnp.ones((LC, B_VDIM), BF16)

    qr = _rope_apply(q_ref[:, B_NOPE:].astype(F32), c_ref[...], s1_ref[...], s2_ref[...], B_ROPE // 2)
    q = jnp.concatenate([q_ref[:, :B_NOPE], qr.astype(BF16)], axis=1)
    m, acc = _softmax_step(q, kfc[...], vfc[...], *_softmax_init(MLA_TQ))
    for n in range(L // MLA_KV_CHUNK):
        sl = slice(n * MLA_KV_CHUNK, (n + 1) * MLA_KV_CHUNK)
        m, acc = _softmax_step(q, kfx[sl, :], vfx[sl, :], m, acc)
    o_ref[...] = _softmax_finish(acc).astype(o_ref.dtype)


def _mla_ctx_kernel(q_ref, kcn_ref, vc_ref, krc_ref, o_ref):
    k = jnp.concatenate([kcn_ref[...], krc_ref[...]], axis=1)
    v1 = jnp.concatenate([vc_ref[...], jnp.ones((LC, B_VDIM), BF16)], axis=1)
    m, acc = _softmax_step(q_ref[...], k, v1, *_softmax_init(LC))
    o_ref[...] = _softmax_finish(acc).astype(o_ref.dtype)


def _mla_attn(q, kv, kr, tabs):
    xq = L // MLA_TQ
    ctx0 = B * L // LC
    tspec = pl.BlockSpec((MLA_TQ, LANE), lambda b, h, qi: (qi, 0))
    kc_spec = pl.BlockSpec((LC, LANE), lambda b, h, *_: (ctx0 + b, 2 * h))
    vc_spec = pl.BlockSpec((LC, LANE), lambda b, h, *_: (ctx0 + b, 2 * h + 1))
    krc_spec = pl.BlockSpec((LC, LANE), lambda b, h, *_: (ctx0 + b, 0))
    y_x = pl.pallas_call(
        _mla_attn_kernel,
        grid=(B, B_HEADS, xq),
        in_specs=[pl.BlockSpec((MLA_TQ, B_QPAD), lambda b, h, qi: (b * xq + qi, h)),
                  tspec, tspec, tspec,
                  pl.BlockSpec((L, LANE), lambda b, h, qi: (b, 2 * h)),
                  pl.BlockSpec((L, LANE), lambda b, h, qi: (b, 2 * h + 1)),
                  pl.BlockSpec((L, LANE), lambda b, h, qi: (b, 0)),
                  kc_spec, vc_spec, krc_spec],
        out_specs=pl.BlockSpec((MLA_TQ, B_VDIM), lambda b, h, qi: (b * xq + qi, h)),
        out_shape=jax.ShapeDtypeStruct((B * L, B_HEADS * B_VDIM), BF16),
        scratch_shapes=[pltpu.VMEM((L, B_QPAD), BF16), pltpu.VMEM((LC, B_QPAD), BF16),
                        pltpu.VMEM((L, 2 * B_VDIM), BF16), pltpu.VMEM((LC, 2 * B_VDIM), BF16)],
        compiler_params=_cp("parallel", "parallel", "arbitrary"),
    )(q, *tabs, kv, kv, kr, kv, kv, kr)
    y_c = pl.pallas_call(
        _mla_ctx_kernel,
        grid=(B, B_HEADS),
        in_specs=[pl.BlockSpec((LC, B_QPAD), lambda b, h: (ctx0 + b, h)), kc_spec, vc_spec, krc_spec],
        out_specs=pl.BlockSpec((LC, B_VDIM), lambda b, h: (b, h)),
        out_shape=jax.ShapeDtypeStruct((B * LC, B_HEADS * B_VDIM), BF16),
        compiler_params=_cp("parallel", "parallel"),
    )(q, kv, kv, kr)
    return jnp.concatenate([y_x, y_c], axis=0)


def _rope_qk_kernel(x_ref, c_ref, s1_ref, s2_ref, o_ref):
    c, s1, s2 = c_ref[...], s1_ref[...], s2_ref[...]
    for h in range(x_ref.shape[1] // LANE):
        sl = slice(h * LANE, (h + 1) * LANE)
        o_ref[:, sl] = _rope_apply(x_ref[:, sl].astype(F32), c, s1, s2, 1).astype(o_ref.dtype)


def _rope_qk(ux):
    width = C_Q + C_KV
    tn = C_KV
    xt = L // T
    nq = C_Q // tn
    tabs = [jnp.concatenate([tq, tk], axis=0)
            for tq, tk in zip(_rope_tables_pairs(HEAD_DIM ** -0.5), _rope_tables_pairs(1.0))]
    tspec = pl.BlockSpec((T, LANE), lambda i, j: (i % xt + jnp.where(j < nq, 0, xt), 0))
    return pl.pallas_call(
        _rope_qk_kernel,
        grid=(B * xt, width // tn),
        in_specs=[pl.BlockSpec((T, tn), lambda i, j: (i, j)), tspec, tspec, tspec],
        out_specs=pl.BlockSpec((T, tn), lambda i, j: (i, j)),
        out_shape=jax.ShapeDtypeStruct((B * L, width), BF16),
        compiler_params=_cp("parallel", "parallel"),
    )(ux, *tabs)


def _win_kernel(sink_ref, q_ref, kx_ref, vx_ref, kc_ref, vc_ref, o_ref):
    g = pl.program_id(1)
    t = pl.program_id(2)
    span = 2 * T
    start = pl.multiple_of(jnp.clip(t * T - C_WINDOW, 0, L - span), C_WINDOW)
    kw = kx_ref[pl.ds(start, span), :]
    vw = vx_ref[pl.ds(start, span), :]
    qpos = t * T + lax.broadcasted_iota(jnp.int32, (T, span), 0)
    kpos = start + lax.broadcasted_iota(jnp.int32, (T, span), 1)
    valid = jnp.abs(qpos - kpos) <= C_WINDOW
    kc = kc_ref[...]
    vc = vc_ref[...]
    for r in range(C_GROUP):
        sl = slice(r * HEAD_DIM, (r + 1) * HEAD_DIM)
        q = q_ref[:, sl]
        sw = jnp.where(valid, _dot_nt(q, kw), NEG_BIG)
        sc = _dot_nt(q, kc)
        sk = sink_ref[g * C_GROUP + r]
        m = jnp.maximum(jnp.maximum(jnp.max(sw, axis=-1, keepdims=True),
                                    jnp.max(sc, axis=-1, keepdims=True)), sk)
        pw = jnp.exp(sw - m)
        pc = jnp.exp(sc - m)
        l = jnp.sum(pw, axis=-1, keepdims=True) + jnp.sum(pc, axis=-1, keepdims=True) + jnp.exp(sk - m)
        o = _dot(pw.astype(BF16), vw) + _dot(pc.astype(BF16), vc)
        o_ref[:, sl] = (o / l).astype(o_ref.dtype)


def _win_attn(qk, ux, sink):
    assert T == 2 * C_WINDOW
    xt = L // T
    ctx0 = B * L // LC
    kcol = C_Q // LANE
    vcol = (C_Q + C_KV) // LANE
    gw = C_GROUP * HEAD_DIM
    return pl.pallas_call(
        _win_kernel,
        grid=(B, C_KV_HEADS, xt),
        in_specs=[pl.BlockSpec(memory_space=pltpu.SMEM),
                  pl.BlockSpec((T, gw), lambda b, g, t: (b * xt + t, g)),
                  pl.BlockSpec((L, LANE), lambda b, g, t: (b, kcol + g)),
                  pl.BlockSpec((L, LANE), lambda b, g, t: (b, vcol + g)),
                  pl.BlockSpec((LC, LANE), lambda b, g, t: (ctx0 + b, kcol + g)),
                  pl.BlockSpec((LC, LANE), lambda b, g, t: (ctx0 + b, vcol + g))],
        out_specs=pl.BlockSpec((T, gw), lambda b, g, t: (b * xt + t, g)),
        out_shape=jax.ShapeDtypeStruct((B * L, C_Q), BF16),
        compiler_params=_cp("parallel", "parallel", "arbitrary"),
    )(sink, qk, qk, ux, ux, ux)


CONV_TN = 512
SUBLANE = 8


def _conv_kernel(bg_ref, cg_ref, hh_ref, cgp_ref, hhp_ref, cgn_ref, hhn_ref, w_ref, o_ref):
    xt = L // T
    t = pl.program_id(0) % xt
    z = cg_ref[...].astype(F32) * hh_ref[...].astype(F32)
    zp = cgp_ref[SUBLANE - 1:SUBLANE, :].astype(F32) * hhp_ref[SUBLANE - 1:SUBLANE, :].astype(F32)
    zn = cgn_ref[0:1, :].astype(F32) * hhn_ref[0:1, :].astype(F32)
    zp = jnp.where(t > 0, zp, 0.0)
    zn = jnp.where(t < xt - 1, zn, 0.0)
    rows = lax.broadcasted_iota(jnp.int32, (T, CONV_TN), 0)
    z_prev = jnp.where(rows == 0, zp, pltpu.roll(z, 1, 0))
    z_next = jnp.where(rows == T - 1, zn, pltpu.roll(z, T - 1, 0))
    y = w_ref[0:1, :] * z_prev + w_ref[1:2, :] * z + w_ref[2:3, :] * z_next
    o_ref[...] = (bg_ref[...].astype(F32) * y).astype(o_ref.dtype)


def _conv(ux, conv_w):
    xt = L // T
    nx = B * xt
    c0 = C_IN // CONV_TN
    nc = D_CH // CONV_TN
    rb = T // SUBLANE

    def main(cb):
        return pl.BlockSpec((T, CONV_TN), lambda i, j: (i, c0 + cb * nc + j))

    def prev(cb):
        return pl.BlockSpec((SUBLANE, CONV_TN), lambda i, j: (jnp.maximum(i * rb - 1, 0), c0 + cb * nc + j))

    def nxt(cb):
        return pl.BlockSpec((SUBLANE, CONV_TN),
                            lambda i, j: (jnp.minimum((i + 1) * rb, nx * rb - 1), c0 + cb * nc + j))

    return pl.pallas_call(
        _conv_kernel,
        grid=(nx, nc),
        in_specs=[main(0), main(1), main(2), prev(1), prev(2), nxt(1), nxt(2),
                  pl.BlockSpec((D_CONV, CONV_TN), lambda i, j: (0, j))],
        out_specs=pl.BlockSpec((T, CONV_TN), lambda i, j: (i, j)),
        out_shape=jax.ShapeDtypeStruct((B * L, D_CH), BF16),
        compiler_params=_cp("parallel", "parallel"),
    )(ux, ux, ux, ux, ux, ux, ux, conv_w)


def _router_kernel(h_ref, wr_ref, b_ref, i_ref, g_ref):
    tm = h_ref.shape[0]
    scores = _sigmoid(_dot_nt(wr_ref[...], h_ref[...]))
    biased = scores + b_ref[...]
    ng = N_GROUPS
    sl = [biased[j * ng:(j + 1) * ng] for j in range(GROUP_SIZE)]
    sc = [scores[j * ng:(j + 1) * ng] for j in range(GROUP_SIZE)]
    m1 = sl[0]
    m2 = jnp.full_like(m1, -jnp.inf)
    for v in sl[1:]:
        m2 = jnp.maximum(m2, jnp.minimum(m1, v))
        m1 = jnp.maximum(m1, v)
    gs = m1 + m2
    gidx = lax.broadcasted_iota(jnp.int32, (ng, tm), 0)
    rank = jnp.zeros((ng, tm), jnp.int32)
    for k in range(1, ng):
        r = pltpu.roll(gs, k, 0)
        rank += jnp.where((r > gs) | ((r == gs) & (gidx >= k)), 1, 0)
    keep = rank < TOPK_GROUPS
    ms = [jnp.where(keep, v, -jnp.inf) for v in sl]
    rolled = [[ms[j] if k == 0 else pltpu.roll(ms[j], k, 0) for k in range(ng)] for j in range(GROUP_SIZE)]
    tot = jnp.zeros((ng, tm), F32)
    picked = []
    ranks = []
    for j in range(GROUP_SIZE):
        cnt = jnp.zeros((ng, tm), jnp.int32)
        for jp in range(GROUP_SIZE):
            for k in range(ng):
                r = rolled[jp][k]
                if k == 0:
                    if jp == j:
                        continue
                    beats = (r >= ms[j]) if jp < j else (r > ms[j])
                else:
                    beats = (r > ms[j]) | ((r == ms[j]) & (gidx >= k))
                cnt += jnp.where(beats, 1, 0)
        w = jnp.where(cnt < TOP_K, sc[j], 0.0)
        picked.append(w)
        ranks.append(cnt)
        tot = tot + w
    tot = jnp.sum(tot, axis=0, keepdims=True)
    wn = [p / tot * ROUTE_SCALE for p in picked]
    eid = [(gidx * GROUP_SIZE + j).astype(F32) for j in range(GROUP_SIZE)]
    for k in range(TOP_K):
        ik = jnp.zeros((ng, tm), F32)
        wk = jnp.zeros((ng, tm), F32)
        for j in range(GROUP_SIZE):
            hit = ranks[j] == k
            ik += jnp.where(hit, eid[j], 0.0)
            wk += jnp.where(hit, wn[j], 0.0)
        i_ref[k:k + 1, :] = jnp.sum(ik, axis=0, keepdims=True).astype(jnp.int32)
        g_ref[k:k + 1, :] = jnp.sum(wk, axis=0, keepdims=True)


def _router(h, wr_t, bias, nrows):
    out = pl.BlockSpec((TOP_K, TM), lambda i: (0, i))
    return pl.pallas_call(
        _router_kernel,
        grid=(nrows // TM,),
        in_specs=[pl.BlockSpec((TM, D), lambda i: (i, 0)),
                  pl.BlockSpec((N_EXPERTS, D), lambda i: (0, 0)),
                  pl.BlockSpec((N_EXPERTS, 1), lambda i: (0, 0))],
        out_specs=[out, out],
        out_shape=[jax.ShapeDtypeStruct((TOP_K, nrows), jnp.int32),
                   jax.ShapeDtypeStruct((TOP_K, nrows), F32)],
        compiler_params=_cp("parallel"),
    )(h, wr_t, bias)


def _router_order():
    r = np.arange(N_EXPERTS)
    return (r % N_GROUPS) * GROUP_SIZE + r // N_GROUPS


def _glu_kernel(x_ref, wg_ref, wu_ref, wd_ref, o_ref):
    x = x_ref[...]
    hid = _silu(_dot(x, wg_ref[...])) * _dot(x, wu_ref[...])
    o_ref[...] = _dot(hid.astype(BF16), wd_ref[...]).astype(o_ref.dtype)


def _glu(h, wg, wu, wd, nrows):
    return pl.pallas_call(
        _glu_kernel,
        grid=(nrows // TM,),
        in_specs=[pl.BlockSpec((TM, D), lambda i: (i, 0)),
                  pl.BlockSpec((D, D_EXPERT), lambda i: (0, 0)),
                  pl.BlockSpec((D, D_EXPERT), lambda i: (0, 0)),
                  pl.BlockSpec((D_EXPERT, D), lambda i: (0, 0))],
        out_specs=pl.BlockSpec((TM, D), lambda i: (i, 0)),
        out_shape=jax.ShapeDtypeStruct((nrows, D), BF16),
        compiler_params=_cp("parallel"),
    )(h, wg, wu, wd)


TME = 256
TD = 512
TC = 128


def _route_plan(idx, nrows):
    nt = (nrows * TOP_K) // TME + N_EXPERTS
    onehot = idx[:, :, None] == jnp.arange(N_EXPERTS, dtype=jnp.int32)[None, None, :]
    cs = jnp.cumsum(jnp.any(onehot, axis=1).astype(jnp.int32), axis=0)
    counts = cs[-1]
    padded = ((counts + TME - 1) // TME) * TME
    ends = jnp.cumsum(padded)
    off = ends - padded
    dest = jnp.sum(jnp.where(onehot, (off[None, :] + cs - 1)[:, None, :], 0), axis=2).astype(jnp.int32)
    tiles = jnp.arange(nt, dtype=jnp.int32)
    used = ends[-1] // TME
    t_expert = jnp.sum((ends // TME)[None, :] <= tiles[:, None], axis=1)
    t_expert = jnp.minimum(t_expert, N_EXPERTS - 1).astype(jnp.int32)
    t_valid = (tiles < used).astype(jnp.int32)
    t_first = ((tiles * TME == off[t_expert]) & (tiles < used)).astype(jnp.int32)
    t_src = jnp.minimum(tiles, used - 1).astype(jnp.int32)
    return dict(nt=nt, dest=dest, used=used.reshape(1).astype(jnp.int32),
                pad_start=(off + counts).astype(jnp.int32),
                pad_cnt=(padded - counts).astype(jnp.int32),
                t_expert=t_expert, t_valid=t_valid, t_first=t_first, t_src=t_src)


def _slab(ref, row):
    return ref.at[pl.ds(pl.multiple_of(row * NP, NP), NP), :]


def _dispatch_kernel(ps_ref, pc_ref, used_ref, dest_ref, hp_ref, xs_ref, sem, *, nt):
    i = pl.program_id(0)

    def body(j, c):
        for k in range(TOP_K):
            pltpu.make_async_copy(_slab(hp_ref, j), _slab(xs_ref, dest_ref[0, 0, j * TOP_K + k]),
                                  sem).start(priority=k % 2)
        return c

    lax.fori_loop(0, TD, body, 0)
    for k in range(TOP_K):
        pltpu.make_async_copy(hp_ref, xs_ref.at[pl.ds(0, TD * NP), :], sem).wait()

    @pl.when(i == 0)
    def _():
        def pad_copy(e):
            n = pc_ref[e] * NP
            return pltpu.make_async_copy(hp_ref.at[pl.ds(0, n), :],
                                         xs_ref.at[pl.ds(pl.multiple_of(ps_ref[e] * NP, NP), n), :], sem)

        def tail_copy(t):
            n = TME * NP
            return pltpu.make_async_copy(hp_ref.at[pl.ds(0, n), :],
                                         xs_ref.at[pl.ds(pl.multiple_of(t * n, n), n), :], sem)

        def over(lo, hi, copy, pred):
            def start(e, c):
                @pl.when(pred(e))
                def _():
                    copy(e).start()
                return c

            def wait(e, c):
                @pl.when(pred(e))
                def _():
                    copy(e).wait()
                return c

            lax.fori_loop(lo, hi, start, 0)
            lax.fori_loop(lo, hi, wait, 0)

        over(0, N_EXPERTS, pad_copy, lambda e: pc_ref[e] > 0)
        over(0, nt, tail_copy, lambda t: t >= used_ref[0])


def _dispatch(hp, plan, nrows):
    dest = plan["dest"].reshape(nrows // TD, 1, TD * TOP_K)
    return pl.pallas_call(
        functools.partial(_dispatch_kernel, nt=plan["nt"]),
        grid_spec=pltpu.PrefetchScalarGridSpec(
            num_scalar_prefetch=3,
            grid=(nrows // TD,),
            in_specs=[pl.BlockSpec((1, 1, TD * TOP_K), lambda i, ps, pc, us: (i, 0, 0), memory_space=pltpu.SMEM),
                      pl.BlockSpec((TD * NP, LANE), lambda i, ps, pc, us: (i, 0))],
            out_specs=pl.BlockSpec(memory_space=pl.ANY),
            scratch_shapes=[pltpu.SemaphoreType.DMA(())]),
        out_shape=jax.ShapeDtypeStruct((plan["nt"] * TME * NP, LANE), jnp.uint32),
        compiler_params=pltpu.CompilerParams(dimension_semantics=("arbitrary",), disable_bounds_checks=True,
                                             vmem_limit_bytes=VMEM_LIMIT_BYTES),
    )(plan["pad_start"], plan["pad_cnt"], plan["used"], dest, hp)


def _expert_kernel(te_ref, tv_ref, tf_ref, ts_ref, x_ref, wg_ref, wu_ref, wd_ref, o_ref, wgu_b, wd_b):
    i = pl.program_id(0)

    @pl.when(tv_ref[i] == 0)
    def _():
        o_ref[...] = jnp.zeros_like(o_ref)

    @pl.when(tv_ref[i] == 1)
    def _():
        @pl.when(tf_ref[i] == 1)
        def _():
            wgu_b[:, :D_EXPERT] = wg_ref[0, 0].astype(BF16)
            wgu_b[:, D_EXPERT:] = wu_ref[0, 0].astype(BF16)
            wd_b[...] = wd_ref[0, 0].astype(BF16)

        pieces = []
        for p in range(NP):
            lo, hi = _unpack2(x_ref[pl.ds(p, TME, stride=NP), :])
            pieces += [lo.astype(BF16), hi.astype(BF16)]
        a = _dot(jnp.concatenate(pieces, axis=1), wgu_b[...])
        hid = _silu(a[:, :D_EXPERT]) * a[:, D_EXPERT:]
        y = _dot(hid.astype(BF16), wd_b[...])
        for p in range(NP):
            o_ref[pl.ds(p, TME, stride=NP), :] = _pack2(y[:, p * PACK:p * PACK + LANE],
                                                        y[:, p * PACK + LANE:(p + 1) * PACK])


def _experts(xs, plan, wg, wu, wd, layer):
    nt = plan["nt"]
    rows = pl.BlockSpec((TME * NP, LANE), lambda i, te, tv, tf, ts: (i, 0))
    return pl.pallas_call(
        _expert_kernel,
        grid_spec=pltpu.PrefetchScalarGridSpec(
            num_scalar_prefetch=4,
            grid=(nt,),
            in_specs=[pl.BlockSpec((TME * NP, LANE), lambda i, te, tv, tf, ts: (ts[i], 0)),
                      pl.BlockSpec((1, 1, D, D_EXPERT), lambda i, te, tv, tf, ts: (layer, te[i], 0, 0)),
                      pl.BlockSpec((1, 1, D, D_EXPERT), lambda i, te, tv, tf, ts: (layer, te[i], 0, 0)),
                      pl.BlockSpec((1, 1, D_EXPERT, D), lambda i, te, tv, tf, ts: (layer, te[i], 0, 0))],
            out_specs=rows,
            scratch_shapes=[pltpu.VMEM((D, 2 * D_EXPERT), BF16), pltpu.VMEM((D_EXPERT, D), BF16)]),
        out_shape=jax.ShapeDtypeStruct((nt * TME * NP, LANE), jnp.uint32),
        compiler_params=pltpu.CompilerParams(dimension_semantics=("arbitrary",),
                                             vmem_limit_bytes=EXPERT_VMEM_LIMIT_BYTES),
    )(plan["t_expert"], plan["t_valid"], plan["t_first"], plan["t_src"], xs, wg, wu, wd)


def _gather_pitch():
    tiles = -(-NP // SUBLANE)
    return (tiles + 1 + tiles % 2) * SUBLANE


def _combine_kernel(dcur_ref, dnxt_ref, w_ref, r_ref, sh_ref, g_ref, ys_ref, o_ref, buf0, buf1, sem):
    i = pl.program_id(0)
    last = pl.num_programs(0) - 1

    pitch = _gather_pitch()

    def issue(d_ref, buf, slot):
        def body(j, c):
            for k in range(TOP_K):
                dst = buf.at[pl.ds(pl.multiple_of((k * TC + j) * pitch, SUBLANE), NP), :]
                pltpu.make_async_copy(_slab(ys_ref, d_ref[0, 0, j * TOP_K + k]), dst,
                                      sem.at[slot]).start(priority=k % 2)
            return c

        lax.fori_loop(0, TC, body, 0)

    def drain(buf, slot):
        nrow = TC * TOP_K * NP
        pltpu.make_async_copy(ys_ref.at[pl.ds(0, nrow), :], buf.at[pl.ds(0, nrow), :], sem.at[slot]).wait()

    def reduce(buf):
        gates = [w_ref[:, k:k + 1] for k in range(TOP_K)]
        gmod = g_ref[0]
        for p in range(NP):
            lo_sl = slice(p * PACK, p * PACK + LANE)
            hi_sl = slice(p * PACK + LANE, (p + 1) * PACK)
            lo_acc = sh_ref[:, lo_sl].astype(F32)
            hi_acc = sh_ref[:, hi_sl].astype(F32)
            for k in range(TOP_K):
                lo, hi = _unpack2(buf[pl.ds(k * TC * pitch + p, TC, stride=pitch), :])
                lo_acc += gates[k] * lo
                hi_acc += gates[k] * hi
            o_ref[:, lo_sl] = r_ref[:, lo_sl] + gmod[:, lo_sl] * lo_acc
            o_ref[:, hi_sl] = r_ref[:, hi_sl] + gmod[:, hi_sl] * hi_acc

    @pl.when(i == 0)
    def _():
        issue(dcur_ref, buf0, 0)

    @pl.when(i % 2 == 0)
    def _():
        @pl.when(i < last)
        def _():
            issue(dnxt_ref, buf1, 1)

        drain(buf0, 0)
        reduce(buf0)

    @pl.when(i % 2 == 1)
    def _():
        @pl.when(i < last)
        def _():
            issue(dnxt_ref, buf0, 0)

        drain(buf1, 1)
        reduce(buf1)


def _combine(resid, ys, shared, w, plan, gate, nrows):
    nsteps = nrows // TC
    dest = plan["dest"].reshape(nsteps, 1, TC * TOP_K)
    blk = pl.BlockSpec((TC, D), lambda i: (i, 0))
    return pl.pallas_call(
        _combine_kernel,
        grid=(nsteps,),
        in_specs=[pl.BlockSpec((1, 1, TC * TOP_K), lambda i: (i, 0, 0), memory_space=pltpu.SMEM),
                  pl.BlockSpec((1, 1, TC * TOP_K), lambda i: (jnp.minimum(i + 1, nsteps - 1), 0, 0),
                               memory_space=pltpu.SMEM),
                  pl.BlockSpec((TC, TOP_K), lambda i: (i, 0)),
                  blk, blk,
                  pl.BlockSpec((1, 1, D), lambda i: (_mod_row(i, TC), 0, 0)),
                  pl.BlockSpec(memory_space=pl.ANY)],
        out_specs=blk,
        out_shape=jax.ShapeDtypeStruct((nrows, D), F32),
        scratch_shapes=[pltpu.VMEM((TC * TOP_K * _gather_pitch(), LANE), jnp.uint32),
                        pltpu.VMEM((TC * TOP_K * _gather_pitch(), LANE), jnp.uint32),
                        pltpu.SemaphoreType.DMA((2,))],
        compiler_params=pltpu.CompilerParams(dimension_semantics=("arbitrary",),
                                             vmem_limit_bytes=VMEM_LIMIT_BYTES, disable_bounds_checks=True),
    )(dest, dest, w, resid, shared, gate, ys)


def _moe(s, h, hp, gate, nrows, layer, router_w, router_bias, wg, wu, wd, sg, su, sd):
    order = _router_order()
    wr_t = router_w.T[order].astype(BF16)
    bias = router_bias[order].reshape(N_EXPERTS, 1).astype(F32)
    idx_t, w_t = _router(h, wr_t, bias, nrows)
    plan = _route_plan(idx_t.T, nrows)
    xs = _dispatch(hp, plan, nrows)
    ys = _experts(xs, plan, wg, wu, wd, layer)
    shared = _glu(h, sg.astype(BF16), su.astype(BF16), sd.astype(BF16), nrows)
    return _combine(s, ys, shared, w_t.T, plan, gate, nrows)


def _mla_in_weights(w_in):
    wb = w_in[:, A_IN:]
    kr = wb[:, B_Q_RANK + B_KV_RANK:][:, _rope_perm(B_ROPE)]
    pad = jnp.zeros((D, LANE - B_ROPE), w_in.dtype)
    return jnp.concatenate([wb[:, :B_Q_RANK + B_KV_RANK], kr, pad], axis=1).astype(BF16)


def _mla_uq_weights(w_uq):
    scale = (B_NOPE + B_ROPE) ** -0.5 * np.log2(np.e)
    w = (w_uq * scale).reshape(B_Q_RANK, B_HEADS, B_NOPE + B_ROPE)
    rope = w[:, :, B_NOPE:][:, :, _rope_perm(B_ROPE)]
    pad = jnp.zeros((B_Q_RANK, B_HEADS, B_QPAD - B_NOPE - B_ROPE), w.dtype)
    return jnp.concatenate([w[:, :, :B_NOPE], rope, pad], axis=-1).reshape(B_Q_RANK, B_HEADS * B_QPAD).astype(BF16)


def kernel(x, c, ctx, c_ctx, ada_w, ada_b, norm_mix_w, norm_ffn_w, ab_w_in, ab_w_out, hgrn_lb_logits, hgrn_gnorm_w, mla_q_norm_w, mla_w_uq, mla_kv_norm_w, mla_w_ukv, cd_w_in, cd_w_out, gqa_sink, conv_w, router_w, router_bias, exp_w_gate, exp_w_up, exp_w_down, sh_w_gate, sh_w_up, sh_w_down, final_norm_w):
    nx = B * L
    n = nx + B * LC
    s = jnp.concatenate([x.reshape(nx, D), ctx.reshape(B * LC, D)], axis=0)
    lb_all = jnp.cumsum(jax.nn.softmax(hgrn_lb_logits.astype(F32), axis=0), axis=0)
    cond = jnp.concatenate([c, c_ctx[None, :], jnp.zeros((8 - B - 1, D), F32)], axis=0)

    def modulation(i):
        m = _ada(cond, ada_w, ada_b[i].reshape(1, N_MOD * D), i).reshape(8, N_MOD, D)
        return [m[:B + 1, k, :].reshape(B + 1, 1, D) for k in range(N_MOD)]

    def moe_args(i):
        return (i, router_w[i], router_bias[i], exp_w_gate, exp_w_up, exp_w_down,
                sh_w_gate[i], sh_w_up[i], sh_w_down[i])

    mod = modulation(0)
    h = _norm_mod(s, norm_mix_w[0], mod[1], mod[0], n)
    ua = _matmul(h, ab_w_in[0][:, :A_IN].astype(BF16), 1024)
    ub = _matmul(h, _mla_in_weights(ab_w_in[0]), B_Q_RANK + B_KV_RANK + LANE)
    lb = lb_all[0].reshape(2, A_HEADS, 1, A_DK)
    o_f, o_b = _hgrn(ua, lb[0], lb[1])
    ya = _hgrn_out(o_f, o_b, ua, hgrn_gnorm_w[0])
    tabs = _rope_tables(B_ROPE, True)
    q = _norm_mm(ub, 0, B_Q_RANK, mla_q_norm_w[0], _mla_uq_weights(mla_w_uq[0]), 1024)
    kv = _norm_mm(ub, B_Q_RANK // B_KV_RANK, B_KV_RANK, mla_kv_norm_w[0], mla_w_ukv[0].astype(BF16), 1024)
    kr = _krope(ub, tabs)
    yb = _mla_attn(q, kv, kr, tuple(t[T:] for t in tabs))
    s = _mm2_res(ya, yb, ab_w_out[0].astype(BF16), s, mod[2], n)
    h, hp = _norm_mod_pack(s, norm_ffn_w[0], mod[4], mod[3], n)
    s = _moe(s, h, hp, mod[5], n, *moe_args(0))

    mod = modulation(1)
    h = _norm_mod(s, norm_mix_w[1], mod[1], mod[0], n)
    ux = _matmul(h, cd_w_in[0].astype(BF16), 1024)
    qk = _rope_qk(ux)
    att = _win_attn(qk, ux, gqa_sink[0].astype(F32))
    cv = _conv(ux, conv_w[0])
    s = _mm2_res(att, cv, cd_w_out[0].astype(BF16), s, mod[2], nx)
    h, hp = _norm_mod_pack(s, norm_ffn_w[1], mod[4], mod[3], nx)
    s = _moe(s, h, hp, mod[5], nx, *moe_args(1))
    return _rms(s, final_norm_w).reshape(B, L, D)
```

```python
import functools

import jax
import jax.numpy as jnp
import numpy as np
from jax import lax
from jax.experimental import pallas as pl
from jax.experimental.pallas import tpu as pltpu

F32 = jnp.float32
BF16 = jnp.bfloat16

D = 4096
B = 2
L = 8192
LC = 256
GRID_W = 64
RMS_EPS = 1e-6
ROPE_THETA = 10000.0
N_MOD = 6

A_HEADS = 16
A_DK = 128
A_KEY = A_HEADS * A_DK
A_CHUNK = 32
A_IN = 5 * A_KEY

B_HEADS = 16
B_Q_RANK = 1024
B_KV_RANK = 512
B_NOPE = 128
B_ROPE = 64
B_VDIM = 128
B_QPAD = 256

C_HEADS = 16
C_KV_HEADS = 4
C_GROUP = C_HEADS // C_KV_HEADS
C_WINDOW = 128
HEAD_DIM = 128
C_Q = C_HEADS * HEAD_DIM
C_KV = C_KV_HEADS * HEAD_DIM
C_IN = C_Q + 2 * C_KV
D_CH = 2048
D_CONV = 3

N_EXPERTS = 64
TOP_K = 8
N_GROUPS = 8
TOPK_GROUPS = 4
GROUP_SIZE = N_EXPERTS // N_GROUPS
D_EXPERT = 256
ROUTE_SCALE = 2.5

T = 256
TM = 512
LANE = 128
VMEM_LIMIT_BYTES = 48 * 1024 * 1024
EXPERT_VMEM_LIMIT_BYTES = 56 * 1024 * 1024
NEG_BIG = -1e30
PACK = 2 * LANE
NP = D // PACK


def _cp(*sem):
    return pltpu.CompilerParams(dimension_semantics=sem, vmem_limit_bytes=VMEM_LIMIT_BYTES)


def _sigmoid(x):
    return 1.0 / (1.0 + jnp.exp(-x))


def _silu(x):
    return x * _sigmoid(x)


def _dot(a, b):
    return jnp.dot(a, b, preferred_element_type=F32)


def _dot_nt(a, b):
    return lax.dot_general(a, b, (((1,), (1,)), ((), ())), preferred_element_type=F32)


def _mod_row(i, tile):
    return jnp.where(i < (B * L) // tile, i // (L // tile), B)


def _ada_kernel(a_ref, w_ref, b_ref, o_ref):
    a = _silu(a_ref[...]).astype(BF16)
    o_ref[...] = _dot(a, w_ref[0].astype(BF16)) + b_ref[...]


def _ada(a, w, bias, layer):
    n = w.shape[2]
    tn = 512
    return pl.pallas_call(
        _ada_kernel,
        grid=(n // tn,),
        in_specs=[pl.BlockSpec((8, D), lambda j: (0, 0)),
                  pl.BlockSpec((1, D, tn), lambda j: (layer, 0, j)),
                  pl.BlockSpec((1, tn), lambda j: (0, j))],
        out_specs=pl.BlockSpec((8, tn), lambda j: (0, j)),
        out_shape=jax.ShapeDtypeStruct((8, n), F32),
        compiler_params=_cp("parallel"),
    )(a, w, bias)


def _norm_mod_kernel(s_ref, w_ref, sc_ref, sh_ref, o_ref):
    x = s_ref[...]
    y = x * lax.rsqrt(jnp.mean(x * x, axis=-1, keepdims=True) + RMS_EPS) * w_ref[...]
    o_ref[...] = (y * (1.0 + sc_ref[0]) + sh_ref[0]).astype(o_ref.dtype)


def _norm_mod(s, w, scale, shift, nrows):
    mod_spec = pl.BlockSpec((1, 1, D), lambda i: (_mod_row(i, T), 0, 0))
    return pl.pallas_call(
        _norm_mod_kernel,
        grid=(nrows // T,),
        in_specs=[pl.BlockSpec((T, D), lambda i: (i, 0)),
                  pl.BlockSpec((1, D), lambda i: (0, 0)),
                  mod_spec, mod_spec],
        out_specs=pl.BlockSpec((T, D), lambda i: (i, 0)),
        out_shape=jax.ShapeDtypeStruct((nrows, D), BF16),
        compiler_params=_cp("parallel"),
    )(s, w.reshape(1, D), scale, shift)


def _pack2(lo, hi):
    lo_b = lax.bitcast_convert_type(lo.astype(BF16).astype(F32), jnp.uint32) >> 16
    hi_b = lax.bitcast_convert_type(hi.astype(BF16).astype(F32), jnp.uint32)
    return hi_b | lo_b


def _unpack2(w):
    lo = lax.bitcast_convert_type(w << 16, F32)
    hi = lax.bitcast_convert_type(w & jnp.uint32(0xFFFF0000), F32)
    return lo, hi


def _norm_mod_pack_kernel(s_ref, w_ref, sc_ref, sh_ref, o_ref, p_ref):
    x = s_ref[...]
    y = x * lax.rsqrt(jnp.mean(x * x, axis=-1, keepdims=True) + RMS_EPS) * w_ref[...]
    y = y * (1.0 + sc_ref[0]) + sh_ref[0]
    o_ref[...] = y.astype(o_ref.dtype)
    for p in range(NP):
        p_ref[pl.ds(p, T, stride=NP), :] = _pack2(y[:, p * PACK:p * PACK + LANE], y[:, p * PACK + LANE:(p + 1) * PACK])


def _norm_mod_pack(s, w, scale, shift, nrows):
    mod_spec = pl.BlockSpec((1, 1, D), lambda i: (_mod_row(i, T), 0, 0))
    return pl.pallas_call(
        _norm_mod_pack_kernel,
        grid=(nrows // T,),
        in_specs=[pl.BlockSpec((T, D), lambda i: (i, 0)),
                  pl.BlockSpec((1, D), lambda i: (0, 0)),
                  mod_spec, mod_spec],
        out_specs=[pl.BlockSpec((T, D), lambda i: (i, 0)),
                   pl.BlockSpec((T * NP, LANE), lambda i: (i, 0))],
        out_shape=[jax.ShapeDtypeStruct((nrows, D), BF16),
                   jax.ShapeDtypeStruct((nrows * NP, LANE), jnp.uint32)],
        compiler_params=_cp("parallel"),
    )(s, w.reshape(1, D), scale, shift)


def _rms_kernel(s_ref, w_ref, o_ref):
    x = s_ref[...]
    o_ref[...] = x * lax.rsqrt(jnp.mean(x * x, axis=-1, keepdims=True) + RMS_EPS) * w_ref[...]


def _rms(s, w):
    nrows = s.shape[0]
    return pl.pallas_call(
        _rms_kernel,
        grid=(nrows // T,),
        in_specs=[pl.BlockSpec((T, D), lambda i: (i, 0)),
                  pl.BlockSpec((1, D), lambda i: (0, 0))],
        out_specs=pl.BlockSpec((T, D), lambda i: (i, 0)),
        out_shape=jax.ShapeDtypeStruct((nrows, D), F32),
        compiler_params=_cp("parallel"),
    )(s, w.reshape(1, D))


def _mm_kernel(a_ref, w_ref, o_ref):
    o_ref[...] = _dot(a_ref[...], w_ref[...]).astype(o_ref.dtype)


def _matmul(a, w, tn):
    m, k = a.shape
    n = w.shape[1]
    return pl.pallas_call(
        _mm_kernel,
        grid=(m // TM, n // tn),
        in_specs=[pl.BlockSpec((TM, k), lambda i, j: (i, 0)),
                  pl.BlockSpec((k, tn), lambda i, j: (0, j))],
        out_specs=pl.BlockSpec((TM, tn), lambda i, j: (i, j)),
        out_shape=jax.ShapeDtypeStruct((m, n), BF16),
        compiler_params=_cp("parallel", "arbitrary"),
    )(a, w)


def _norm_mm_kernel(a_ref, nw_ref, w_ref, o_ref):
    x = a_ref[...].astype(F32)
    y = x * lax.rsqrt(jnp.mean(x * x, axis=-1, keepdims=True) + RMS_EPS) * nw_ref[...]
    o_ref[...] = _dot(y.astype(BF16), w_ref[...]).astype(o_ref.dtype)


def _norm_mm(u, acol, kdim, nw, w, tn):
    m = u.shape[0]
    n = w.shape[1]
    return pl.pallas_call(
        _norm_mm_kernel,
        grid=(m // TM, n // tn),
        in_specs=[pl.BlockSpec((TM, kdim), lambda i, j: (i, acol)),
                  pl.BlockSpec((1, kdim), lambda i, j: (0, 0)),
                  pl.BlockSpec((kdim, tn), lambda i, j: (0, j))],
        out_specs=pl.BlockSpec((TM, tn), lambda i, j: (i, j)),
        out_shape=jax.ShapeDtypeStruct((m, n), BF16),
        compiler_params=_cp("parallel", "arbitrary"),
    )(u, nw.reshape(1, kdim), w)


def _mm2_res_kernel(a1_ref, a2_ref, w1_ref, w2_ref, r_ref, g_ref, o_ref):
    acc = _dot(a1_ref[...], w1_ref[...]) + _dot(a2_ref[...], w2_ref[...])
    o_ref[...] = r_ref[...] + g_ref[0] * acc


def _mm2_res(a1, a2, w, resid, gate, nrows):
    k1 = a1.shape[1]
    k2 = a2.shape[1]
    assert k1 == k2 and w.shape == (k1 + k2, D)
    tn = 512
    return pl.pallas_call(
        _mm2_res_kernel,
        grid=(nrows // TM, D // tn),
        in_specs=[pl.BlockSpec((TM, k1), lambda i, j: (i, 0)),
                  pl.BlockSpec((TM, k2), lambda i, j: (i, 0)),
                  pl.BlockSpec((k1, tn), lambda i, j: (0, j)),
                  pl.BlockSpec((k2, tn), lambda i, j: (1, j)),
                  pl.BlockSpec((TM, tn), lambda i, j: (i, j)),
                  pl.BlockSpec((1, 1, tn), lambda i, j: (_mod_row(i, TM), 0, j))],
        out_specs=pl.BlockSpec((TM, tn), lambda i, j: (i, j)),
        out_shape=jax.ShapeDtypeStruct((nrows, D), F32),
        compiler_params=_cp("parallel", "arbitrary"),
    )(a1, a2, w, w, resid, gate)


def _split3(x):
    hi = x.astype(BF16)
    r = x - hi.astype(F32)
    mid = r.astype(BF16)
    lo = (r - mid.astype(F32)).astype(BF16)
    return hi, mid, lo


def _hgrn_dir(q_raw, f_raw, v, lb, st_ref, o_ref, rev):
    c = A_CHUNK
    q = _silu(q_raw.astype(F32)) * (A_DK ** -0.5)
    f = lb + (1.0 - lb) * _sigmoid(f_raw.astype(F32))
    k = 1.0 - f
    logf = jnp.log(f)
    row = lax.broadcasted_iota(jnp.int32, (T, T), 0)
    col = lax.broadcasted_iota(jnp.int32, (T, T), 1)
    mask = ((row ^ col) < c) & ((col >= row) if rev else (col <= row))
    tri = jnp.where(mask, 1.0, 0.0).astype(BF16)
    hi, mid, lo = _split3(logf)
    cum = _dot(tri, hi) + _dot(tri, mid) + _dot(tri, lo)
    nchunk = T // c
    cum3 = cum.reshape(nchunk, c, LANE)
    ref = cum3[:, c // 2 - 1:c // 2] if rev else cum3[:, c // 2:c // 2 + 1]
    last = cum3[:, 0:1] if rev else cum3[:, c - 1:c]
    q3 = q.reshape(nchunk, c, LANE)
    k3 = k.reshape(nchunk, c, LANE)
    qs = (q3 * jnp.exp(cum3 - ref)).reshape(T, LANE).astype(BF16)
    ks = (k3 * jnp.exp(ref - cum3)).reshape(T, LANE).astype(BF16)
    qd = (q * jnp.exp(cum)).astype(BF16)
    kd = (k3 * jnp.exp(last - cum3)).reshape(T, LANE).astype(BF16)
    decay = jnp.exp(last)
    att = jnp.where(mask, _dot_nt(qs, ks), 0.0)
    o = _dot(att.astype(BF16), v)
    kv = [lax.dot_general(v[n * c:(n + 1) * c], kd[n * c:(n + 1) * c], (((0,), (0,)), ((), ())),
                          preferred_element_type=F32) for n in range(nchunk)]
    st = st_ref[...]
    o_inter = [None] * nchunk
    for n in (range(nchunk - 1, -1, -1) if rev else range(nchunk)):
        o_inter[n] = _dot_nt(qd[n * c:(n + 1) * c], st.astype(BF16))
        st = st * decay[n] + kv[n]
    st_ref[...] = st
    o_ref[...] = (o + jnp.concatenate(o_inter, axis=0)).astype(o_ref.dtype)


def _hgrn_kernel(qf, ff, vf, qb, fb, vb, lbf, lbb, of, ob, stf, stb):
    @pl.when(pl.program_id(2) == 0)
    def _():
        stf[...] = jnp.zeros_like(stf)
        stb[...] = jnp.zeros_like(stb)

    _hgrn_dir(qf[...], ff[...], vf[...], lbf[0], stf, of, False)
    _hgrn_dir(qb[...], fb[...], vb[...], lbb[0], stb, ob, True)


def _hgrn(u, lbf, lbb):
    assert LC == T
    xt = L // T
    n_rows = u.shape[0]
    hh = A_HEADS

    def blk_f(b, n):
        return jnp.where(n == 0, B * xt + b, b * xt + n - 1)

    def blk_b(b, n):
        return jnp.where(n == 0, B * xt + b, b * xt + xt - n)

    def spec(blk, cb):
        return pl.BlockSpec((T, LANE), lambda b, h, n: (blk(b, n), cb * hh + h))

    lb_spec = pl.BlockSpec((1, 1, LANE), lambda b, h, n: (h, 0, 0))
    return pl.pallas_call(
        _hgrn_kernel,
        grid=(B, hh, xt + 1),
        in_specs=[spec(blk_f, 0), spec(blk_f, 1), spec(blk_f, 3),
                  spec(blk_b, 0), spec(blk_b, 2), spec(blk_b, 3),
                  lb_spec, lb_spec],
        out_specs=[spec(blk_f, 0), spec(blk_b, 0)],
        out_shape=[jax.ShapeDtypeStruct((n_rows, A_KEY), BF16)] * 2,
        scratch_shapes=[pltpu.VMEM((LANE, LANE), F32)] * 2,
        compiler_params=_cp("parallel", "parallel", "arbitrary"),
    )(u, u, u, u, u, u, lbf, lbb)


def _hgrn_out_kernel(of_ref, ob_ref, g_ref, w_ref, y_ref):
    for h in range(A_HEADS):
        sl = slice(h * LANE, (h + 1) * LANE)
        o = of_ref[:, sl].astype(F32) + ob_ref[:, sl].astype(F32)
        o = o * lax.rsqrt(jnp.mean(o * o, axis=-1, keepdims=True) + RMS_EPS) * w_ref[...]
        y_ref[:, sl] = (o * _silu(g_ref[:, sl].astype(F32))).astype(y_ref.dtype)


def _hgrn_out(o_f, o_b, u, gnorm_w):
    n_rows = o_f.shape[0]
    blk = pl.BlockSpec((T, A_KEY), lambda i: (i, 0))
    return pl.pallas_call(
        _hgrn_out_kernel,
        grid=(n_rows // T,),
        in_specs=[blk, blk,
                  pl.BlockSpec((T, A_KEY), lambda i: (i, 4)),
                  pl.BlockSpec((1, LANE), lambda i: (0, 0))],
        out_specs=blk,
        out_shape=jax.ShapeDtypeStruct((n_rows, A_KEY), BF16),
        compiler_params=_cp("parallel"),
    )(o_f, o_b, u, gnorm_w.reshape(1, LANE))


def _rope_perm(n):
    return np.concatenate([np.arange(0, n, 2), np.arange(1, n, 2)])


def _rope_cos_sin(rot_dim):
    rows = L // GRID_W
    row = jnp.repeat(jnp.arange(rows), GRID_W).astype(F32)
    col = jnp.tile(jnp.arange(GRID_W), rows).astype(F32)
    n_freq = rot_dim // 4
    inv = ROPE_THETA ** (-jnp.arange(n_freq, dtype=F32) / n_freq)
    ang = jnp.concatenate([row[:, None] * inv, col[:, None] * inv], axis=-1)
    return jnp.cos(ang), jnp.sin(ang)


def _rope_tables_pairs(scale):
    cos, sin = _rope_cos_sin(LANE)
    zero = jnp.zeros_like(sin)
    c = jnp.repeat(cos, 2, axis=-1) * scale
    s1 = jnp.stack([-sin, zero], axis=-1).reshape(L, LANE) * scale
    s2 = jnp.stack([zero, sin], axis=-1).reshape(L, LANE) * scale
    return c, s1, s2


def _rope_tables(rot_dim, with_ctx):
    half = rot_dim // 2
    cos, sin = _rope_cos_sin(rot_dim)
    zpad = jnp.zeros((L, LANE - rot_dim), F32)
    zh = jnp.zeros((L, half), F32)
    c = jnp.concatenate([cos, cos, zpad], axis=-1)
    s1 = jnp.concatenate([-sin, zh, zpad], axis=-1)
    s2 = jnp.concatenate([zh, sin, zpad], axis=-1)
    if with_ctx:
        ident = jnp.ones((T, LANE), F32)
        zero = jnp.zeros((T, LANE), F32)
        c = jnp.concatenate([ident, c], axis=0)
        s1 = jnp.concatenate([zero, s1], axis=0)
        s2 = jnp.concatenate([zero, s2], axis=0)
    return c, s1, s2


def _rope_apply(x, c, s1, s2, half):
    return x * c + pltpu.roll(x, LANE - half, 1) * s1 + pltpu.roll(x, half, 1) * s2


def _tab_blk(i):
    xt = L // T
    return jnp.where(i < B * xt, 1 + i % xt, 0)


def _krope_kernel(x_ref, c_ref, s1_ref, s2_ref, o_ref):
    x = x_ref[...].astype(F32)
    o_ref[...] = _rope_apply(x, c_ref[...], s1_ref[...], s2_ref[...], B_ROPE // 2).astype(o_ref.dtype)


def _krope(ub, tabs):
    n_rows = ub.shape[0]
    cb = ub.shape[1] // LANE - 1
    tspec = pl.BlockSpec((T, LANE), lambda i: (_tab_blk(i), 0))
    return pl.pallas_call(
        _krope_kernel,
        grid=(n_rows // T,),
        in_specs=[pl.BlockSpec((T, LANE), lambda i: (i, cb)), tspec, tspec, tspec],
        out_specs=pl.BlockSpec((T, LANE), lambda i: (i, 0)),
        out_shape=jax.ShapeDtypeStruct((n_rows, LANE), BF16),
        compiler_params=_cp("parallel"),
    )(ub, *tabs)


MLA_KV_CHUNK = 512
MLA_TQ = 1024


def _softmax_step(q, k, v1, m, acc):
    s = _dot_nt(q, k)
    m_new = jnp.maximum(m, jnp.max(s, axis=-1, keepdims=True))
    p = jnp.exp2((s - m_new).astype(BF16))
    acc = jnp.exp2(m - m_new) * acc + _dot(p, v1)
    return m_new, acc


def _softmax_init(rows):
    return jnp.full((rows, 1), NEG_BIG, F32), jnp.zeros((rows, 2 * B_VDIM), F32)


def _softmax_finish(acc):
    return acc[:, :B_VDIM] / acc[:, B_VDIM:B_VDIM + 1]


def _mla_attn_kernel(q_ref, c_ref, s1_ref, s2_ref, kxn_ref, vx_ref, krx_ref, kcn_ref, vc_ref, krc_ref,
                     o_ref, kfx, kfc, vfx, vfc):
    @pl.when(pl.program_id(2) == 0)
    def _():
        kfx[:, :B_NOPE] = kxn_ref[...]
        kfx[:, B_NOPE:] = krx_ref[...]
        kfc[:, :B_NOPE] = kcn_ref[...]
        kfc[:, B_NOPE:] = krc_ref[...]
        vfx[:, :B_VDIM] = vx_ref[...]
        vfx[:, B_VDIM:] = jnp.ones((L, B_VDIM), BF16)
        vfc[:, :B_VDIM] = vc_ref[...]
        vfc[:, B_VDIM:] = jnp.ones((LC, B_VDIM), BF16)

    qr = _rope_apply(q_ref[:, B_NOPE:].astype(F32), c_ref[...], s1_ref[...], s2_ref[...], B_ROPE // 2)
    q = jnp.concatenate([q_ref[:, :B_NOPE], qr.astype(BF16)], axis=1)
    m, acc = _softmax_step(q, kfc[...], vfc[...], *_softmax_init(MLA_TQ))
    for n in range(L // MLA_KV_CHUNK):
        sl = slice(n * MLA_KV_CHUNK, (n + 1) * MLA_KV_CHUNK)
        m, acc = _softmax_step(q, kfx[sl, :], vfx[sl, :], m, acc)
    o_ref[...] = _softmax_finish(acc).astype(o_ref.dtype)


def _mla_ctx_kernel(q_ref, kcn_ref, vc_ref, krc_ref, o_ref):
    k = jnp.concatenate([kcn_ref[...], krc_ref[...]], axis=1)
    v1 = jnp.concatenate([vc_ref[...], jnp.ones((LC, B_VDIM), BF16)], axis=1)
    m, acc = _softmax_step(q_ref[...], k, v1, *_softmax_init(LC))
    o_ref[...] = _softmax_finish(acc).astype(o_ref.dtype)


def _mla_attn(q, kv, kr, tabs):
    xq = L // MLA_TQ
    ctx0 = B * L // LC
    tspec = pl.BlockSpec((MLA_TQ, LANE), lambda b, h, qi: (qi, 0))
    kc_spec = pl.BlockSpec((LC, LANE), lambda b, h, *_: (ctx0 + b, 2 * h))
    vc_spec = pl.BlockSpec((LC, LANE), lambda b, h, *_: (ctx0 + b, 2 * h + 1))
    krc_spec = pl.BlockSpec((LC, LANE), lambda b, h, *_: (ctx0 + b, 0))
    y_x = pl.pallas_call(
        _mla_attn_kernel,
        grid=(B, B_HEADS, xq),
        in_specs=[pl.BlockSpec((MLA_TQ, B_QPAD), lambda b, h, qi: (b * xq + qi, h)),
                  tspec, tspec, tspec,
                  pl.BlockSpec((L, LANE), lambda b, h, qi: (b, 2 * h)),
                  pl.BlockSpec((L, LANE), lambda b, h, qi: (b, 2 * h + 1)),
                  pl.BlockSpec((L, LANE), lambda b, h, qi: (b, 0)),
                  kc_spec, vc_spec, krc_spec],
        out_specs=pl.BlockSpec((MLA_TQ, B_VDIM), lambda b, h, qi: (b * xq + qi, h)),
        out_shape=jax.ShapeDtypeStruct((B * L, B_HEADS * B_VDIM), BF16),
        scratch_shapes=[pltpu.VMEM((L, B_QPAD), BF16), pltpu.VMEM((LC, B_QPAD), BF16),
                        pltpu.VMEM((L, 2 * B_VDIM), BF16), pltpu.VMEM((LC, 2 * B_VDIM), BF16)],
        compiler_params=_cp("parallel", "parallel", "arbitrary"),
    )(q, *tabs, kv, kv, kr, kv, kv, kr)
    y_c = pl.pallas_call(
        _mla_ctx_kernel,
        grid=(B, B_HEADS),
        in_specs=[pl.BlockSpec((LC, B_QPAD), lambda b, h: (ctx0 + b, h)), kc_spec, vc_spec, krc_spec],
        out_specs=pl.BlockSpec((LC, B_VDIM), lambda b, h: (b, h)),
        out_shape=jax.ShapeDtypeStruct((B * LC, B_HEADS * B_VDIM), BF16),
        compiler_params=_cp("parallel", "parallel"),
    )(q, kv, kv, kr)
    return jnp.concatenate([y_x, y_c], axis=0)


def _rope_qk_kernel(x_ref, c_ref, s1_ref, s2_ref, o_ref):
    c, s1, s2 = c_ref[...], s1_ref[...], s2_ref[...]
    for h in range(x_ref.shape[1] // LANE):
        sl = slice(h * LANE, (h + 1) * LANE)
        o_ref[:, sl] = _rope_apply(x_ref[:, sl].astype(F32), c, s1, s2, 1).astype(o_ref.dtype)


def _rope_qk(ux):
    width = C_Q + C_KV
    tn = C_KV
    xt = L // T
    nq = C_Q // tn
    tabs = [jnp.concatenate([tq, tk], axis=0)
            for tq, tk in zip(_rope_tables_pairs(HEAD_DIM ** -0.5), _rope_tables_pairs(1.0))]
    tspec = pl.BlockSpec((T, LANE), lambda i, j: (i % xt + jnp.where(j < nq, 0, xt), 0))
    return pl.pallas_call(
        _rope_qk_kernel,
        grid=(B * xt, width // tn),
        in_specs=[pl.BlockSpec((T, tn), lambda i, j: (i, j)), tspec, tspec, tspec],
        out_specs=pl.BlockSpec((T, tn), lambda i, j: (i, j)),
        out_shape=jax.ShapeDtypeStruct((B * L, width), BF16),
        compiler_params=_cp("parallel", "parallel"),
    )(ux, *tabs)


def _win_kernel(sink_ref, q_ref, kx_ref, vx_ref, kc_ref, vc_ref, o_ref):
    g = pl.program_id(1)
    t = pl.program_id(2)
    span = 2 * T
    start = pl.multiple_of(jnp.clip(t * T - C_WINDOW, 0, L - span), C_WINDOW)
    kw = kx_ref[pl.ds(start, span), :]
    vw = vx_ref[pl.ds(start, span), :]
    qpos = t * T + lax.broadcasted_iota(jnp.int32, (T, span), 0)
    kpos = start + lax.broadcasted_iota(jnp.int32, (T, span), 1)
    valid = jnp.abs(qpos - kpos) <= C_WINDOW
    kc = kc_ref[...]
    vc = vc_ref[...]
    for r in range(C_GROUP):
        sl = slice(r * HEAD_DIM, (r + 1) * HEAD_DIM)
        q = q_ref[:, sl]
        sw = jnp.where(valid, _dot_nt(q, kw), NEG_BIG)
        sc = _dot_nt(q, kc)
        sk = sink_ref[g * C_GROUP + r]
        m = jnp.maximum(jnp.maximum(jnp.max(sw, axis=-1, keepdims=True),
                                    jnp.max(sc, axis=-1, keepdims=True)), sk)
        pw = jnp.exp(sw - m)
        pc = jnp.exp(sc - m)
        l = jnp.sum(pw, axis=-1, keepdims=True) + jnp.sum(pc, axis=-1, keepdims=True) + jnp.exp(sk - m)
        o = _dot(pw.astype(BF16), vw) + _dot(pc.astype(BF16), vc)
        o_ref[:, sl] = (o / l).astype(o_ref.dtype)


def _win_attn(qk, ux, sink):
    assert T == 2 * C_WINDOW
    xt = L // T
    ctx0 = B * L // LC
    kcol = C_Q // LANE
    vcol = (C_Q + C_KV) // LANE
    gw = C_GROUP * HEAD_DIM
    return pl.pallas_call(
        _win_kernel,
        grid=(B, C_KV_HEADS, xt),
        in_specs=[pl.BlockSpec(memory_space=pltpu.SMEM),
                  pl.BlockSpec((T, gw), lambda b, g, t: (b * xt + t, g)),
                  pl.BlockSpec((L, LANE), lambda b, g, t: (b, kcol + g)),
                  pl.BlockSpec((L, LANE), lambda b, g, t: (b, vcol + g)),
                  pl.BlockSpec((LC, LANE), lambda b, g, t: (ctx0 + b, kcol + g)),
                  pl.BlockSpec((LC, LANE), lambda b, g, t: (ctx0 + b, vcol + g))],
        out_specs=pl.BlockSpec((T, gw), lambda b, g, t: (b * xt + t, g)),
        out_shape=jax.ShapeDtypeStruct((B * L, C_Q), BF16),
        compiler_params=_cp("parallel", "parallel", "arbitrary"),
    )(sink, qk, qk, ux, ux, ux)


CONV_TN = 512
SUBLANE = 8


def _conv_kernel(bg_ref, cg_ref, hh_ref, cgp_ref, hhp_ref, cgn_ref, hhn_ref, w_ref, o_ref):
    xt = L // T
    t = pl.program_id(0) % xt
    z = cg_ref[...].astype(F32) * hh_ref[...].astype(F32)
    zp = cgp_ref[SUBLANE - 1:SUBLANE, :].astype(F32) * hhp_ref[SUBLANE - 1:SUBLANE, :].astype(F32)
    zn = cgn_ref[0:1, :].astype(F32) * hhn_ref[0:1, :].astype(F32)
    zp = jnp.where(t > 0, zp, 0.0)
    zn = jnp.where(t < xt - 1, zn, 0.0)
    rows = lax.broadcasted_iota(jnp.int32, (T, CONV_TN), 0)
    z_prev = jnp.where(rows == 0, zp, pltpu.roll(z, 1, 0))
    z_next = jnp.where(rows == T - 1, zn, pltpu.roll(z, T - 1, 0))
    y = w_ref[0:1, :] * z_prev + w_ref[1:2, :] * z + w_ref[2:3, :] * z_next
    o_ref[...] = (bg_ref[...].astype(F32) * y).astype(o_ref.dtype)


def _conv(ux, conv_w):
    xt = L // T
    nx = B * xt
    c0 = C_IN // CONV_TN
    nc = D_CH // CONV_TN
    rb = T // SUBLANE

    def main(cb):
        return pl.BlockSpec((T, CONV_TN), lambda i, j: (i, c0 + cb * nc + j))

    def prev(cb):
        return pl.BlockSpec((SUBLANE, CONV_TN), lambda i, j: (jnp.maximum(i * rb - 1, 0), c0 + cb * nc + j))

    def nxt(cb):
        return pl.BlockSpec((SUBLANE, CONV_TN),
                            lambda i, j: (jnp.minimum((i + 1) * rb, nx * rb - 1), c0 + cb * nc + j))

    return pl.pallas_call(
        _conv_kernel,
        grid=(nx, nc),
        in_specs=[main(0), main(1), main(2), prev(1), prev(2), nxt(1), nxt(2),
                  pl.BlockSpec((D_CONV, CONV_TN), lambda i, j: (0, j))],
        out_specs=pl.BlockSpec((T, CONV_TN), lambda i, j: (i, j)),
        out_shape=jax.ShapeDtypeStruct((B * L, D_CH), BF16),
        compiler_params=_cp("parallel", "parallel"),
    )(ux, ux, ux, ux, ux, ux, ux, conv_w)


def _router_kernel(h_ref, wr_ref, b_ref, i_ref, g_ref):
    tm = h_ref.shape[0]
    scores = _sigmoid(_dot_nt(wr_ref[...], h_ref[...]))
    biased = scores + b_ref[...]
    ng = N_GROUPS
    sl = [biased[j * ng:(j + 1) * ng] for j in range(GROUP_SIZE)]
    sc = [scores[j * ng:(j + 1) * ng] for j in range(GROUP_SIZE)]
    m1 = sl[0]
    m2 = jnp.full_like(m1, -jnp.inf)
    for v in sl[1:]:
        m2 = jnp.maximum(m2, jnp.minimum(m1, v))
        m1 = jnp.maximum(m1, v)
    gs = m1 + m2
    gidx = lax.broadcasted_iota(jnp.int32, (ng, tm), 0)
    rank = jnp.zeros((ng, tm), jnp.int32)
    for k in range(1, ng):
        r = pltpu.roll(gs, k, 0)
        rank += jnp.where((r > gs) | ((r == gs) & (gidx >= k)), 1, 0)
    keep = rank < TOPK_GROUPS
    ms = [jnp.where(keep, v, -jnp.inf) for v in sl]
    rolled = [[ms[j] if k == 0 else pltpu.roll(ms[j], k, 0) for k in range(ng)] for j in range(GROUP_SIZE)]
    tot = jnp.zeros((ng, tm), F32)
    picked = []
    ranks = []
    for j in range(GROUP_SIZE):
        cnt = jnp.zeros((ng, tm), jnp.int32)
        for jp in range(GROUP_SIZE):
            for k in range(ng):
                r = rolled[jp][k]
                if k == 0:
                    if jp == j:
                        continue
                    beats = (r >= ms[j]) if jp < j else (r > ms[j])
                else:
                    beats = (r > ms[j]) | ((r == ms[j]) & (gidx >= k))
                cnt += jnp.where(beats, 1, 0)
        w = jnp.where(cnt < TOP_K, sc[j], 0.0)
        picked.append(w)
        ranks.append(cnt)
        tot = tot + w
    tot = jnp.sum(tot, axis=0, keepdims=True)
    wn = [p / tot * ROUTE_SCALE for p in picked]
    eid = [(gidx * GROUP_SIZE + j).astype(F32) for j in range(GROUP_SIZE)]
    for k in range(TOP_K):
        ik = jnp.zeros((ng, tm), F32)
        wk = jnp.zeros((ng, tm), F32)
        for j in range(GROUP_SIZE):
            hit = ranks[j] == k
            ik += jnp.where(hit, eid[j], 0.0)
            wk += jnp.where(hit, wn[j], 0.0)
        i_ref[k:k + 1, :] = jnp.sum(ik, axis=0, keepdims=True).astype(jnp.int32)
        g_ref[k:k + 1, :] = jnp.sum(wk, axis=0, keepdims=True)


def _router(h, wr_t, bias, nrows):
    out = pl.BlockSpec((TOP_K, TM), lambda i: (0, i))
    return pl.pallas_call(
        _router_kernel,
        grid=(nrows // TM,),
        in_specs=[pl.BlockSpec((TM, D), lambda i: (i, 0)),
                  pl.BlockSpec((N_EXPERTS, D), lambda i: (0, 0)),
                  pl.BlockSpec((N_EXPERTS, 1), lambda i: (0, 0))],
        out_specs=[out, out],
        out_shape=[jax.ShapeDtypeStruct((TOP_K, nrows), jnp.int32),
                   jax.ShapeDtypeStruct((TOP_K, nrows), F32)],
        compiler_params=_cp("parallel"),
    )(h, wr_t, bias)


def _router_order():
    r = np.arange(N_EXPERTS)
    return (r % N_GROUPS) * GROUP_SIZE + r // N_GROUPS


def _glu_kernel(x_ref, wg_ref, wu_ref, wd_ref, o_ref):
    x = x_ref[...]
    hid = _silu(_dot(x, wg_ref[...])) * _dot(x, wu_ref[...])
    o_ref[...] = _dot(hid.astype(BF16), wd_ref[...]).astype(o_ref.dtype)


def _glu(h, wg, wu, wd, nrows):
    return pl.pallas_call(
        _glu_kernel,
        grid=(nrows // TM,),
        in_specs=[pl.BlockSpec((TM, D), lambda i: (i, 0)),
                  pl.BlockSpec((D, D_EXPERT), lambda i: (0, 0)),
                  pl.BlockSpec((D, D_EXPERT), lambda i: (0, 0)),
                  pl.BlockSpec((D_EXPERT, D), lambda i: (0, 0))],
        out_specs=pl.BlockSpec((TM, D), lambda i: (i, 0)),
        out_shape=jax.ShapeDtypeStruct((nrows, D), BF16),
        compiler_params=_cp("parallel"),
    )(h, wg, wu, wd)


TME = 256
TD = 512
TC = 128


def _route_plan(idx, nrows):
    nt = (nrows * TOP_K) // TME + N_EXPERTS
    onehot = idx[:, :, None] == jnp.arange(N_EXPERTS, dtype=jnp.int32)[None, None, :]
    cs = jnp.cumsum(jnp.any(onehot, axis=1).astype(jnp.int32), axis=0)
    counts = cs[-1]
    padded = ((counts + TME - 1) // TME) * TME
    ends = jnp.cumsum(padded)
    off = ends - padded
    dest = jnp.sum(jnp.where(onehot, (off[None, :] + cs - 1)[:, None, :], 0), axis=2).astype(jnp.int32)
    tiles = jnp.arange(nt, dtype=jnp.int32)
    used = ends[-1] // TME
    t_expert = jnp.sum((ends // TME)[None, :] <= tiles[:, None], axis=1)
    t_expert = jnp.minimum(t_expert, N_EXPERTS - 1).astype(jnp.int32)
    t_valid = (tiles < used).astype(jnp.int32)
    t_first = ((tiles * TME == off[t_expert]) & (tiles < used)).astype(jnp.int32)
    t_src = jnp.minimum(tiles, used - 1).astype(jnp.int32)
    return dict(nt=nt, dest=dest, used=used.reshape(1).astype(jnp.int32),
                pad_start=(off + counts).astype(jnp.int32),
                pad_cnt=(padded - counts).astype(jnp.int32),
                t_expert=t_expert, t_valid=t_valid, t_first=t_first, t_src=t_src)


def _slab(ref, row):
    return ref.at[pl.ds(pl.multiple_of(row * NP, NP), NP), :]


def _dispatch_kernel(ps_ref, pc_ref, used_ref, dest_ref, hp_ref, xs_ref, sem, *, nt):
    i = pl.program_id(0)

    def body(j, c):
        for k in range(TOP_K):
            pltpu.make_async_copy(_slab(hp_ref, j), _slab(xs_ref, dest_ref[0, 0, j * TOP_K + k]),
                                  sem).start(priority=k % 2)
        return c

    lax.fori_loop(0, TD, body, 0)
    for k in range(TOP_K):
        pltpu.make_async_copy(hp_ref, xs_ref.at[pl.ds(0, TD * NP), :], sem).wait()

    @pl.when(i == 0)
    def _():
        def pad_copy(e):
            n = pc_ref[e] * NP
            return pltpu.make_async_copy(hp_ref.at[pl.ds(0, n), :],
                                         xs_ref.at[pl.ds(pl.multiple_of(ps_ref[e] * NP, NP), n), :], sem)

        def tail_copy(t):
            n = TME * NP
            return pltpu.make_async_copy(hp_ref.at[pl.ds(0, n), :],
                                         xs_ref.at[pl.ds(pl.multiple_of(t * n, n), n), :], sem)

        def over(lo, hi, copy, pred):
            def start(e, c):
                @pl.when(pred(e))
                def _():
                    copy(e).start()
                return c

            def wait(e, c):
                @pl.when(pred(e))
                def _():
                    copy(e).wait()
                return c

            lax.fori_loop(lo, hi, start, 0)
            lax.fori_loop(lo, hi, wait, 0)

        over(0, N_EXPERTS, pad_copy, lambda e: pc_ref[e] > 0)
        over(0, nt, tail_copy, lambda t: t >= used_ref[0])


def _dispatch(hp, plan, nrows):
    dest = plan["dest"].reshape(nrows // TD, 1, TD * TOP_K)
    return pl.pallas_call(
        functools.partial(_dispatch_kernel, nt=plan["nt"]),
        grid_spec=pltpu.PrefetchScalarGridSpec(
            num_scalar_prefetch=3,
            grid=(nrows // TD,),
            in_specs=[pl.BlockSpec((1, 1, TD * TOP_K), lambda i, ps, pc, us: (i, 0, 0), memory_space=pltpu.SMEM),
                      pl.BlockSpec((TD * NP, LANE), lambda i, ps, pc, us: (i, 0))],
            out_specs=pl.BlockSpec(memory_space=pl.ANY),
            scratch_shapes=[pltpu.SemaphoreType.DMA(())]),
        out_shape=jax.ShapeDtypeStruct((plan["nt"] * TME * NP, LANE), jnp.uint32),
        compiler_params=pltpu.CompilerParams(dimension_semantics=("arbitrary",), disable_bounds_checks=True,
                                             vmem_limit_bytes=VMEM_LIMIT_BYTES),
    )(plan["pad_start"], plan["pad_cnt"], plan["used"], dest, hp)


def _expert_kernel(te_ref, tv_ref, tf_ref, ts_ref, x_ref, wg_ref, wu_ref, wd_ref, o_ref, wgu_b, wd_b):
    i = pl.program_id(0)

    @pl.when(tv_ref[i] == 0)
    def _():
        o_ref[...] = jnp.zeros_like(o_ref)

    @pl.when(tv_ref[i] == 1)
    def _():
        @pl.when(tf_ref[i] == 1)
        def _():
            wgu_b[:, :D_EXPERT] = wg_ref[0, 0].astype(BF16)
            wgu_b[:, D_EXPERT:] = wu_ref[0, 0].astype(BF16)
            wd_b[...] = wd_ref[0, 0].astype(BF16)

        pieces = []
        for p in range(NP):
            lo, hi = _unpack2(x_ref[pl.ds(p, TME, stride=NP), :])
            pieces += [lo.astype(BF16), hi.astype(BF16)]
        a = _dot(jnp.concatenate(pieces, axis=1), wgu_b[...])
        hid = _silu(a[:, :D_EXPERT]) * a[:, D_EXPERT:]
        y = _dot(hid.astype(BF16), wd_b[...])
        for p in range(NP):
            o_ref[pl.ds(p, TME, stride=NP), :] = _pack2(y[:, p * PACK:p * PACK + LANE],
                                                        y[:, p * PACK + LANE:(p + 1) * PACK])


def _experts(xs, plan, wg, wu, wd, layer):
    nt = plan["nt"]
    rows = pl.BlockSpec((TME * NP, LANE), lambda i, te, tv, tf, ts: (i, 0))
    return pl.pallas_call(
        _expert_kernel,
        grid_spec=pltpu.PrefetchScalarGridSpec(
            num_scalar_prefetch=4,
            grid=(nt,),
            in_specs=[pl.BlockSpec((TME * NP, LANE), lambda i, te, tv, tf, ts: (ts[i], 0)),
                      pl.BlockSpec((1, 1, D, D_EXPERT), lambda i, te, tv, tf, ts: (layer, te[i], 0, 0)),
                      pl.BlockSpec((1, 1, D, D_EXPERT), lambda i, te, tv, tf, ts: (layer, te[i], 0, 0)),
                      pl.BlockSpec((1, 1, D_EXPERT, D), lambda i, te, tv, tf, ts: (layer, te[i], 0, 0))],
            out_specs=rows,
            scratch_shapes=[pltpu.VMEM((D, 2 * D_EXPERT), BF16), pltpu.VMEM((D_EXPERT, D), BF16)]),
        out_shape=jax.ShapeDtypeStruct((nt * TME * NP, LANE), jnp.uint32),
        compiler_params=pltpu.CompilerParams(dimension_semantics=("arbitrary",),
                                             vmem_limit_bytes=EXPERT_VMEM_LIMIT_BYTES),
    )(plan["t_expert"], plan["t_valid"], plan["t_first"], plan["t_src"], xs, wg, wu, wd)


def _gather_pitch():
    tiles = -(-NP // SUBLANE)
    return (tiles + 1 + tiles % 2) * SUBLANE


def _combine_kernel(dcur_ref, dnxt_ref, w_ref, r_ref, sh_ref, g_ref, ys_ref, o_ref, buf0, buf1, sem):
    i = pl.program_id(0)
    last = pl.num_programs(0) - 1

    pitch = _gather_pitch()

    def issue(d_ref, buf, slot):
        def body(j, c):
            for k in range(TOP_K):
                dst = buf.at[pl.ds(pl.multiple_of((k * TC + j) * pitch, SUBLANE), NP), :]
                pltpu.make_async_copy(_slab(ys_ref, d_ref[0, 0, j * TOP_K + k]), dst,
                                      sem.at[slot]).start(priority=k % 2)
            return c

        lax.fori_loop(0, TC, body, 0)

    def drain(buf, slot):
        nrow = TC * TOP_K * NP
        pltpu.make_async_copy(ys_ref.at[pl.ds(0, nrow), :], buf.at[pl.ds(0, nrow), :], sem.at[slot]).wait()

    def reduce(buf):
        gates = [w_ref[:, k:k + 1] for k in range(TOP_K)]
        gmod = g_ref[0]
        for p in range(NP):
            lo_sl = slice(p * PACK, p * PACK + LANE)
            hi_sl = slice(p * PACK + LANE, (p + 1) * PACK)
            lo_acc = sh_ref[:, lo_sl].astype(F32)
            hi_acc = sh_ref[:, hi_sl].astype(F32)
            for k in range(TOP_K):
                lo, hi = _unpack2(buf[pl.ds(k * TC * pitch + p, TC, stride=pitch), :])
                lo_acc += gates[k] * lo
                hi_acc += gates[k] * hi
            o_ref[:, lo_sl] = r_ref[:, lo_sl] + gmod[:, lo_sl] * lo_acc
            o_ref[:, hi_sl] = r_ref[:, hi_sl] + gmod[:, hi_sl] * hi_acc

    @pl.when(i == 0)
    def _():
        issue(dcur_ref, buf0, 0)

    @pl.when(i % 2 == 0)
    def _():
        @pl.when(i < last)
        def _():
            issue(dnxt_ref, buf1, 1)

        drain(buf0, 0)
        reduce(buf0)

    @pl.when(i % 2 == 1)
    def _():
        @pl.when(i < last)
        def _():
            issue(dnxt_ref, buf0, 0)

        drain(buf1, 1)
        reduce(buf1)


def _combine(resid, ys, shared, w, plan, gate, nrows):
    nsteps = nrows // TC
    dest = plan["dest"].reshape(nsteps, 1, TC * TOP_K)
    blk = pl.BlockSpec((TC, D), lambda i: (i, 0))
    return pl.pallas_call(
        _combine_kernel,
        grid=(nsteps,),
        in_specs=[pl.BlockSpec((1, 1, TC * TOP_K), lambda i: (i, 0, 0), memory_space=pltpu.SMEM),
                  pl.BlockSpec((1, 1, TC * TOP_K), lambda i: (jnp.minimum(i + 1, nsteps - 1), 0, 0),
                               memory_space=pltpu.SMEM),
                  pl.BlockSpec((TC, TOP_K), lambda i: (i, 0)),
                  blk, blk,
                  pl.BlockSpec((1, 1, D), lambda i: (_mod_row(i, TC), 0, 0)),
                  pl.BlockSpec(memory_space=pl.ANY)],
        out_specs=blk,
        out_shape=jax.ShapeDtypeStruct((nrows, D), F32),
        scratch_shapes=[pltpu.VMEM((TC * TOP_K * _gather_pitch(), LANE), jnp.uint32),
                        pltpu.VMEM((TC * TOP_K * _gather_pitch(), LANE), jnp.uint32),
                        pltpu.SemaphoreType.DMA((2,))],
        compiler_params=pltpu.CompilerParams(dimension_semantics=("arbitrary",),
                                             vmem_limit_bytes=VMEM_LIMIT_BYTES, disable_bounds_checks=True),
    )(dest, dest, w, resid, shared, gate, ys)


def _moe(s, h, hp, gate, nrows, layer, router_w, router_bias, wg, wu, wd, sg, su, sd):
    order = _router_order()
    wr_t = router_w.T[order].astype(BF16)
    bias = router_bias[order].reshape(N_EXPERTS, 1).astype(F32)
    idx_t, w_t = _router(h, wr_t, bias, nrows)
    plan = _route_plan(idx_t.T, nrows)
    xs = _dispatch(hp, plan, nrows)
    ys = _experts(xs, plan, wg, wu, wd, layer)
    shared = _glu(h, sg.astype(BF16), su.astype(BF16), sd.astype(BF16), nrows)
    return _combine(s, ys, shared, w_t.T, plan, gate, nrows)


def _mla_in_weights(w_in):
    wb = w_in[:, A_IN:]
    kr = wb[:, B_Q_RANK + B_KV_RANK:][:, _rope_perm(B_ROPE)]
    pad = jnp.zeros((D, LANE - B_ROPE), w_in.dtype)
    return jnp.concatenate([wb[:, :B_Q_RANK + B_KV_RANK], kr, pad], axis=1).astype(BF16)


def _mla_uq_weights(w_uq):
    scale = (B_NOPE + B_ROPE) ** -0.5 * np.log2(np.e)
    w = (w_uq * scale).reshape(B_Q_RANK, B_HEADS, B_NOPE + B_ROPE)
    rope = w[:, :, B_NOPE:][:, :, _rope_perm(B_ROPE)]
    pad = jnp.zeros((B_Q_RANK, B_HEADS, B_QPAD - B_NOPE - B_ROPE), w.dtype)
    return jnp.concatenate([w[:, :, :B_NOPE], rope, pad], axis=-1).reshape(B_Q_RANK, B_HEADS * B_QPAD).astype(BF16)


def kernel(x, c, ctx, c_ctx, ada_w, ada_b, norm_mix_w, norm_ffn_w, ab_w_in, ab_w_out, hgrn_lb_logits, hgrn_gnorm_w, mla_q_norm_w, mla_w_uq, mla_kv_norm_w, mla_w_ukv, cd_w_in, cd_w_out, gqa_sink, conv_w, router_w, router_bias, exp_w_gate, exp_w_up, exp_w_down, sh_w_gate, sh_w_up, sh_w_down, final_norm_w):
    nx = B * L
    n = nx + B * LC
    s = jnp.concatenate([x.reshape(nx, D), ctx.reshape(B * LC, D)], axis=0)
    lb_all = jnp.cumsum(jax.nn.softmax(hgrn_lb_logits.astype(F32), axis=0), axis=0)
    cond = jnp.concatenate([c, c_ctx[None, :], jnp.zeros((8 - B - 1, D), F32)], axis=0)

    def modulation(i):
        m = _ada(cond, ada_w, ada_b[i].reshape(1, N_MOD * D), i).reshape(8, N_MOD, D)
        return [m[:B + 1, k, :].reshape(B + 1, 1, D) for k in range(N_MOD)]

    def moe_args(i):
        return (i, router_w[i], router_bias[i], exp_w_gate, exp_w_up, exp_w_down,
                sh_w_gate[i], sh_w_up[i], sh_w_down[i])

    mod = modulation(0)
    h = _norm_mod(s, norm_mix_w[0], mod[1], mod[0], n)
    ua = _matmul(h, ab_w_in[0][:, :A_IN].astype(BF16), 1024)
    ub = _matmul(h, _mla_in_weights(ab_w_in[0]), B_Q_RANK + B_KV_RANK + LANE)
    lb = lb_all[0].reshape(2, A_HEADS, 1, A_DK)
    o_f, o_b = _hgrn(ua, lb[0], lb[1])
    ya = _hgrn_out(o_f, o_b, ua, hgrn_gnorm_w[0])
    tabs = _rope_tables(B_ROPE, True)
    q = _norm_mm(ub, 0, B_Q_RANK, mla_q_norm_w[0], _mla_uq_weights(mla_w_uq[0]), 1024)
    kv = _norm_mm(ub, B_Q_RANK // B_KV_RANK, B_KV_RANK, mla_kv_norm_w[0], mla_w_ukv[0].astype(BF16), 1024)
    kr = _krope(ub, tabs)
    yb = _mla_attn(q, kv, kr, tuple(t[T:] for t in tabs))
    s = _mm2_res(ya, yb, ab_w_out[0].astype(BF16), s, mod[2], n)
    h, hp = _norm_mod_pack(s, norm_ffn_w[0], mod[4], mod[3], n)
    s = _moe(s, h, hp, mod[5], n, *moe_args(0))

    mod = modulation(1)
    h = _norm_mod(s, norm_mix_w[1], mod[1], mod[0], n)
    ux = _matmul(h, cd_w_in[0].astype(BF16), 1024)
    qk = _rope_qk(ux)
    att = _win_attn(qk, ux, gqa_sink[0].astype(F32))
    cv = _conv(ux, conv_w[0])
    s = _mm2_res(att, cv, cd_w_out[0].astype(BF16), s, mod[2], nx)
    h, hp = _norm_mod_pack(s, norm_ffn_w[1], mod[4], mod[3], nx)
    s = _moe(s, h, hp, mod[5], nx, *moe_args(1))
    return _rms(s, final_norm_w).reshape(B, L, D)
```

```python
import functools

import jax
import jax.numpy as jnp
import numpy as np
from jax import lax
from jax.experimental import pallas as pl
from jax.experimental.pallas import tpu as pltpu

F32 = jnp.float32
BF16 = jnp.bfloat16

D = 4096
B = 2
L = 8192
LC = 256
GRID_W = 64
RMS_EPS = 1e-6
ROPE_THETA = 10000.0
N_MOD = 6

A_HEADS = 16
A_DK = 128
A_KEY = A_HEADS * A_DK
A_CHUNK = 32
A_IN = 5 * A_KEY

B_HEADS = 16
B_Q_RANK = 1024
B_KV_RANK = 512
B_NOPE = 128
B_ROPE = 64
B_VDIM = 128
B_QPAD = 256

C_HEADS = 16
C_KV_HEADS = 4
C_GROUP = C_HEADS // C_KV_HEADS
C_WINDOW = 128
HEAD_DIM = 128
C_Q = C_HEADS * HEAD_DIM
C_KV = C_KV_HEADS * HEAD_DIM
C_IN = C_Q + 2 * C_KV
D_CH = 2048
D_CONV = 3

N_EXPERTS = 64
TOP_K = 8
N_GROUPS = 8
TOPK_GROUPS = 4
GROUP_SIZE = N_EXPERTS // N_GROUPS
D_EXPERT = 256
ROUTE_SCALE = 2.5

T = 256
TM = 512
LANE = 128
VMEM_LIMIT_BYTES = 48 * 1024 * 1024
EXPERT_VMEM_LIMIT_BYTES = 56 * 1024 * 1024
NEG_BIG = -1e30
PACK = 2 * LANE
NP = D // PACK


def _cp(*sem):
    return pltpu.CompilerParams(dimension_semantics=sem, vmem_limit_bytes=VMEM_LIMIT_BYTES)


def _sigmoid(x):
    return 1.0 / (1.0 + jnp.exp(-x))


def _silu(x):
    return x * _sigmoid(x)


def _dot(a, b):
    return jnp.dot(a, b, preferred_element_type=F32)


def _dot_nt(a, b):
    return lax.dot_general(a, b, (((1,), (1,)), ((), ())), preferred_element_type=F32)


def _mod_row(i, tile):
    return jnp.where(i < (B * L) // tile, i // (L // tile), B)


def _ada_kernel(a_ref, w_ref, b_ref, o_ref):
    a = _silu(a_ref[...]).astype(BF16)
    o_ref[...] = _dot(a, w_ref[0].astype(BF16)) + b_ref[...]


def _ada(a, w, bias, layer):
    n = w.shape[2]
    tn = 512
    return pl.pallas_call(
        _ada_kernel,
        grid=(n // tn,),
        in_specs=[pl.BlockSpec((8, D), lambda j: (0, 0)),
                  pl.BlockSpec((1, D, tn), lambda j: (layer, 0, j)),
                  pl.BlockSpec((1, tn), lambda j: (0, j))],
        out_specs=pl.BlockSpec((8, tn), lambda j: (0, j)),
        out_shape=jax.ShapeDtypeStruct((8, n), F32),
        compiler_params=_cp("parallel"),
    )(a, w, bias)


def _norm_mod_kernel(s_ref, w_ref, sc_ref, sh_ref, o_ref):
    x = s_ref[...]
    y = x * lax.rsqrt(jnp.mean(x * x, axis=-1, keepdims=True) + RMS_EPS) * w_ref[...]
    o_ref[...] = (y * (1.0 + sc_ref[0]) + sh_ref[0]).astype(o_ref.dtype)


def _norm_mod(s, w, scale, shift, nrows):
    mod_spec = pl.BlockSpec((1, 1, D), lambda i: (_mod_row(i, T), 0, 0))
    return pl.pallas_call(
        _norm_mod_kernel,
        grid=(nrows // T,),
        in_specs=[pl.BlockSpec((T, D), lambda i: (i, 0)),
                  pl.BlockSpec((1, D), lambda i: (0, 0)),
                  mod_spec, mod_spec],
        out_specs=pl.BlockSpec((T, D), lambda i: (i, 0)),
        out_shape=jax.ShapeDtypeStruct((nrows, D), BF16),
        compiler_params=_cp("parallel"),
    )(s, w.reshape(1, D), scale, shift)


def _norm_mod_split_kernel(x_ref, c_ref, w_ref, sc_ref, sh_ref, o_ref):
    x = jnp.where(pl.program_id(0) < (B * L) // T, x_ref[...], c_ref[...])
    y = x * lax.rsqrt(jnp.mean(x * x, axis=-1, keepdims=True) + RMS_EPS) * w_ref[...]
    o_ref[...] = (y * (1.0 + sc_ref[0]) + sh_ref[0]).astype(o_ref.dtype)


def _norm_mod_split(x, c, w, scale, shift):
    nxt = x.shape[0] // T
    nct = c.shape[0] // T
    mod_spec = pl.BlockSpec((1, 1, D), lambda i: (_mod_row(i, T), 0, 0))
    return pl.pallas_call(
        _norm_mod_split_kernel,
        grid=(nxt + nct,),
        in_specs=[pl.BlockSpec((T, D), lambda i: (jnp.minimum(i, nxt - 1), 0)),
                  pl.BlockSpec((T, D), lambda i: (jnp.maximum(i - nxt, 0), 0)),
                  pl.BlockSpec((1, D), lambda i: (0, 0)),
                  mod_spec, mod_spec],
        out_specs=pl.BlockSpec((T, D), lambda i: (i, 0)),
        out_shape=jax.ShapeDtypeStruct(((nxt + nct) * T, D), BF16),
        compiler_params=_cp("parallel"),
    )(x, c, w.reshape(1, D), scale, shift)


def _pack2(lo, hi):
    lo_b = lax.bitcast_convert_type(lo.astype(BF16).astype(F32), jnp.uint32) >> 16
    hi_b = lax.bitcast_convert_type(hi.astype(BF16).astype(F32), jnp.uint32)
    return hi_b | lo_b


def _unpack2(w):
    lo = lax.bitcast_convert_type(w << 16, F32)
    hi = lax.bitcast_convert_type(w & jnp.uint32(0xFFFF0000), F32)
    return lo, hi


def _norm_mod_pack_kernel(s_ref, w_ref, sc_ref, sh_ref, o_ref, p_ref):
    x = s_ref[...]
    y = x * lax.rsqrt(jnp.mean(x * x, axis=-1, keepdims=True) + RMS_EPS) * w_ref[...]
    y = y * (1.0 + sc_ref[0]) + sh_ref[0]
    o_ref[...] = y.astype(o_ref.dtype)
    for p in range(NP):
        p_ref[pl.ds(p, T, stride=NP), :] = _pack2(y[:, p * PACK:p * PACK + LANE], y[:, p * PACK + LANE:(p + 1) * PACK])


def _norm_mod_pack(s, w, scale, shift, nrows):
    mod_spec = pl.BlockSpec((1, 1, D), lambda i: (_mod_row(i, T), 0, 0))
    return pl.pallas_call(
        _norm_mod_pack_kernel,
        grid=(nrows // T,),
        in_specs=[pl.BlockSpec((T, D), lambda i: (i, 0)),
                  pl.BlockSpec((1, D), lambda i: (0, 0)),
                  mod_spec, mod_spec],
        out_specs=[pl.BlockSpec((T, D), lambda i: (i, 0)),
                   pl.BlockSpec((T * NP, LANE), lambda i: (i, 0))],
        out_shape=[jax.ShapeDtypeStruct((nrows, D), BF16),
                   jax.ShapeDtypeStruct((nrows * NP, LANE), jnp.uint32)],
        compiler_params=_cp("parallel"),
    )(s, w.reshape(1, D), scale, shift)


def _rms_kernel(s_ref, w_ref, o_ref):
    x = s_ref[...]
    o_ref[...] = x * lax.rsqrt(jnp.mean(x * x, axis=-1, keepdims=True) + RMS_EPS) * w_ref[...]


def _rms(s, w):
    nrows = s.shape[0]
    return pl.pallas_call(
        _rms_kernel,
        grid=(nrows // T,),
        in_specs=[pl.BlockSpec((T, D), lambda i: (i, 0)),
                  pl.BlockSpec((1, D), lambda i: (0, 0))],
        out_specs=pl.BlockSpec((T, D), lambda i: (i, 0)),
        out_shape=jax.ShapeDtypeStruct((nrows, D), F32),
        compiler_params=_cp("parallel"),
    )(s, w.reshape(1, D))


def _mm_kernel(a_ref, w_ref, o_ref):
    o_ref[...] = _dot(a_ref[...], w_ref[...]).astype(o_ref.dtype)


def _matmul(a, w, tn):
    m, k = a.shape
    n = w.shape[1]
    return pl.pallas_call(
        _mm_kernel,
        grid=(m // TM, n // tn),
        in_specs=[pl.BlockSpec((TM, k), lambda i, j: (i, 0)),
                  pl.BlockSpec((k, tn), lambda i, j: (0, j))],
        out_specs=pl.BlockSpec((TM, tn), lambda i, j: (i, j)),
        out_shape=jax.ShapeDtypeStruct((m, n), BF16),
        compiler_params=_cp("parallel", "arbitrary"),
    )(a, w)


def _norm_mm_kernel(a_ref, nw_ref, w_ref, o_ref):
    x = a_ref[...].astype(F32)
    y = x * lax.rsqrt(jnp.mean(x * x, axis=-1, keepdims=True) + RMS_EPS) * nw_ref[...]
    o_ref[...] = _dot(y.astype(BF16), w_ref[...]).astype(o_ref.dtype)


def _norm_mm(u, acol, kdim, nw, w, tn):
    m = u.shape[0]
    n = w.shape[1]
    return pl.pallas_call(
        _norm_mm_kernel,
        grid=(m // TM, n // tn),
        in_specs=[pl.BlockSpec((TM, kdim), lambda i, j: (i, acol)),
                  pl.BlockSpec((1, kdim), lambda i, j: (0, 0)),
                  pl.BlockSpec((kdim, tn), lambda i, j: (0, j))],
        out_specs=pl.BlockSpec((TM, tn), lambda i, j: (i, j)),
        out_shape=jax.ShapeDtypeStruct((m, n), BF16),
        compiler_params=_cp("parallel", "arbitrary"),
    )(u, nw.reshape(1, kdim), w)


def _mm2_res_kernel(a1_ref, a2_ref, w1_ref, w2_ref, r_ref, g_ref, o_ref):
    acc = _dot(a1_ref[...], w1_ref[...]) + _dot(a2_ref[...], w2_ref[...])
    o_ref[...] = r_ref[...] + g_ref[0] * acc


def _mm2_res_split_kernel(a1_ref, a2_ref, w1_ref, w2_ref, rx_ref, rc_ref, g_ref, o_ref):
    acc = _dot(a1_ref[...], w1_ref[...]) + _dot(a2_ref[...], w2_ref[...])
    r = jnp.where(pl.program_id(0) < (B * L) // TM, rx_ref[...], rc_ref[...])
    o_ref[...] = r + g_ref[0] * acc


def _mm2_res_split(a1, a2, w, rx, rc, gate):
    k1 = a1.shape[1]
    nxt = rx.shape[0] // TM
    assert rc.shape[0] == TM and a2.shape[1] == k1 and w.shape == (2 * k1, D)
    tn = 512
    return pl.pallas_call(
        _mm2_res_split_kernel,
        grid=(nxt + 1, D // tn),
        in_specs=[pl.BlockSpec((TM, k1), lambda i, j: (i, 0)),
                  pl.BlockSpec((TM, k1), lambda i, j: (i, 0)),
                  pl.BlockSpec((k1, tn), lambda i, j: (0, j)),
                  pl.BlockSpec((k1, tn), lambda i, j: (1, j)),
                  pl.BlockSpec((TM, tn), lambda i, j: (jnp.minimum(i, nxt - 1), j)),
                  pl.BlockSpec((TM, tn), lambda i, j: (0, j)),
                  pl.BlockSpec((1, 1, tn), lambda i, j: (_mod_row(i, TM), 0, j))],
        out_specs=pl.BlockSpec((TM, tn), lambda i, j: (i, j)),
        out_shape=jax.ShapeDtypeStruct(((nxt + 1) * TM, D), F32),
        compiler_params=_cp("parallel", "arbitrary"),
    )(a1, a2, w, w, rx, rc, gate)


def _mm2_res(a1, a2, w, resid, gate, nrows):
    k1 = a1.shape[1]
    k2 = a2.shape[1]
    assert k1 == k2 and w.shape == (k1 + k2, D)
    tn = 512
    return pl.pallas_call(
        _mm2_res_kernel,
        grid=(nrows // TM, D // tn),
        in_specs=[pl.BlockSpec((TM, k1), lambda i, j: (i, 0)),
                  pl.BlockSpec((TM, k2), lambda i, j: (i, 0)),
                  pl.BlockSpec((k1, tn), lambda i, j: (0, j)),
                  pl.BlockSpec((k2, tn), lambda i, j: (1, j)),
                  pl.BlockSpec((TM, tn), lambda i, j: (i, j)),
                  pl.BlockSpec((1, 1, tn), lambda i, j: (_mod_row(i, TM), 0, j))],
        out_specs=pl.BlockSpec((TM, tn), lambda i, j: (i, j)),
        out_shape=jax.ShapeDtypeStruct((nrows, D), F32),
        compiler_params=_cp("parallel", "arbitrary"),
    )(a1, a2, w, w, resid, gate)


def _split3(x):
    hi = x.astype(BF16)
    r = x - hi.astype(F32)
    mid = r.astype(BF16)
    lo = (r - mid.astype(F32)).astype(BF16)
    return hi, mid, lo


def _hgrn_dir(q_raw, f_raw, v, lb, st_ref, o_ref, rev):
    c = A_CHUNK
    q = _silu(q_raw.astype(F32)) * (A_DK ** -0.5)
    f = lb + (1.0 - lb) * _sigmoid(f_raw.astype(F32))
    k = 1.0 - f
    logf = jnp.log(f)
    row = lax.broadcasted_iota(jnp.int32, (T, T), 0)
    col = lax.broadcasted_iota(jnp.int32, (T, T), 1)
    mask = ((row ^ col) < c) & ((col >= row) if rev else (col <= row))
    tri = jnp.where(mask, 1.0, 0.0).astype(BF16)
    hi, mid, lo = _split3(logf)
    cum = _dot(tri, hi) + _dot(tri, mid) + _dot(tri, lo)
    nchunk = T // c
    cum3 = cum.reshape(nchunk, c, LANE)
    ref = cum3[:, c // 2 - 1:c // 2] if rev else cum3[:, c // 2:c // 2 + 1]
    last = cum3[:, 0:1] if rev else cum3[:, c - 1:c]
    q3 = q.reshape(nchunk, c, LANE)
    k3 = k.reshape(nchunk, c, LANE)
    qs = (q3 * jnp.exp(cum3 - ref)).reshape(T, LANE).astype(BF16)
    ks = (k3 * jnp.exp(ref - cum3)).reshape(T, LANE).astype(BF16)
    qd = (q * jnp.exp(cum)).astype(BF16)
    kd = (k3 * jnp.exp(last - cum3)).reshape(T, LANE).astype(BF16)
    decay = jnp.exp(last)
    att = jnp.where(mask, _dot_nt(qs, ks), 0.0)
    o = _dot(att.astype(BF16), v)
    kv = [lax.dot_general(v[n * c:(n + 1) * c], kd[n * c:(n + 1) * c], (((0,), (0,)), ((), ())),
                          preferred_element_type=F32) for n in range(nchunk)]
    st = st_ref[...]
    o_inter = [None] * nchunk
    for n in (range(nchunk - 1, -1, -1) if rev else range(nchunk)):
        o_inter[n] = _dot_nt(qd[n * c:(n + 1) * c], st.astype(BF16))
        st = st * decay[n] + kv[n]
    st_ref[...] = st
    o_ref[...] = (o + jnp.concatenate(o_inter, axis=0)).astype(o_ref.dtype)


def _hgrn_kernel(qf, ff, vf, qb, fb, vb, lbf, lbb, of, ob, stf, stb):
    @pl.when(pl.program_id(2) == 0)
    def _():
        stf[...] = jnp.zeros_like(stf)
        stb[...] = jnp.zeros_like(stb)

    _hgrn_dir(qf[...], ff[...], vf[...], lbf[0], stf, of, False)
    _hgrn_dir(qb[...], fb[...], vb[...], lbb[0], stb, ob, True)


def _hgrn(u, lbf, lbb):
    assert LC == T
    xt = L // T
    n_rows = u.shape[0]
    hh = A_HEADS

    def blk_f(b, n):
        return jnp.where(n == 0, B * xt + b, b * xt + n - 1)

    def blk_b(b, n):
        return jnp.where(n == 0, B * xt + b, b * xt + xt - n)

    def spec(blk, cb):
        return pl.BlockSpec((T, LANE), lambda b, h, n: (blk(b, n), cb * hh + h))

    lb_spec = pl.BlockSpec((1, 1, LANE), lambda b, h, n: (h, 0, 0))
    return pl.pallas_call(
        _hgrn_kernel,
        grid=(B, hh, xt + 1),
        in_specs=[spec(blk_f, 0), spec(blk_f, 1), spec(blk_f, 3),
                  spec(blk_b, 0), spec(blk_b, 2), spec(blk_b, 3),
                  lb_spec, lb_spec],
        out_specs=[spec(blk_f, 0), spec(blk_b, 0)],
        out_shape=[jax.ShapeDtypeStruct((n_rows, A_KEY), BF16)] * 2,
        scratch_shapes=[pltpu.VMEM((LANE, LANE), F32)] * 2,
        compiler_params=_cp("parallel", "parallel", "arbitrary"),
    )(u, u, u, u, u, u, lbf, lbb)


def _hgrn_out_kernel(of_ref, ob_ref, g_ref, w_ref, y_ref):
    for h in range(A_HEADS):
        sl = slice(h * LANE, (h + 1) * LANE)
        o = of_ref[:, sl].astype(F32) + ob_ref[:, sl].astype(F32)
        o = o * lax.rsqrt(jnp.mean(o * o, axis=-1, keepdims=True) + RMS_EPS) * w_ref[...]
        y_ref[:, sl] = (o * _silu(g_ref[:, sl].astype(F32))).astype(y_ref.dtype)


def _hgrn_out(o_f, o_b, u, gnorm_w):
    n_rows = o_f.shape[0]
    blk = pl.BlockSpec((T, A_KEY), lambda i: (i, 0))
    return pl.pallas_call(
        _hgrn_out_kernel,
        grid=(n_rows // T,),
        in_specs=[blk, blk,
                  pl.BlockSpec((T, A_KEY), lambda i: (i, 4)),
                  pl.BlockSpec((1, LANE), lambda i: (0, 0))],
        out_specs=blk,
        out_shape=jax.ShapeDtypeStruct((n_rows, A_KEY), BF16),
        compiler_params=_cp("parallel"),
    )(o_f, o_b, u, gnorm_w.reshape(1, LANE))


def _rope_perm(n):
    return np.concatenate([np.arange(0, n, 2), np.arange(1, n, 2)])


def _rope_cos_sin(rot_dim):
    rows = L // GRID_W
    row = jnp.repeat(jnp.arange(rows), GRID_W).astype(F32)
    col = jnp.tile(jnp.arange(GRID_W), rows).astype(F32)
    n_freq = rot_dim // 4
    inv = ROPE_THETA ** (-jnp.arange(n_freq, dtype=F32) / n_freq)
    ang = jnp.concatenate([row[:, None] * inv, col[:, None] * inv], axis=-1)
    return jnp.cos(ang), jnp.sin(ang)


def _rope_tables_pairs(scale):
    cos, sin = _rope_cos_sin(LANE)
    zero = jnp.zeros_like(sin)
    c = jnp.repeat(cos, 2, axis=-1) * scale
    s1 = jnp.stack([-sin, zero], axis=-1).reshape(L, LANE) * scale
    s2 = jnp.stack([zero, sin], axis=-1).reshape(L, LANE) * scale
    return c, s1, s2


def _rope_tables(rot_dim, with_ctx):
    half = rot_dim // 2
    cos, sin = _rope_cos_sin(rot_dim)
    zpad = jnp.zeros((L, LANE - rot_dim), F32)
    zh = jnp.zeros((L, half), F32)
    c = jnp.concatenate([cos, cos, zpad], axis=-1)
    s1 = jnp.concatenate([-sin, zh, zpad], axis=-1)
    s2 = jnp.concatenate([zh, sin, zpad], axis=-1)
    if with_ctx:
        ident = jnp.ones((T, LANE), F32)
        zero = jnp.zeros((T, LANE), F32)
        c = jnp.concatenate([ident, c], axis=0)
        s1 = jnp.concatenate([zero, s1], axis=0)
        s2 = jnp.concatenate([zero, s2], axis=0)
    return c, s1, s2


def _rope_apply(x, c, s1, s2, half):
    return x * c + pltpu.roll(x, LANE - half, 1) * s1 + pltpu.roll(x, half, 1) * s2


def _tab_blk(i):
    xt = L // T
    return jnp.where(i < B * xt, 1 + i % xt, 0)


def _krope_kernel(x_ref, c_ref, s1_ref, s2_ref, o_ref):
    x = x_ref[...].astype(F32)
    o_ref[...] = _rope_apply(x, c_ref[...], s1_ref[...], s2_ref[...], B_ROPE // 2).astype(o_ref.dtype)


def _krope(ub, tabs):
    n_rows = ub.shape[0]
    cb = ub.shape[1] // LANE - 1
    tspec = pl.BlockSpec((T, LANE), lambda i: (_tab_blk(i), 0))
    return pl.pallas_call(
        _krope_kernel,
        grid=(n_rows // T,),
        in_specs=[pl.BlockSpec((T, LANE), lambda i: (i, cb)), tspec, tspec, tspec],
        out_specs=pl.BlockSpec((T, LANE), lambda i: (i, 0)),
        out_shape=jax.ShapeDtypeStruct((n_rows, LANE), BF16),
        compiler_params=_cp("parallel"),
    )(ub, *tabs)


MLA_KV_CHUNK = 512
MLA_TQ = 1024


def _softmax_step(q, k, v1, m, acc):
    s = _dot_nt(q, k)
    m_new = jnp.maximum(m, jnp.max(s, axis=-1, keepdims=True))
    p = jnp.exp2((s - m_new).astype(BF16))
    acc = jnp.exp2(m - m_new) * acc + _dot(p, v1)
    return m_new, acc


def _softmax_init(rows):
    return jnp.full((rows, 1), NEG_BIG, F32), jnp.zeros((rows, 2 * B_VDIM), F32)


def _softmax_finish(acc):
    return acc[:, :B_VDIM] / acc[:, B_VDIM:B_VDIM + 1]


def _mla_attn_kernel(q_ref, c_ref, s1_ref, s2_ref, kxn_ref, vx_ref, krx_ref, kcn_ref, vc_ref, krc_ref,
                     o_ref, kfx, kfc, vfx, vfc):
    @pl.when(pl.program_id(2) == 0)
    def _():
        kfx[:, :B_NOPE] = kxn_ref[...]
        kfx[:, B_NOPE:] = krx_ref[...]
        kfc[:, :B_NOPE] = kcn_ref[...]
        kfc[:, B_NOPE:] = krc_ref[...]
        vfx[:, :B_VDIM] = vx_ref[...]
        vfx[:, B_VDIM:] = jnp.ones((L, B_VDIM), BF16)
        vfc[:, :B_VDIM] = vc_ref[...]
        vfc[:, B_VDIM:] = jnp.ones((LC, B_VDIM), BF16)

    qr = _rope_apply(q_ref[:, B_NOPE:].astype(F32), c_ref[...], s1_ref[...], s2_ref[...], B_ROPE // 2)
    q = jnp.concatenate([q_ref[:, :B_NOPE], qr.astype(BF16)], axis=1)
    m, acc = _softmax_step(q, kfc[...], vfc[...], *_softmax_init(MLA_TQ))
    for n in range(L // MLA_KV_CHUNK):
        sl = slice(n * MLA_KV_CHUNK, (n + 1) * MLA_KV_CHUNK)
        m, acc = _softmax_step(q, kfx[sl, :], vfx[sl, :], m, acc)
    o_ref[...] = _softmax_finish(acc).astype(o_ref.dtype)


def _mla_ctx_kernel(q_ref, kcn_ref, vc_ref, krc_ref, o_ref):
    k = jnp.concatenate([kcn_ref[...], krc_ref[...]], axis=1)
    v1 = jnp.concatenate([vc_ref[...], jnp.ones((LC, B_VDIM), BF16)], axis=1)
    m, acc = _softmax_step(q_ref[...], k, v1, *_softmax_init(LC))
    o_ref[...] = _softmax_finish(acc).astype(o_ref.dtype)


def _mla_attn(q, kv, kr, tabs):
    xq = L // MLA_TQ
    ctx0 = B * L // LC
    tspec = pl.BlockSpec((MLA_TQ, LANE), lambda b, h, qi: (qi, 0))
    kc_spec = pl.BlockSpec((LC, LANE), lambda b, h, *_: (ctx0 + b, 2 * h))
    vc_spec = pl.BlockSpec((LC, LANE), lambda b, h, *_: (ctx0 + b, 2 * h + 1))
    krc_spec = pl.BlockSpec((LC, LANE), lambda b, h, *_: (ctx0 + b, 0))
    y_x = pl.pallas_call(
        _mla_attn_kernel,
        grid=(B, B_HEADS, xq),
        in_specs=[pl.BlockSpec((MLA_TQ, B_QPAD), lambda b, h, qi: (b * xq + qi, h)),
                  tspec, tspec, tspec,
                  pl.BlockSpec((L, LANE), lambda b, h, qi: (b, 2 * h)),
                  pl.BlockSpec((L, LANE), lambda b, h, qi: (b, 2 * h + 1)),
                  pl.BlockSpec((L, LANE), lambda b, h, qi: (b, 0)),
                  kc_spec, vc_spec, krc_spec],
        out_specs=pl.BlockSpec((MLA_TQ, B_VDIM), lambda b, h, qi: (b * xq + qi, h)),
        out_shape=jax.ShapeDtypeStruct((B * L, B_HEADS * B_VDIM), BF16),
        scratch_shapes=[pltpu.VMEM((L, B_QPAD), BF16), pltpu.VMEM((LC, B_QPAD), BF16),
                        pltpu.VMEM((L, 2 * B_VDIM), BF16), pltpu.VMEM((LC, 2 * B_VDIM), BF16)],
        compiler_params=_cp("parallel", "parallel", "arbitrary"),
    )(q, *tabs, kv, kv, kr, kv, kv, kr)
    y_c = pl.pallas_call(
        _mla_ctx_kernel,
        grid=(B, B_HEADS),
        in_specs=[pl.BlockSpec((LC, B_QPAD), lambda b, h: (ctx0 + b, h)), kc_spec, vc_spec, krc_spec],
        out_specs=pl.BlockSpec((LC, B_VDIM), lambda b, h: (b, h)),
        out_shape=jax.ShapeDtypeStruct((B * LC, B_HEADS * B_VDIM), BF16),
        compiler_params=_cp("parallel", "parallel"),
    )(q, kv, kv, kr)
    return jnp.concatenate([y_x, y_c], axis=0)


def _rope_qk_kernel(x_ref, c_ref, s1_ref, s2_ref, o_ref):
    c, s1, s2 = c_ref[...], s1_ref[...], s2_ref[...]
    for h in range(x_ref.shape[1] // LANE):
        sl = slice(h * LANE, (h + 1) * LANE)
        o_ref[:, sl] = _rope_apply(x_ref[:, sl].astype(F32), c, s1, s2, 1).astype(o_ref.dtype)


def _rope_qk(ux):
    width = C_Q + C_KV
    tn = C_KV
    xt = L // T
    nq = C_Q // tn
    tabs = [jnp.concatenate([tq, tk], axis=0)
            for tq, tk in zip(_rope_tables_pairs(HEAD_DIM ** -0.5), _rope_tables_pairs(1.0))]
    tspec = pl.BlockSpec((T, LANE), lambda i, j: (i % xt + jnp.where(j < nq, 0, xt), 0))
    return pl.pallas_call(
        _rope_qk_kernel,
        grid=(B * xt, width // tn),
        in_specs=[pl.BlockSpec((T, tn), lambda i, j: (i, j)), tspec, tspec, tspec],
        out_specs=pl.BlockSpec((T, tn), lambda i, j: (i, j)),
        out_shape=jax.ShapeDtypeStruct((B * L, width), BF16),
        compiler_params=_cp("parallel", "parallel"),
    )(ux, *tabs)


def _win_kernel(sink_ref, q_ref, kx_ref, vx_ref, kc_ref, vc_ref, o_ref):
    g = pl.program_id(1)
    t = pl.program_id(2)
    span = 2 * T
    start = pl.multiple_of(jnp.clip(t * T - C_WINDOW, 0, L - span), C_WINDOW)
    kw = kx_ref[pl.ds(start, span), :]
    vw = vx_ref[pl.ds(start, span), :]
    qpos = t * T + lax.broadcasted_iota(jnp.int32, (T, span), 0)
    kpos = start + lax.broadcasted_iota(jnp.int32, (T, span), 1)
    valid = jnp.abs(qpos - kpos) <= C_WINDOW
    kc = kc_ref[...]
    vc = vc_ref[...]
    for r in range(C_GROUP):
        sl = slice(r * HEAD_DIM, (r + 1) * HEAD_DIM)
        q = q_ref[:, sl]
        sw = jnp.where(valid, _dot_nt(q, kw), NEG_BIG)
        sc = _dot_nt(q, kc)
        sk = sink_ref[g * C_GROUP + r]
        m = jnp.maximum(jnp.maximum(jnp.max(sw, axis=-1, keepdims=True),
                                    jnp.max(sc, axis=-1, keepdims=True)), sk)
        pw = jnp.exp(sw - m)
        pc = jnp.exp(sc - m)
        l = jnp.sum(pw, axis=-1, keepdims=True) + jnp.sum(pc, axis=-1, keepdims=True) + jnp.exp(sk - m)
        o = _dot(pw.astype(BF16), vw) + _dot(pc.astype(BF16), vc)
        o_ref[:, sl] = (o / l).astype(o_ref.dtype)


def _win_attn(qk, ux, sink):
    assert T == 2 * C_WINDOW
    xt = L // T
    ctx0 = B * L // LC
    kcol = C_Q // LANE
    vcol = (C_Q + C_KV) // LANE
    gw = C_GROUP * HEAD_DIM
    return pl.pallas_call(
        _win_kernel,
        grid=(B, C_KV_HEADS, xt),
        in_specs=[pl.BlockSpec(memory_space=pltpu.SMEM),
                  pl.BlockSpec((T, gw), lambda b, g, t: (b * xt + t, g)),
                  pl.BlockSpec((L, LANE), lambda b, g, t: (b, kcol + g)),
                  pl.BlockSpec((L, LANE), lambda b, g, t: (b, vcol + g)),
                  pl.BlockSpec((LC, LANE), lambda b, g, t: (ctx0 + b, kcol + g)),
                  pl.BlockSpec((LC, LANE), lambda b, g, t: (ctx0 + b, vcol + g))],
        out_specs=pl.BlockSpec((T, gw), lambda b, g, t: (b * xt + t, g)),
        out_shape=jax.ShapeDtypeStruct((B * L, C_Q), BF16),
        compiler_params=_cp("parallel", "parallel", "arbitrary"),
    )(sink, qk, qk, ux, ux, ux)


CONV_TN = 512
SUBLANE = 8


def _conv_kernel(bg_ref, cg_ref, hh_ref, cgp_ref, hhp_ref, cgn_ref, hhn_ref, w_ref, o_ref):
    xt = L // T
    t = pl.program_id(0) % xt
    z = cg_ref[...].astype(F32) * hh_ref[...].astype(F32)
    zp = cgp_ref[SUBLANE - 1:SUBLANE, :].astype(F32) * hhp_ref[SUBLANE - 1:SUBLANE, :].astype(F32)
    zn = cgn_ref[0:1, :].astype(F32) * hhn_ref[0:1, :].astype(F32)
    zp = jnp.where(t > 0, zp, 0.0)
    zn = jnp.where(t < xt - 1, zn, 0.0)
    rows = lax.broadcasted_iota(jnp.int32, (T, CONV_TN), 0)
    z_prev = jnp.where(rows == 0, zp, pltpu.roll(z, 1, 0))
    z_next = jnp.where(rows == T - 1, zn, pltpu.roll(z, T - 1, 0))
    y = w_ref[0:1, :] * z_prev + w_ref[1:2, :] * z + w_ref[2:3, :] * z_next
    o_ref[...] = (bg_ref[...].astype(F32) * y).astype(o_ref.dtype)


def _conv(ux, conv_w):
    xt = L // T
    nx = B * xt
    c0 = C_IN // CONV_TN
    nc = D_CH // CONV_TN
    rb = T // SUBLANE

    def main(cb):
        return pl.BlockSpec((T, CONV_TN), lambda i, j: (i, c0 + cb * nc + j))

    def prev(cb):
        return pl.BlockSpec((SUBLANE, CONV_TN), lambda i, j: (jnp.maximum(i * rb - 1, 0), c0 + cb * nc + j))

    def nxt(cb):
        return pl.BlockSpec((SUBLANE, CONV_TN),
                            lambda i, j: (jnp.minimum((i + 1) * rb, nx * rb - 1), c0 + cb * nc + j))

    return pl.pallas_call(
        _conv_kernel,
        grid=(nx, nc),
        in_specs=[main(0), main(1), main(2), prev(1), prev(2), nxt(1), nxt(2),
                  pl.BlockSpec((D_CONV, CONV_TN), lambda i, j: (0, j))],
        out_specs=pl.BlockSpec((T, CONV_TN), lambda i, j: (i, j)),
        out_shape=jax.ShapeDtypeStruct((B * L, D_CH), BF16),
        compiler_params=_cp("parallel", "parallel"),
    )(ux, ux, ux, ux, ux, ux, ux, conv_w)


def _router_kernel(h_ref, wr_ref, b_ref, i_ref, g_ref):
    tm = h_ref.shape[0]
    scores = _sigmoid(_dot_nt(wr_ref[...], h_ref[...]))
    biased = scores + b_ref[...]
    ng = N_GROUPS
    sl = [biased[j * ng:(j + 1) * ng] for j in range(GROUP_SIZE)]
    sc = [scores[j * ng:(j + 1) * ng] for j in range(GROUP_SIZE)]
    m1 = sl[0]
    m2 = jnp.full_like(m1, -jnp.inf)
    for v in sl[1:]:
        m2 = jnp.maximum(m2, jnp.minimum(m1, v))
        m1 = jnp.maximum(m1, v)
    gs = m1 + m2
    gidx = lax.broadcasted_iota(jnp.int32, (ng, tm), 0)
    rank = jnp.zeros((ng, tm), jnp.int32)
    for k in range(1, ng):
        r = pltpu.roll(gs, k, 0)
        rank += jnp.where((r > gs) | ((r == gs) & (gidx >= k)), 1, 0)
    keep = rank < TOPK_GROUPS
    ms = [jnp.where(keep, v, -jnp.inf) for v in sl]
    rolled = [[ms[j] if k == 0 else pltpu.roll(ms[j], k, 0) for k in range(ng)] for j in range(GROUP_SIZE)]
    tot = jnp.zeros((ng, tm), F32)
    picked = []
    ranks = []
    for j in range(GROUP_SIZE):
        cnt = jnp.zeros((ng, tm), jnp.int32)
        for jp in range(GROUP_SIZE):
            for k in range(ng):
                r = rolled[jp][k]
                if k == 0:
                    if jp == j:
                        continue
                    beats = (r >= ms[j]) if jp < j else (r > ms[j])
                else:
                    beats = (r > ms[j]) | ((r == ms[j]) & (gidx >= k))
                cnt += jnp.where(beats, 1, 0)
        w = jnp.where(cnt < TOP_K, sc[j], 0.0)
        picked.append(w)
        ranks.append(cnt)
        tot = tot + w
    tot = jnp.sum(tot, axis=0, keepdims=True)
    wn = [p / tot * ROUTE_SCALE for p in picked]
    eid = [(gidx * GROUP_SIZE + j).astype(F32) for j in range(GROUP_SIZE)]
    for k in range(TOP_K):
        ik = jnp.zeros((ng, tm), F32)
        wk = jnp.zeros((ng, tm), F32)
        for j in range(GROUP_SIZE):
            hit = ranks[j] == k
            ik += jnp.where(hit, eid[j], 0.0)
            wk += jnp.where(hit, wn[j], 0.0)
        i_ref[k:k + 1, :] = jnp.sum(ik, axis=0, keepdims=True).astype(jnp.int32)
        g_ref[k:k + 1, :] = jnp.sum(wk, axis=0, keepdims=True)


def _router(h, wr_t, bias, nrows):
    out = pl.BlockSpec((TOP_K, TM), lambda i: (0, i))
    return pl.pallas_call(
        _router_kernel,
        grid=(nrows // TM,),
        in_specs=[pl.BlockSpec((TM, D), lambda i: (i, 0)),
                  pl.BlockSpec((N_EXPERTS, D), lambda i: (0, 0)),
                  pl.BlockSpec((N_EXPERTS, 1), lambda i: (0, 0))],
        out_specs=[out, out],
        out_shape=[jax.ShapeDtypeStruct((TOP_K, nrows), jnp.int32),
                   jax.ShapeDtypeStruct((TOP_K, nrows), F32)],
        compiler_params=_cp("parallel"),
    )(h, wr_t, bias)


def _router_order():
    r = np.arange(N_EXPERTS)
    return (r % N_GROUPS) * GROUP_SIZE + r // N_GROUPS


def _glu_kernel(x_ref, wg_ref, wu_ref, wd_ref, o_ref):
    x = x_ref[...]
    hid = _silu(_dot(x, wg_ref[...])) * _dot(x, wu_ref[...])
    o_ref[...] = _dot(hid.astype(BF16), wd_ref[...]).astype(o_ref.dtype)


def _glu(h, wg, wu, wd, nrows):
    return pl.pallas_call(
        _glu_kernel,
        grid=(nrows // TM,),
        in_specs=[pl.BlockSpec((TM, D), lambda i: (i, 0)),
                  pl.BlockSpec((D, D_EXPERT), lambda i: (0, 0)),
                  pl.BlockSpec((D, D_EXPERT), lambda i: (0, 0)),
                  pl.BlockSpec((D_EXPERT, D), lambda i: (0, 0))],
        out_specs=pl.BlockSpec((TM, D), lambda i: (i, 0)),
        out_shape=jax.ShapeDtypeStruct((nrows, D), BF16),
        compiler_params=_cp("parallel"),
    )(h, wg, wu, wd)


TME = 256
TD = 512
TC = 128


def _route_plan(idx, nrows):
    nt = (nrows * TOP_K) // TME + N_EXPERTS
    onehot = idx[:, :, None] == jnp.arange(N_EXPERTS, dtype=jnp.int32)[None, None, :]
    cs = jnp.cumsum(jnp.any(onehot, axis=1).astype(jnp.int32), axis=0)
    counts = cs[-1]
    padded = ((counts + TME - 1) // TME) * TME
    ends = jnp.cumsum(padded)
    off = ends - padded
    dest = jnp.sum(jnp.where(onehot, (off[None, :] + cs - 1)[:, None, :], 0), axis=2).astype(jnp.int32)
    tiles = jnp.arange(nt, dtype=jnp.int32)
    used = ends[-1] // TME
    t_expert = jnp.sum((ends // TME)[None, :] <= tiles[:, None], axis=1)
    t_expert = jnp.minimum(t_expert, N_EXPERTS - 1).astype(jnp.int32)
    t_valid = (tiles < used).astype(jnp.int32)
    t_first = ((tiles * TME == off[t_expert]) & (tiles < used)).astype(jnp.int32)
    t_src = jnp.minimum(tiles, used - 1).astype(jnp.int32)
    return dict(nt=nt, dest=dest, used=used.reshape(1).astype(jnp.int32),
                pad_start=(off + counts).astype(jnp.int32),
                pad_cnt=(padded - counts).astype(jnp.int32),
                t_expert=t_expert, t_valid=t_valid, t_first=t_first, t_src=t_src)


def _slab(ref, row):
    return ref.at[pl.ds(pl.multiple_of(row * NP, NP), NP), :]


def _dispatch_kernel(ps_ref, pc_ref, used_ref, dest_ref, hp_ref, xs_ref, sem, *, nt):
    i = pl.program_id(0)

    def body(j, c):
        for k in range(TOP_K):
            pltpu.make_async_copy(_slab(hp_ref, j), _slab(xs_ref, dest_ref[0, 0, j * TOP_K + k]),
                                  sem).start(priority=k % 2)
        return c

    lax.fori_loop(0, TD, body, 0)
    for k in range(TOP_K):
        pltpu.make_async_copy(hp_ref, xs_ref.at[pl.ds(0, TD * NP), :], sem).wait()

    @pl.when(i == 0)
    def _():
        def pad_copy(e):
            n = pc_ref[e] * NP
            return pltpu.make_async_copy(hp_ref.at[pl.ds(0, n), :],
                                         xs_ref.at[pl.ds(pl.multiple_of(ps_ref[e] * NP, NP), n), :], sem)

        def tail_copy(t):
            n = TME * NP
            return pltpu.make_async_copy(hp_ref.at[pl.ds(0, n), :],
                                         xs_ref.at[pl.ds(pl.multiple_of(t * n, n), n), :], sem)

        def over(lo, hi, copy, pred):
            def start(e, c):
                @pl.when(pred(e))
                def _():
                    copy(e).start()
                return c

            def wait(e, c):
                @pl.when(pred(e))
                def _():
                    copy(e).wait()
                return c

            lax.fori_loop(lo, hi, start, 0)
            lax.fori_loop(lo, hi, wait, 0)

        over(0, N_EXPERTS, pad_copy, lambda e: pc_ref[e] > 0)
        over(0, nt, tail_copy, lambda t: t >= used_ref[0])


def _dispatch(hp, plan, nrows):
    dest = plan["dest"].reshape(nrows // TD, 1, TD * TOP_K)
    return pl.pallas_call(
        functools.partial(_dispatch_kernel, nt=plan["nt"]),
        grid_spec=pltpu.PrefetchScalarGridSpec(
            num_scalar_prefetch=3,
            grid=(nrows // TD,),
            in_specs=[pl.BlockSpec((1, 1, TD * TOP_K), lambda i, ps, pc, us: (i, 0, 0), memory_space=pltpu.SMEM),
                      pl.BlockSpec((TD * NP, LANE), lambda i, ps, pc, us: (i, 0))],
            out_specs=pl.BlockSpec(memory_space=pl.ANY),
            scratch_shapes=[pltpu.SemaphoreType.DMA(())]),
        out_shape=jax.ShapeDtypeStruct((plan["nt"] * TME * NP, LANE), jnp.uint32),
        compiler_params=pltpu.CompilerParams(dimension_semantics=("arbitrary",), disable_bounds_checks=True,
                                             vmem_limit_bytes=VMEM_LIMIT_BYTES),
    )(plan["pad_start"], plan["pad_cnt"], plan["used"], dest, hp)


def _expert_kernel(te_ref, tv_ref, tf_ref, ts_ref, x_ref, wg_ref, wu_ref, wd_ref, o_ref, wgu_b, wd_b):
    i = pl.program_id(0)

    @pl.when(tv_ref[i] == 0)
    def _():
        o_ref[...] = jnp.zeros_like(o_ref)

    @pl.when(tv_ref[i] == 1)
    def _():
        @pl.when(tf_ref[i] == 1)
        def _():
            wgu_b[:, :D_EXPERT] = wg_ref[0, 0].astype(BF16)
            wgu_b[:, D_EXPERT:] = wu_ref[0, 0].astype(BF16)
            wd_b[...] = wd_ref[0, 0].astype(BF16)

        pieces = []
        for p in range(NP):
            lo, hi = _unpack2(x_ref[pl.ds(p, TME, stride=NP), :])
            pieces += [lo.astype(BF16), hi.astype(BF16)]
        a = _dot(jnp.concatenate(pieces, axis=1), wgu_b[...])
        hid = _silu(a[:, :D_EXPERT]) * a[:, D_EXPERT:]
        y = _dot(hid.astype(BF16), wd_b[...])
        for p in range(NP):
            o_ref[pl.ds(p, TME, stride=NP), :] = _pack2(y[:, p * PACK:p * PACK + LANE],
                                                        y[:, p * PACK + LANE:(p + 1) * PACK])


def _experts(xs, plan, wg, wu, wd, layer):
    nt = plan["nt"]
    rows = pl.BlockSpec((TME * NP, LANE), lambda i, te, tv, tf, ts: (i, 0))
    return pl.pallas_call(
        _expert_kernel,
        grid_spec=pltpu.PrefetchScalarGridSpec(
            num_scalar_prefetch=4,
            grid=(nt,),
            in_specs=[pl.BlockSpec((TME * NP, LANE), lambda i, te, tv, tf, ts: (ts[i], 0)),
                      pl.BlockSpec((1, 1, D, D_EXPERT), lambda i, te, tv, tf, ts: (layer, te[i], 0, 0)),
                      pl.BlockSpec((1, 1, D, D_EXPERT), lambda i, te, tv, tf, ts: (layer, te[i], 0, 0)),
                      pl.BlockSpec((1, 1, D_EXPERT, D), lambda i, te, tv, tf, ts: (layer, te[i], 0, 0))],
            out_specs=rows,
            scratch_shapes=[pltpu.VMEM((D, 2 * D_EXPERT), BF16), pltpu.VMEM((D_EXPERT, D), BF16)]),
        out_shape=jax.ShapeDtypeStruct((nt * TME * NP, LANE), jnp.uint32),
        compiler_params=pltpu.CompilerParams(dimension_semantics=("arbitrary",),
                                             vmem_limit_bytes=EXPERT_VMEM_LIMIT_BYTES),
    )(plan["t_expert"], plan["t_valid"], plan["t_first"], plan["t_src"], xs, wg, wu, wd)


def _gather_pitch():
    tiles = -(-NP // SUBLANE)
    return (tiles + 1 + tiles % 2) * SUBLANE


def _combine_kernel(dcur_ref, dnxt_ref, w_ref, r_ref, sh_ref, g_ref, ys_ref, o_ref, buf0, buf1, sem):
    i = pl.program_id(0)
    last = pl.num_programs(0) - 1

    pitch = _gather_pitch()

    def issue(d_ref, buf, slot):
        def body(j, c):
            for k in range(TOP_K):
                dst = buf.at[pl.ds(pl.multiple_of((k * TC + j) * pitch, SUBLANE), NP), :]
                pltpu.make_async_copy(_slab(ys_ref, d_ref[0, 0, j * TOP_K + k]), dst,
                                      sem.at[slot]).start(priority=k % 2)
            return c

        lax.fori_loop(0, TC, body, 0)

    def drain(buf, slot):
        nrow = TC * TOP_K * NP
        pltpu.make_async_copy(ys_ref.at[pl.ds(0, nrow), :], buf.at[pl.ds(0, nrow), :], sem.at[slot]).wait()

    def reduce(buf):
        gates = [w_ref[:, k:k + 1] for k in range(TOP_K)]
        gmod = g_ref[0]
        for p in range(NP):
            lo_sl = slice(p * PACK, p * PACK + LANE)
            hi_sl = slice(p * PACK + LANE, (p + 1) * PACK)
            lo_acc = sh_ref[:, lo_sl].astype(F32)
            hi_acc = sh_ref[:, hi_sl].astype(F32)
            for k in range(TOP_K):
                lo, hi = _unpack2(buf[pl.ds(k * TC * pitch + p, TC, stride=pitch), :])
                lo_acc += gates[k] * lo
                hi_acc += gates[k] * hi
            o_ref[:, lo_sl] = r_ref[:, lo_sl] + gmod[:, lo_sl] * lo_acc
            o_ref[:, hi_sl] = r_ref[:, hi_sl] + gmod[:, hi_sl] * hi_acc

    @pl.when(i == 0)
    def _():
        issue(dcur_ref, buf0, 0)

    @pl.when(i % 2 == 0)
    def _():
        @pl.when(i < last)
        def _():
            issue(dnxt_ref, buf1, 1)

        drain(buf0, 0)
        reduce(buf0)

    @pl.when(i % 2 == 1)
    def _():
        @pl.when(i < last)
        def _():
            issue(dnxt_ref, buf0, 0)

        drain(buf1, 1)
        reduce(buf1)


def _combine(resid, ys, shared, w, plan, gate, nrows):
    nsteps = nrows // TC
    dest = plan["dest"].reshape(nsteps, 1, TC * TOP_K)
    blk = pl.BlockSpec((TC, D), lambda i: (i, 0))
    return pl.pallas_call(
        _combine_kernel,
        grid=(nsteps,),
        in_specs=[pl.BlockSpec((1, 1, TC * TOP_K), lambda i: (i, 0, 0), memory_space=pltpu.SMEM),
                  pl.BlockSpec((1, 1, TC * TOP_K), lambda i: (jnp.minimum(i + 1, nsteps - 1), 0, 0),
                               memory_space=pltpu.SMEM),
                  pl.BlockSpec((TC, TOP_K), lambda i: (i, 0)),
                  blk, blk,
                  pl.BlockSpec((1, 1, D), lambda i: (_mod_row(i, TC), 0, 0)),
                  pl.BlockSpec(memory_space=pl.ANY)],
        out_specs=blk,
        out_shape=jax.ShapeDtypeStruct((nrows, D), F32),
        scratch_shapes=[pltpu.VMEM((TC * TOP_K * _gather_pitch(), LANE), jnp.uint32),
                        pltpu.VMEM((TC * TOP_K * _gather_pitch(), LANE), jnp.uint32),
                        pltpu.SemaphoreType.DMA((2,))],
        compiler_params=pltpu.CompilerParams(dimension_semantics=("arbitrary",),
                                             vmem_limit_bytes=VMEM_LIMIT_BYTES, disable_bounds_checks=True),
    )(dest, dest, w, resid, shared, gate, ys)


def _moe(s, h, hp, gate, nrows, layer, router_w, router_bias, wg, wu, wd, sg, su, sd):
    order = _router_order()
    wr_t = router_w.T[order].astype(BF16)
    bias = router_bias[order].reshape(N_EXPERTS, 1).astype(F32)
    idx_t, w_t = _router(h, wr_t, bias, nrows)
    plan = _route_plan(idx_t.T, nrows)
    xs = _dispatch(hp, plan, nrows)
    ys = _experts(xs, plan, wg, wu, wd, layer)
    shared = _glu(h, sg.astype(BF16), su.astype(BF16), sd.astype(BF16), nrows)
    return _combine(s, ys, shared, w_t.T, plan, gate, nrows)


def _mla_in_weights(w_in):
    wb = w_in[:, A_IN:]
    kr = wb[:, B_Q_RANK + B_KV_RANK:][:, _rope_perm(B_ROPE)]
    pad = jnp.zeros((D, LANE - B_ROPE), w_in.dtype)
    return jnp.concatenate([wb[:, :B_Q_RANK + B_KV_RANK], kr, pad], axis=1).astype(BF16)


def _mla_uq_weights(w_uq):
    scale = (B_NOPE + B_ROPE) ** -0.5 * np.log2(np.e)
    w = (w_uq * scale).reshape(B_Q_RANK, B_HEADS, B_NOPE + B_ROPE)
    rope = w[:, :, B_NOPE:][:, :, _rope_perm(B_ROPE)]
    pad = jnp.zeros((B_Q_RANK, B_HEADS, B_QPAD - B_NOPE - B_ROPE), w.dtype)
    return jnp.concatenate([w[:, :, :B_NOPE], rope, pad], axis=-1).reshape(B_Q_RANK, B_HEADS * B_QPAD).astype(BF16)


def kernel(x, c, ctx, c_ctx, ada_w, ada_b, norm_mix_w, norm_ffn_w, ab_w_in, ab_w_out, hgrn_lb_logits, hgrn_gnorm_w, mla_q_norm_w, mla_w_uq, mla_kv_norm_w, mla_w_ukv, cd_w_in, cd_w_out, gqa_sink, conv_w, router_w, router_bias, exp_w_gate, exp_w_up, exp_w_down, sh_w_gate, sh_w_up, sh_w_down, final_norm_w):
    nx = B * L
    n = nx + B * LC
    x2 = x.reshape(nx, D)
    c2 = ctx.reshape(B * LC, D)
    lb_all = jnp.cumsum(jax.nn.softmax(hgrn_lb_logits.astype(F32), axis=0), axis=0)
    cond = jnp.concatenate([c, c_ctx[None, :], jnp.zeros((8 - B - 1, D), F32)], axis=0)

    def modulation(i):
        m = _ada(cond, ada_w, ada_b[i].reshape(1, N_MOD * D), i).reshape(8, N_MOD, D)
        return [m[:B + 1, k, :].reshape(B + 1, 1, D) for k in range(N_MOD)]

    def moe_args(i):
        return (i, router_w[i], router_bias[i], exp_w_gate, exp_w_up, exp_w_down,
                sh_w_gate[i], sh_w_up[i], sh_w_down[i])

    mod = modulation(0)
    h = _norm_mod_split(x2, c2, norm_mix_w[0], mod[1], mod[0])
    ua = _matmul(h, ab_w_in[0][:, :A_IN].astype(BF16), 1024)
    ub = _matmul(h, _mla_in_weights(ab_w_in[0]), B_Q_RANK + B_KV_RANK + LANE)
    lb = lb_all[0].reshape(2, A_HEADS, 1, A_DK)
    o_f, o_b = _hgrn(ua, lb[0], lb[1])
    ya = _hgrn_out(o_f, o_b, ua, hgrn_gnorm_w[0])
    tabs = _rope_tables(B_ROPE, True)
    q = _norm_mm(ub, 0, B_Q_RANK, mla_q_norm_w[0], _mla_uq_weights(mla_w_uq[0]), 1024)
    kv = _norm_mm(ub, B_Q_RANK // B_KV_RANK, B_KV_RANK, mla_kv_norm_w[0], mla_w_ukv[0].astype(BF16), 1024)
    kr = _krope(ub, tabs)
    yb = _mla_attn(q, kv, kr, tuple(t[T:] for t in tabs))
    s = _mm2_res_split(ya, yb, ab_w_out[0].astype(BF16), x2, c2, mod[2])
    h, hp = _norm_mod_pack(s, norm_ffn_w[0], mod[4], mod[3], n)
    s = _moe(s, h, hp, mod[5], n, *moe_args(0))

    mod = modulation(1)
    h = _norm_mod(s, norm_mix_w[1], mod[1], mod[0], n)
    ux = _matmul(h, cd_w_in[0].astype(BF16), 1024)
    qk = _rope_qk(ux)
    att = _win_attn(qk, ux, gqa_sink[0].astype(F32))
    cv = _conv(ux, conv_w[0])
    s = _mm2_res(att, cv, cd_w_out[0].astype(BF16), s, mod[2], nx)
    h, hp = _norm_mod_pack(s, norm_ffn_w[1], mod[4], mod[3], nx)
    s = _moe(s, h, hp, mod[5], nx, *moe_args(1))
    return _rms(s, final_norm_w).reshape(B, L, D)
```
